```python
import math, functools
import jax, jax.numpy as jnp
from jax import lax
import numpy as np

D_MODEL = 1024
BATCH = 16
SEQ = 4096
DEPTH = 1
DEC_BATCH = 128
DEC_SEQ = 8
PAST_LEN = 8192
PAGE_SIZE = 128

D_MIX = D_MODEL
HD_A = 64
H_A = (D_MIX // 2) // HD_A
D_A = H_A * HD_A
H_B = 4
DV_B = (D_MIX - D_A) // H_B
DK_B = DV_B // 2
D_B = H_B * DV_B
GATE_RANK = 16
GATE_TAU = 16.0
GLA_CHUNK = 64
DILATED = ((128, 1), (512, 4), (2048, 16))
MAX_WINDOW = 2048
BAND_BLOCK = 128
ROPE_THETA = 10000.0
EPS = 1e-6
SPLIT_SIZES = (D_A, D_A, D_A, H_B * DK_B, H_B * DK_B, D_B, GATE_RANK, D_A, D_B)
D_IN = sum(SPLIT_SIZES)

kernel_name = 'hymba_dilated_swa_gla_step'


def split_points():
    return [int(i) for i in np.cumsum(SPLIT_SIZES)[:-1]]


def rmsnorm(x, g):
    xf = x.astype(jnp.float32)
    y = xf * lax.rsqrt(jnp.mean(xf * xf, axis=-1, keepdims=True) + EPS)
    return (y * g.astype(jnp.float32)).astype(x.dtype)


def rope(x, pos):
    half = x.shape[-1] // 2
    inv_freq = ROPE_THETA ** (-jnp.arange(half, dtype=jnp.float32) / half)
    ang = pos.astype(jnp.float32)[:, None] * inv_freq[None, :]
    cos = jnp.cos(ang)[None, :, None, :]
    sin = jnp.sin(ang)[None, :, None, :]
    xf = x.astype(jnp.float32)
    x1, x2 = xf[..., :half], xf[..., half:]
    return jnp.concatenate([x1 * cos - x2 * sin, x2 * cos + x1 * sin], axis=-1).astype(x.dtype)


def softmax_attend(s, v, out_spec):
    m = jnp.max(s, axis=-1, keepdims=True)
    p = jnp.exp(s - m)
    den = jnp.sum(p, axis=-1)
    o = jnp.einsum(out_spec, p, v) / den[..., None]
    return o, m[..., 0] + jnp.log(den)


def dilated_branch_prompt(q, k, v, window, dil):
    B, S, H, hd = q.shape
    n_keys = window // dil + 1
    lm = S // dil
    nb = -(-lm // BAND_BLOCK)
    lp = nb * BAND_BLOCK

    def by_residue(a):
        a = a.reshape(B, lm, dil, H, hd).transpose(0, 2, 1, 3, 4)
        a = jnp.pad(a, ((0, 0), (0, 0), (0, lp - lm), (0, 0), (0, 0)))
        return a.reshape(B, dil, nb, BAND_BLOCK, H, hd)

    def with_prev(a):
        prev = jnp.concatenate([jnp.zeros_like(a[:, :, :1]), a[:, :, :-1]], axis=2)
        return jnp.concatenate([prev, a], axis=3)

    qr = by_residue(q)
    kk = with_prev(by_residue(k))
    vv = with_prev(by_residue(v))
    qi = jnp.arange(BAND_BLOCK)[:, None] + BAND_BLOCK
    ki = jnp.arange(2 * BAND_BLOCK)[None, :]
    dist = qi - ki
    key_m = (jnp.arange(nb)[:, None, None] - 1) * BAND_BLOCK + ki[None]
    valid = (dist >= 0) & (dist < n_keys) & (key_m >= 0)
    s = jnp.einsum('brnqhd,brnkhd->brnqhk', qr, kk) * (hd ** -0.5)
    s = jnp.where(valid[None, None, :, :, None, :], s, -jnp.inf)
    o, lse = softmax_attend(s, vv, 'brnqhk,brnkhd->brnqhd')
    o = o.reshape(B, dil, lp, H, hd)[:, :, :lm].transpose(0, 2, 1, 3, 4).reshape(B, S, H, hd)
    lse = lse.reshape(B, dil, lp, H)[:, :, :lm].transpose(0, 2, 1, 3).reshape(B, S, H)
    return o, lse


def dilated_branch_sample(q, k_buf, v_buf, window, dil):
    T = q.shape[1]
    W = k_buf.shape[1] - T
    hd = q.shape[-1]
    n_keys = window // dil + 1
    idx = W + jnp.arange(T)[:, None] - dil * jnp.arange(n_keys)[None, :]
    valid = idx >= 0
    idx = jnp.maximum(idx, 0)
    kg = k_buf[:, idx]
    vg = v_buf[:, idx]
    s = jnp.einsum('bthd,btnhd->bthn', q, kg) * (hd ** -0.5)
    s = jnp.where(valid[None, :, None, :], s, -jnp.inf)
    return softmax_attend(s, vg, 'bthn,btnhd->bthd')


def dilated_mixture(results):
    outs = jnp.stack([r[0] for r in results], axis=0)
    lses = jnp.stack([r[1] for r in results], axis=0)
    alpha = jax.nn.softmax(lses, axis=0)
    return jnp.einsum('gblh,gblhd->blhd', alpha, outs)


def attend_prompt(q, k, v):
    qf, kf, vf = (a.astype(jnp.float32) for a in (q, k, v))
    return dilated_mixture([dilated_branch_prompt(qf, kf, vf, w, d) for w, d in DILATED])


def attend_sample(k_past, v_past, q, k, v):
    qf = q.astype(jnp.float32)
    k_buf = jnp.concatenate([k_past.astype(jnp.float32), k.astype(jnp.float32)], axis=1)
    v_buf = jnp.concatenate([v_past.astype(jnp.float32), v.astype(jnp.float32)], axis=1)
    return dilated_mixture([dilated_branch_sample(qf, k_buf, v_buf, w, d) for w, d in DILATED])


def gla_chunked(q, k, v, log_a, s0):
    B, L, H, dk = q.shape
    dv = v.shape[-1]
    c = math.gcd(L, GLA_CHUNK)
    n = L // c

    def to_chunks(a):
        return a.astype(jnp.float32).reshape(B, n, c, H, a.shape[-1]).transpose(1, 0, 3, 2, 4)

    causal = jnp.tril(jnp.ones((c, c), dtype=bool))

    def step(state, xs):
        qc, kc, vc, gc = xs
        b = jnp.cumsum(gc, axis=2)
        diff = b[:, :, :, None, :] - b[:, :, None, :, :]
        decay = jnp.exp(jnp.where(causal[:, :, None], diff, -jnp.inf))
        scores = jnp.einsum('bhtk,bhsk,bhtsk->bhts', qc, kc, decay)
        o = (jnp.einsum('bhts,bhsv->bhtv', scores, vc)
             + jnp.einsum('bhtk,bhkv->bhtv', qc * jnp.exp(b), state))
        b_last = b[:, :, -1]
        state = (jnp.exp(b_last)[..., None] * state
                 + jnp.einsum('bhsk,bhsv->bhkv', kc * jnp.exp(b_last[:, :, None] - b), vc))
        return state, o

    s_fin, o = lax.scan(step, s0.astype(jnp.float32),
                        (to_chunks(q), to_chunks(k), to_chunks(v), to_chunks(log_a)))
    return o.transpose(1, 0, 3, 2, 4).reshape(B, L, H, dv), s_fin


def mixer_layer(x, pos, attend, s0, norm_w, w_in, w_gate_up, b_gate, q_norm_w, k_norm_w,
                gla_norm_w, w_out):
    B, L, _ = x.shape
    h = rmsnorm(x, norm_w)
    z = h @ w_in
    q, k, v, qb, kb, vb, g_lr, gate_a, gate_b = jnp.split(z, split_points(), axis=-1)
    heads = lambda a, nh: a.reshape(B, L, nh, -1)
    q = rope(rmsnorm(heads(q, H_A), q_norm_w), pos)
    k = rope(rmsnorm(heads(k, H_A), k_norm_w), pos)
    v = heads(v, H_A)
    o_a = attend(q, k, v)
    log_a = jax.nn.log_sigmoid((g_lr @ w_gate_up + b_gate).astype(jnp.float32)) / GATE_TAU
    o_b, s_fin = gla_chunked(heads(qb, H_B) * (DK_B ** -0.5), heads(kb, H_B), heads(vb, H_B),
                             heads(log_a, H_B), s0)
    o_b = rmsnorm(o_b, gla_norm_w)
    mixed = jnp.concatenate(
        [o_a.reshape(B, L, D_A) * jax.nn.silu(gate_a.astype(jnp.float32)),
         o_b.reshape(B, L, D_B) * jax.nn.silu(gate_b.astype(jnp.float32))], axis=-1)
    y = x + mixed.astype(x.dtype) @ w_out
    return y, k, v, s_fin


def setup_inputs(seed: int = 0) -> dict:
    key = jax.random.key(seed)
    ks = jax.random.split(key, 13)
    win = min(MAX_WINDOW, PAST_LEN)
    nrm = lambda kk, shape, scale: scale * jax.random.normal(kk, shape, dtype=jnp.float32)
    return {
        'x_prompt': nrm(ks[0], (BATCH, SEQ, D_MODEL), 1.0),
        'x_sample': nrm(ks[1], (DEC_BATCH, DEC_SEQ, D_MODEL), 1.0),
        'cache_k_win': nrm(ks[2], (DEPTH, DEC_BATCH, win, H_A, HD_A), 1.0),
        'cache_v_win': nrm(ks[3], (DEPTH, DEC_BATCH, win, H_A, HD_A), 1.0),
        'state_gla': nrm(ks[4], (DEPTH, DEC_BATCH, H_B, DK_B, DV_B), 1.0),
        'norm_w': 1.0 + nrm(ks[5], (DEPTH, D_MODEL), 0.02),
        'w_in': nrm(ks[6], (DEPTH, D_MODEL, D_IN), D_MODEL ** -0.5),
        'w_gate_up': nrm(ks[7], (DEPTH, GATE_RANK, H_B * DK_B), GATE_RANK ** -0.5),
        'b_gate': nrm(ks[8], (DEPTH, H_B * DK_B), 0.1),
        'q_norm_w': 1.0 + nrm(ks[9], (DEPTH, HD_A), 0.02),
        'k_norm_w': 1.0 + nrm(ks[10], (DEPTH, HD_A), 0.02),
        'gla_norm_w': 1.0 + nrm(ks[11], (DEPTH, DV_B), 0.02),
        'w_out': nrm(ks[12], (DEPTH, D_MIX, D_MODEL), D_MIX ** -0.5),
    }


def reference(x_prompt, x_sample, cache_k_win, cache_v_win, state_gla, norm_w, w_in, w_gate_up,
              b_gate, q_norm_w, k_norm_w, gla_norm_w, w_out):
    bp, sp, _ = x_prompt.shape
    bs, ts, _ = x_sample.shape
    pos_p = jnp.arange(sp)
    pos_s = PAST_LEN + jnp.arange(ts)
    keep_p = min(MAX_WINDOW, sp)
    hp, hs = x_prompt, x_sample
    kp_l, vp_l, sp_l, ks_l, vs_l, ss_l = [], [], [], [], [], []
    for layer in range(DEPTH):
        params = (norm_w[layer], w_in[layer], w_gate_up[layer], b_gate[layer], q_norm_w[layer],
                  k_norm_w[layer], gla_norm_w[layer], w_out[layer])
        s0 = jnp.zeros((bp, H_B, DK_B, DV_B), dtype=jnp.float32)
        hp, kp, vp, st_p = mixer_layer(hp, pos_p, attend_prompt, s0, *params)
        att_s = functools.partial(attend_sample, cache_k_win[layer], cache_v_win[layer])
        hs, kn, vn, st_s = mixer_layer(hs, pos_s, att_s, state_gla[layer], *params)
        kp_l.append(kp[:, sp - keep_p:])
        vp_l.append(vp[:, sp - keep_p:])
        sp_l.append(st_p.astype(state_gla.dtype))
        ks_l.append(kn)
        vs_l.append(vn)
        ss_l.append(st_s.astype(state_gla.dtype))
    return (hp, hs, jnp.stack(kp_l), jnp.stack(vp_l), jnp.stack(sp_l),
            jnp.stack(ks_l), jnp.stack(vs_l), jnp.stack(ss_l))
```

```python
import functools

import numpy as np
import jax
import jax.numpy as jnp
from jax import lax
from jax.experimental import pallas as pl
from jax.experimental.pallas import tpu as pltpu

F32 = jnp.float32
BF16 = jnp.bfloat16

H_A, HD_A = 8, 64
D_A = H_A * HD_A
H_B, DK_B, DV_B = 4, 64, 128
D_KB = H_B * DK_B
D_B = H_B * DV_B
GATE_RANK = 16
GATE_TAU = 16.0
DILATED = ((128, 1), (512, 4), (2048, 16))
MAX_DIL = 16
MAX_WINDOW = 2048
BAND = 128
ROPE_THETA = 10000.0
EPS = 1e-6
PAST_LEN = 8192
GLA_CHUNK = 64
GLA_SUB = 8
LANES = 128
VMEM_LIMIT = 56 * 1024 * 1024

_SECTIONS = (("q", D_A), ("k", D_A), ("v", D_A), ("qb", D_KB), ("kb", D_KB), ("vb", D_B),
             ("glr", LANES), ("ga", D_A), ("gb", D_B))
_OFF = {}
_o = 0
for _n, _w in _SECTIONS:
    _OFF[_n] = (_o, _o + _w)
    _o += _w
D_PAD = _o


def _sec(z, name):
    a, b = _OFF[name]
    return z[:, a:b]


def _mod2(x, n):
    assert n & (n - 1) == 0
    return jnp.bitwise_and(x, n - 1)


def _div2(x, n):
    assert n & (n - 1) == 0
    return jnp.right_shift(x, n.bit_length() - 1)


def _dot(a, b):
    return jnp.dot(a, b, preferred_element_type=F32)


def _dot_nt(a, b):
    return lax.dot_general(a, b, (((1,), (1,)), ((), ())), preferred_element_type=F32)


def _dot_tn(a, b):
    return lax.dot_general(a, b, (((0,), (0,)), ((), ())), preferred_element_type=F32)


def _proj_tile(x, cos, sin, nw, w, bd, wq, wk, wgu, bg):
    ms = jnp.mean(x * x, axis=-1, keepdims=True)
    h = (x * lax.rsqrt(ms + EPS) * nw).astype(BF16)
    z = _dot(h, w)
    lane = lax.broadcasted_iota(jnp.int32, (x.shape[0], LANES), 1)
    first_half = _mod2(lane, HD_A) < (HD_A // 2)

    def qk_norm_rope(zz, wn):
        ss = _dot((zz * zz).astype(BF16), bd)
        y = zz * lax.rsqrt(ss * (1.0 / HD_A) + EPS) * wn
        outs = []
        for j in range(D_A // LANES):
            yj = y[:, j * LANES:(j + 1) * LANES]
            swapped = jnp.where(first_half, pltpu.roll(yj, LANES - HD_A // 2, 1),
                                pltpu.roll(yj, HD_A // 2, 1))
            outs.append(yj * cos + swapped * sin)
        return jnp.concatenate(outs, axis=1)

    q = qk_norm_rope(_sec(z, "q"), wq)
    k = qk_norm_rope(_sec(z, "k"), wk)
    xg = _dot(_sec(z, "glr").astype(BF16), wgu) + bg
    log_a = (jnp.minimum(xg, 0.0) - jnp.log1p(jnp.exp(-jnp.abs(xg)))) * (1.0 / GATE_TAU)
    ga = _sec(z, "ga")
    gb = _sec(z, "gb")
    return dict(q=q * (HD_A ** -0.5), k=k, v=_sec(z, "v"),
                qb=_sec(z, "qb") * (DK_B ** -0.5), kb=_sec(z, "kb"), vb=_sec(z, "vb"), la=log_a,
                ga=ga / (1.0 + jnp.exp(-ga)), gb=gb / (1.0 + jnp.exp(-gb)))


def _proj_prompt_kernel(x_ref, cos_ref, sin_ref, nw_ref, w_ref, bd_ref, wq_ref, wk_ref, wgu_ref, bg_ref,
                        q16, k16, v16, q1, k1, v1, kf, vf, qb, kb, vb, la, ga, gb, *, nr, d_model, keep_m):
    lm = x_ref.shape[0]
    for j in range(nr):
        r = _proj_tile(x_ref[:, j * d_model:(j + 1) * d_model], cos_ref[j], sin_ref[j], nw_ref[...],
                       w_ref[...], bd_ref[...], wq_ref[...], wk_ref[...], wgu_ref[...], bg_ref[...])
        ca = slice(j * D_A, (j + 1) * D_A)
        ckb = slice(j * D_KB, (j + 1) * D_KB)
        cb = slice(j * D_B, (j + 1) * D_B)
        q16[j] = r["q"].astype(BF16)
        k16[j] = r["k"].astype(BF16)
        v16[j] = r["v"].astype(BF16)
        q1[:, ca] = r["q"].astype(BF16)
        k1[:, ca] = r["k"].astype(BF16)
        v1[:, ca] = r["v"].astype(BF16)
        kf[:, ca] = r["k"][lm - keep_m:, :]
        vf[:, ca] = r["v"][lm - keep_m:, :]
        qb[:, ckb] = r["qb"]
        kb[:, ckb] = r["kb"]
        vb[:, cb] = r["vb"]
        la[:, ckb] = r["la"]
        ga[j] = r["ga"]
        gb[j] = r["gb"]


def _proj_sample_kernel(x_ref, cos_ref, sin_ref, nw_ref, w_ref, bd_ref, wq_ref, wk_ref, wgu_ref, bg_ref,
                        q, k, v, qb, kb, vb, la, ga, gb):
    r = _proj_tile(x_ref[...], cos_ref[...], sin_ref[...], nw_ref[...], w_ref[...], bd_ref[...],
                   wq_ref[...], wk_ref[...], wgu_ref[...], bg_ref[...])
    q[...] = r["q"].astype(BF16)
    for name, ref in (("k", k), ("v", v), ("qb", qb), ("kb", kb), ("vb", vb), ("la", la), ("ga", ga), ("gb", gb)):
        ref[...] = r[name]


def _rope_tables(pos):
    half = HD_A // 2
    inv_freq = ROPE_THETA ** (-jnp.arange(half, dtype=F32) / half)
    ang = pos.astype(F32)[:, None] * inv_freq[None, :]
    cos, sin = jnp.cos(ang), jnp.sin(ang)
    reps = LANES // HD_A
    cos_t = jnp.tile(jnp.concatenate([cos, cos], axis=1), (1, reps))
    sin_t = jnp.tile(jnp.concatenate([-sin, sin], axis=1), (1, reps))
    return cos_t, sin_t


def _const_spec(shape):
    return pl.BlockSpec(shape, lambda *_: (0,) * len(shape))


def _weight_args(p):
    return (p["nw"], p["w"], p["bd"], p["wq"], p["wk"], p["wgu"], p["bg"])


def _weight_specs(d_model):
    return [_const_spec((1, d_model)), _const_spec((d_model, D_PAD)), _const_spec((D_A, D_A)),
            _const_spec((1, D_A)), _const_spec((1, D_A)), _const_spec((LANES, D_KB)), _const_spec((1, D_KB))]


def _proj_prompt(x, p, keep):
    bsz, seq, d_model = x.shape
    lm = seq // MAX_DIL
    keep_m = keep // MAX_DIL
    nr = 2
    cos_t, sin_t = _rope_tables(jnp.arange(seq))
    to_res = lambda t: t.reshape(lm, MAX_DIL, LANES).transpose(1, 0, 2)
    seg = lambda w, dt: jax.ShapeDtypeStruct((bsz, MAX_DIL, lm, w), dt)
    nat = lambda w, dt, rows=lm: jax.ShapeDtypeStruct((bsz, rows, MAX_DIL * w), dt)
    seg_spec = lambda w: pl.BlockSpec((None, nr, lm, w), lambda b, g: (b, g, 0, 0))
    nat_spec = lambda w, rows=lm: pl.BlockSpec((None, rows, nr * w), lambda b, g: (b, 0, g))
    outs = pl.pallas_call(
        functools.partial(_proj_prompt_kernel, nr=nr, d_model=d_model, keep_m=keep_m),
        grid=(bsz, MAX_DIL // nr),
        in_specs=[nat_spec(d_model),
                  pl.BlockSpec((nr, lm, LANES), lambda b, g: (g, 0, 0)),
                  pl.BlockSpec((nr, lm, LANES), lambda b, g: (g, 0, 0))] + _weight_specs(d_model),
        out_specs=[seg_spec(D_A)] * 3 + [nat_spec(D_A)] * 3 + [nat_spec(D_A, keep_m)] * 2
                  + [nat_spec(D_KB), nat_spec(D_KB), nat_spec(D_B), nat_spec(D_KB), seg_spec(D_A), seg_spec(D_B)],
        out_shape=[seg(D_A, BF16)] * 3 + [nat(D_A, BF16)] * 3 + [nat(D_A, F32, keep_m)] * 2
                  + [nat(D_KB, F32), nat(D_KB, F32), nat(D_B, F32), nat(D_KB, F32), seg(D_A, F32), seg(D_B, F32)],
        compiler_params=pltpu.CompilerParams(dimension_semantics=("arbitrary", "arbitrary"),
                                             vmem_limit_bytes=VMEM_LIMIT),
        name="proj_prompt",
    )(x.reshape(bsz, lm, MAX_DIL * d_model), to_res(cos_t), to_res(sin_t), *_weight_args(p))
    names = ("q16", "k16", "v16", "q1", "k1", "v1", "kf", "vf", "qb", "kb", "vb", "la", "ga", "gb")
    return dict(zip(names, outs))


def _proj_sample(x2, pos, p):
    n, d_model = x2.shape
    tm = 256 if n % 256 == 0 else n
    cos_t, sin_t = _rope_tables(pos)
    row = lambda w: pl.BlockSpec((tm, w), lambda i: (i, 0))
    widths = (D_A, D_A, D_A, D_KB, D_KB, D_B, D_KB, D_A, D_B)
    outs = pl.pallas_call(
        _proj_sample_kernel,
        grid=(n // tm,),
        in_specs=[row(d_model), row(LANES), row(LANES)] + _weight_specs(d_model),
        out_specs=[row(w) for w in widths],
        out_shape=[jax.ShapeDtypeStruct((n, w), BF16 if i == 0 else F32) for i, w in enumerate(widths)],
        compiler_params=pltpu.CompilerParams(dimension_semantics=("arbitrary",), vmem_limit_bytes=VMEM_LIMIT),
        name="proj_sample",
    )(x2, cos_t, sin_t, *_weight_args(p))
    return dict(zip(("q", "k", "v", "qb", "kb", "vb", "la", "ga", "gb"), outs))


def _band_attn_kernel(q_ref, kp_ref, kc_ref, vp_ref, vc_ref, o_ref, lse_ref, *, interleave):
    tq = BAND
    first = pl.program_id(2) == 0
    q = q_ref[...].reshape(tq, D_A)
    k = jnp.concatenate([kp_ref[...].reshape(tq, D_A), kc_ref[...].reshape(tq, D_A)], axis=0)
    v = jnp.concatenate([vp_ref[...].reshape(tq, D_A), vc_ref[...].reshape(tq, D_A)], axis=0)
    rows = _mod2(lax.broadcasted_iota(jnp.int32, (2 * tq, 2 * tq), 0), tq)
    cols = lax.broadcasted_iota(jnp.int32, (2 * tq, 2 * tq), 1)
    grp = tq // interleave
    coord = lambda i: interleave * _mod2(i, grp) + _div2(i, grp)
    dist = coord(rows) + tq - (coord(_mod2(cols, tq)) + tq * _div2(cols, tq))
    valid = (dist >= 0) & (dist <= BAND) & ((cols >= tq) | jnp.logical_not(first))
    lane = lax.broadcasted_iota(jnp.int32, (tq, LANES), 1)
    low = lane < HD_A
    lse_all = jnp.zeros((tq, LANES), F32)
    for hp in range(D_A // LANES):
        cs = slice(hp * LANES, (hp + 1) * LANES)
        qp = q[:, cs]
        zero = jnp.zeros_like(qp)
        qs = jnp.concatenate([jnp.where(low, qp, zero), jnp.where(low, zero, qp)], axis=0)
        s = jnp.where(valid, _dot_nt(qs, k[:, cs]), -jnp.inf)
        m = jnp.max(s, axis=-1, keepdims=True)
        pr = jnp.exp(s - m)
        den = jnp.sum(pr, axis=-1, keepdims=True)
        pv = _dot(pr.astype(BF16), v[:, cs]) / den
        o_ref[..., cs] = jnp.where(low, pv[:tq], pv[tq:]).reshape(o_ref.shape[:-1] + (LANES,))
        lse = m + jnp.log(den)
        lse_all = jnp.where(lane == 2 * hp, lse[:tq], lse_all)
        lse_all = jnp.where(lane == 2 * hp + 1, lse[tq:], lse_all)
    lse_ref[...] = lse_all.reshape(lse_ref.shape)


def _band_attn(q, k, v, interleave):
    bsz = q.shape[0]
    if interleave == 1:
        nseg, length = q.shape[1], q.shape[2]
        blk = lambda w: (None, None, BAND, w)
        cur = lambda b, s, i: (b, s, i, 0)
        prev = lambda b, s, i: (b, s, jnp.maximum(i - 1, 0), 0)
        shp = lambda w: (bsz, nseg, length, w)
        nblk = length // BAND
    else:
        nseg, length = q.shape[2], q.shape[3]
        rows = BAND // interleave
        blk = lambda w: (None, interleave, None, rows, w)
        cur = lambda b, s, i: (b, 0, s, i, 0)
        prev = lambda b, s, i: (b, 0, s, jnp.maximum(i - 1, 0), 0)
        shp = lambda w: (bsz, interleave, nseg, length, w)
        nblk = length // rows
    return pl.pallas_call(
        functools.partial(_band_attn_kernel, interleave=interleave),
        grid=(bsz, nseg, nblk),
        in_specs=[pl.BlockSpec(blk(D_A), cur), pl.BlockSpec(blk(D_A), prev), pl.BlockSpec(blk(D_A), cur),
                  pl.BlockSpec(blk(D_A), prev), pl.BlockSpec(blk(D_A), cur)],
        out_specs=[pl.BlockSpec(blk(D_A), cur), pl.BlockSpec(blk(LANES), cur)],
        out_shape=[jax.ShapeDtypeStruct(shp(D_A), F32), jax.ShapeDtypeStruct(shp(LANES), F32)],
        compiler_params=pltpu.CompilerParams(dimension_semantics=("arbitrary",) * 3, vmem_limit_bytes=VMEM_LIMIT),
        name=f"band_attn_{interleave}",
    )(q, k, k, v, v)


def _sample_attn_kernel(q_ref, kn_ref, vn_ref, kc_ref, vc_ref, wc_ref, wn_ref, o_ref):
    t = q_ref.shape[0]
    q = q_ref[...]
    head = _div2(lax.broadcasted_iota(jnp.int32, (t, D_A), 1), HD_A)
    zero = jnp.zeros_like(q)
    qbd = jnp.concatenate([jnp.where(head == h, q, zero) for h in range(H_A)], axis=0)
    wc, wn = wc_ref[...], wn_ref[...]
    sc = jnp.where(wc > 0, _dot_nt(qbd, kc_ref[...].astype(BF16)), -jnp.inf)
    sn = jnp.where(wn > 0, _dot_nt(qbd, kn_ref[...].astype(BF16)), -jnp.inf)
    m = jnp.maximum(jnp.max(sc, axis=-1, keepdims=True), jnp.max(sn, axis=-1, keepdims=True))
    pc = wc * jnp.exp(sc - m)
    pn = wn * jnp.exp(sn - m)
    den = jnp.sum(pc, axis=-1, keepdims=True) + jnp.sum(pn, axis=-1, keepdims=True)
    o = (_dot(pc.astype(BF16), vc_ref[...].astype(BF16)) + _dot(pn.astype(BF16), vn_ref[...].astype(BF16))) / den
    acc = jnp.zeros((t, D_A), F32)
    for h in range(H_A):
        acc = jnp.where(head == h, o[h * t:(h + 1) * t], acc)
    o_ref[...] = acc


def _sample_key_weights(win, t_len):
    tt = np.arange(t_len)[:, None]
    rel_c = win + tt - np.arange(win)[None, :]
    rel_n = tt - np.arange(t_len)[None, :]
    def count(rel):
        c = np.zeros(rel.shape, np.float32)
        for window, dil in DILATED:
            c += (rel >= 0) & (rel % dil == 0) & (rel // dil <= window // dil)
        return c
    return np.tile(count(rel_c), (H_A, 1)), np.tile(count(rel_n), (H_A, 1))


def _sample_attn(q, k_new, v_new, k_cache, v_cache):
    bsz, t_len, _ = q.shape
    win = k_cache.shape[1]
    wc, wn = _sample_key_weights(win, t_len)
    per_b = lambda rows: pl.BlockSpec((None, rows, D_A), lambda b: (b, 0, 0))
    return pl.pallas_call(
        _sample_attn_kernel,
        grid=(bsz,),
        in_specs=[per_b(t_len), per_b(t_len), per_b(t_len), per_b(win), per_b(win),
                  _const_spec(wc.shape), _const_spec(wn.shape)],
        out_specs=per_b(t_len),
        out_shape=jax.ShapeDtypeStruct((bsz, t_len, D_A), F32),
        compiler_params=pltpu.CompilerParams(dimension_semantics=("arbitrary",), vmem_limit_bytes=VMEM_LIMIT),
        name="sample_attn",
    )(q, k_new, v_new, k_cache, v_cache, jnp.asarray(wc), jnp.asarray(wn))


def _stack_heads(x):
    low = lax.broadcasted_iota(jnp.int32, x.shape, 1) < DK_B
    zero = jnp.zeros_like(x)
    return jnp.concatenate([jnp.where(low, x, zero), jnp.where(low, zero, x)], axis=0)


def _gla_chunk(q, k, v, g, st_ref, gw, bdx):
    c = q.shape[0]
    row = lax.broadcasted_iota(jnp.int32, (c, c), 0)
    col = lax.broadcasted_iota(jnp.int32, (c, c), 1)
    tri = (row >= col).astype(BF16)
    g_hi = g.astype(BF16)
    g_lo = (g - g_hi.astype(F32)).astype(BF16)
    b = _dot(tri, g_hi) + _dot(tri, g_lo)
    b_last = b[c - 1:c, :]
    q_in = q * jnp.exp(b)
    k_dec = k * jnp.exp(b_last - b)
    trow = lax.broadcasted_iota(jnp.int32, (c, D_KB), 0)

    levels = []
    p_blk = c
    while p_blk > GLA_SUB:
        half = p_blk // 2
        pieces = [jnp.broadcast_to(b[s + half - 1:s + half, :], (p_blk, D_KB)) for s in range(0, c, p_blk)]
        bm = pieces[0] if len(pieces) == 1 else jnp.concatenate(pieces, axis=0)
        second = _mod2(trow, p_blk) >= half
        q_l = jnp.where(second, q * jnp.exp(jnp.where(second, b - bm, 0.0)), 0.0)
        k_l = jnp.where(second, 0.0, k * jnp.exp(jnp.where(second, 0.0, bm - b)))
        levels.append((q_l, k_l, _div2(row, p_blk) == _div2(col, p_blk)))
        p_blk = half

    sub = _mod2(trow, GLA_SUB)
    terms, v_shift = [], []
    for d in range(GLA_SUB):
        if d == 0:
            k_s, b_s, v_s = k, b, v
        else:
            k_s, b_s, v_s = pltpu.roll(k, d, 0), pltpu.roll(b, d, 0), pltpu.roll(v, d, 0)
        ok = sub >= d
        terms.append(jnp.where(ok, q * k_s * jnp.exp(jnp.where(ok, b - b_s, 0.0)), 0.0))
        v_shift.append(v_s)
    sc_d = _dot(jnp.concatenate(terms, axis=0).astype(BF16), bdx)
    o_diag = sc_d[0:c] * v_shift[0]
    for d in range(1, GLA_SUB):
        o_diag = o_diag + sc_d[d * c:(d + 1) * c] * v_shift[d]

    outs = []
    for p in range(H_B // 2):
        ks = slice(p * LANES, (p + 1) * LANES)
        st = st_ref[p]
        o_pair = _dot_nt(_stack_heads(q_in[:, ks]).astype(BF16), st.astype(BF16))
        if levels:
            sc = None
            for q_l, k_l, same in levels:
                s_l = _dot_nt(_stack_heads(q_l[:, ks]).astype(BF16), k_l[:, ks].astype(BF16))
                s_l = jnp.where(jnp.concatenate([same, same], axis=0), s_l, 0.0)
                sc = s_l if sc is None else sc + s_l
            sc = sc.astype(BF16)
        upd = []
        for e in range(2):
            hidx = 2 * p + e
            v_h = v[:, hidx * DV_B:(hidx + 1) * DV_B]
            o_h = o_pair[e * c:(e + 1) * c] + o_diag[:, hidx * DV_B:(hidx + 1) * DV_B]
            if levels:
                o_h = o_h + _dot(sc[e * c:(e + 1) * c], v_h.astype(BF16))
            ms = jnp.mean(o_h * o_h, axis=-1, keepdims=True)
            outs.append(o_h * lax.rsqrt(ms + EPS) * gw)
            upd.append(_dot_tn(v_h.astype(BF16), k_dec[:, ks].astype(BF16)))
        low = lax.broadcasted_iota(jnp.int32, (DV_B, LANES), 1) < DK_B
        st_ref[p] = st * jnp.exp(b_last[:, ks]) + jnp.where(low, upd[0], upd[1])
    return jnp.concatenate(outs, axis=1)


def _gla_kernel(q_ref, k_ref, v_ref, g_ref, s0_ref, gw_ref, bdx_ref, o_ref, sfin_ref, st_ref, *, chunk, n_chunks):
    step = pl.program_id(1)

    @pl.when(step == 0)
    def _():
        for p in range(H_B // 2):
            st_ref[p] = s0_ref[p * LANES:(p + 1) * LANES, :].T

    for ci in range(n_chunks):
        rs = slice(ci * chunk, (ci + 1) * chunk)
        o_ref[rs, :] = _gla_chunk(q_ref[rs, :], k_ref[rs, :], v_ref[rs, :], g_ref[rs, :], st_ref,
                                  gw_ref[...], bdx_ref[...])

    @pl.when(step == pl.num_programs(1) - 1)
    def _():
        for p in range(H_B // 2):
            sfin_ref[p * LANES:(p + 1) * LANES, :] = st_ref[p].T


def _gla(qb, kb, vb, la, s0, gw, bdx):
    bsz, length, _ = qb.shape
    chunk = int(np.gcd(length, GLA_CHUNK))
    n_chunks = 4 if length % (4 * chunk) == 0 else 1
    tl = chunk * n_chunks
    tok = lambda w: pl.BlockSpec((None, tl, w), lambda b, i: (b, i, 0))
    st = pl.BlockSpec((None, H_B * DK_B, DV_B), lambda b, i: (b, 0, 0))
    return pl.pallas_call(
        functools.partial(_gla_kernel, chunk=chunk, n_chunks=n_chunks),
        grid=(bsz, length // tl),
        in_specs=[tok(D_KB), tok(D_KB), tok(D_B), tok(D_KB), st, _const_spec((1, DV_B)), _const_spec((D_KB, D_B))],
        out_specs=[tok(D_B), st],
        out_shape=[jax.ShapeDtypeStruct((bsz, length, D_B), F32),
                   jax.ShapeDtypeStruct((bsz, H_B * DK_B, DV_B), F32)],
        scratch_shapes=[pltpu.VMEM((H_B // 2, DV_B, LANES), F32)],
        compiler_params=pltpu.CompilerParams(dimension_semantics=("arbitrary", "arbitrary"),
                                             vmem_limit_bytes=VMEM_LIMIT),
        name="gla",
    )(qb, kb, vb, la, s0, gw, bdx)


def _mix_and_project(o_a, ga, o_b, gb, x, wo):
    mixed = jnp.concatenate([o_a * ga, o_b * gb], axis=1).astype(BF16)
    return x + _dot(mixed, wo)


def _out_prompt_kernel(o16, l16, o4, l4, o1, l1, ga, ob, gb, x_ref, ex_ref, wo_ref, y_ref, *, nr, d_model):
    for j in range(nr):
        la_, lb_, lc_ = l16[j], l4[j], l1[:, j * LANES:(j + 1) * LANES]
        m = jnp.maximum(jnp.maximum(la_, lb_), lc_)
        ea, eb, ec = jnp.exp(la_ - m), jnp.exp(lb_ - m), jnp.exp(lc_ - m)
        inv = 1.0 / (ea + eb + ec)
        spread = lambda a: _dot((a * inv).astype(BF16), ex_ref[...])
        o_a = (spread(ea) * o16[j] + spread(eb) * o4[j] + spread(ec) * o1[:, j * D_A:(j + 1) * D_A])
        y_ref[:, j * d_model:(j + 1) * d_model] = _mix_and_project(
            o_a, ga[j], ob[:, j * D_B:(j + 1) * D_B], gb[j], x_ref[:, j * d_model:(j + 1) * d_model], wo_ref[...])


def _out_sample_kernel(oa, ga, ob, gb, x_ref, wo_ref, y_ref):
    y_ref[...] = _mix_and_project(oa[...], ga[...], ob[...], gb[...], x_ref[...], wo_ref[...])


def _out_prompt(att16, att4, att1, ga, o_b, gb, x, p):
    bsz, seq, d_model = x.shape
    lm = seq // MAX_DIL
    nr = 2
    seg_spec = lambda w: pl.BlockSpec((None, nr, lm, w), lambda b, g: (b, g, 0, 0))
    nat_spec = lambda w: pl.BlockSpec((None, lm, nr * w), lambda b, g: (b, 0, g))
    o16, l16 = att16
    o4, l4 = (a.reshape(bsz, MAX_DIL, lm, a.shape[-1]) for a in att4)
    o1, l1 = (a.reshape(bsz, lm, MAX_DIL * a.shape[-1]) for a in att1)
    d_mix = D_A + D_B
    y = pl.pallas_call(
        functools.partial(_out_prompt_kernel, nr=nr, d_model=d_model),
        grid=(bsz, MAX_DIL // nr),
        in_specs=[seg_spec(D_A), seg_spec(LANES), seg_spec(D_A), seg_spec(LANES), nat_spec(D_A), nat_spec(LANES),
                  seg_spec(D_A), nat_spec(D_B), seg_spec(D_B), nat_spec(d_model),
                  _const_spec((LANES, D_A)), _const_spec((d_mix, d_model))],
        out_specs=nat_spec(d_model),
        out_shape=jax.ShapeDtypeStruct((bsz, lm, MAX_DIL * d_model), F32),
        compiler_params=pltpu.CompilerParams(dimension_semantics=("arbitrary", "arbitrary"),
                                             vmem_limit_bytes=VMEM_LIMIT),
        name="out_prompt",
    )(o16, l16, o4, l4, o1, l1, ga, o_b.reshape(bsz, lm, MAX_DIL * D_B), gb,
      x.reshape(bsz, lm, MAX_DIL * d_model), p["ex"], p["wo"])
    return y.reshape(bsz, seq, d_model)


def _out_sample(o_a, ga, o_b, gb, x2, p):
    n, d_model = x2.shape
    tm = 256 if n % 256 == 0 else n
    row = lambda w: pl.BlockSpec((tm, w), lambda i: (i, 0))
    return pl.pallas_call(
        _out_sample_kernel,
        grid=(n // tm,),
        in_specs=[row(D_A), row(D_A), row(D_B), row(D_B), row(d_model), _const_spec((D_A + D_B, d_model))],
        out_specs=row(d_model),
        out_shape=jax.ShapeDtypeStruct((n, d_model), F32),
        compiler_params=pltpu.CompilerParams(dimension_semantics=("arbitrary",), vmem_limit_bytes=VMEM_LIMIT),
        name="out_sample",
    )(o_a, ga, o_b, gb, x2, p["wo"])


def _layer_params(norm_w, w_in, w_gate_up, b_gate, q_norm_w, k_norm_w, gla_norm_w, w_out):
    d_model = w_in.shape[0]
    split = np.cumsum((D_A, D_A, D_A, D_KB, D_KB, D_B, GATE_RANK, D_A, D_B))
    glr = w_in[:, split[5]:split[6]]
    w = jnp.concatenate([w_in[:, :split[5]], glr, jnp.zeros((d_model, LANES - GATE_RANK), w_in.dtype),
                         w_in[:, split[6]:]], axis=1).astype(BF16)
    head_of = np.arange(D_A) // HD_A
    bd = jnp.asarray(head_of[:, None] == head_of[None, :], BF16)
    ex = jnp.asarray(np.arange(LANES)[:, None] == head_of[None, :], BF16)
    bdx = jnp.asarray((np.arange(D_KB) // DK_B)[:, None] == (np.arange(D_B) // DV_B)[None, :], BF16)
    wgu = jnp.concatenate([w_gate_up, jnp.zeros((LANES - GATE_RANK, D_KB), w_gate_up.dtype)], axis=0).astype(BF16)
    return dict(nw=norm_w.reshape(1, d_model), w=w, bd=bd, ex=ex, bdx=bdx,
                wq=jnp.tile(q_norm_w, H_A).reshape(1, D_A), wk=jnp.tile(k_norm_w, H_A).reshape(1, D_A),
                wgu=wgu, bg=b_gate.reshape(1, D_KB), gw=gla_norm_w.reshape(1, DV_B), wo=w_out.astype(BF16))


def _prompt_layer(x, p, keep):
    bsz, seq, _ = x.shape
    lm = seq // MAX_DIL
    r = _proj_prompt(x, p, keep)
    att16 = _band_attn(r["q16"], r["k16"], r["v16"], 1)
    by4 = lambda a: a.reshape(bsz, 4, 4, lm, D_A)
    att4 = _band_attn(by4(r["q16"]), by4(r["k16"]), by4(r["v16"]), 4)
    nat = lambda a: a.reshape(bsz, 1, seq, D_A)
    att1 = _band_attn(nat(r["q1"]), nat(r["k1"]), nat(r["v1"]), 1)
    tok = lambda a, w: a.reshape(bsz, seq, w)
    o_b, s_fin = _gla(tok(r["qb"], D_KB), tok(r["kb"], D_KB), tok(r["vb"], D_B), tok(r["la"], D_KB),
                      jnp.zeros((bsz, H_B * DK_B, DV_B), F32), p["gw"], p["bdx"])
    y = _out_prompt(att16, att4, att1, r["ga"], o_b, r["gb"], x, p)
    k_win = r["kf"].reshape(bsz, keep, H_A, HD_A)
    v_win = r["vf"].reshape(bsz, keep, H_A, HD_A)
    return y, k_win, v_win, s_fin.reshape(bsz, H_B, DK_B, DV_B)


def _sample_layer(x, p, k_cache, v_cache, s0):
    bsz, t_len, d_model = x.shape
    win = k_cache.shape[1]
    pos = PAST_LEN + jnp.tile(jnp.arange(t_len), bsz)
    r = _proj_sample(x.reshape(bsz * t_len, d_model), pos, p)
    tok = lambda a: a.reshape(bsz, t_len, a.shape[-1])
    o_a = _sample_attn(tok(r["q"]), tok(r["k"]), tok(r["v"]), k_cache.reshape(bsz, win, D_A),
                       v_cache.reshape(bsz, win, D_A))
    o_b, s_fin = _gla(tok(r["qb"]), tok(r["kb"]), tok(r["vb"]), tok(r["la"]),
                      s0.reshape(bsz, H_B * DK_B, DV_B), p["gw"], p["bdx"])
    y = _out_sample(o_a.reshape(bsz * t_len, D_A), r["ga"], o_b.reshape(bsz * t_len, D_B), r["gb"],
                    x.reshape(bsz * t_len, d_model), p)
    return (y.reshape(bsz, t_len, d_model), r["k"].reshape(bsz, t_len, H_A, HD_A),
            r["v"].reshape(bsz, t_len, H_A, HD_A), s_fin.reshape(bsz, H_B, DK_B, DV_B))


def kernel(x_prompt, x_sample, cache_k_win, cache_v_win, state_gla, norm_w, w_in, w_gate_up, b_gate,
           q_norm_w, k_norm_w, gla_norm_w, w_out):
    seq = x_prompt.shape[1]
    assert seq % (MAX_DIL * BAND) == 0, "prompt length must tile into residue-class bands"
    keep = min(MAX_WINDOW, seq)
    hp, hs = x_prompt, x_sample
    outs = [[] for _ in range(6)]
    for layer in range(w_in.shape[0]):
        p = _layer_params(norm_w[layer], w_in[layer], w_gate_up[layer], b_gate[layer], q_norm_w[layer],
                          k_norm_w[layer], gla_norm_w[layer], w_out[layer])
        hp, kp, vp, st_p = _prompt_layer(hp, p, keep)
        hs, kn, vn, st_s = _sample_layer(hs, p, cache_k_win[layer], cache_v_win[layer], state_gla[layer])
        for lst, val in zip(outs, (kp, vp, st_p.astype(state_gla.dtype), kn, vn, st_s.astype(state_gla.dtype))):
            lst.append(val)
    return (hp, hs) + tuple(jnp.stack(o) for o in outs)
```

```python
import functools

import numpy as np
import jax
import jax.numpy as jnp
from jax import lax
from jax.experimental import pallas as pl
from jax.experimental.pallas import tpu as pltpu

F32 = jnp.float32
BF16 = jnp.bfloat16

H_A, HD_A = 8, 64
D_A = H_A * HD_A
H_B, DK_B, DV_B = 4, 64, 128
D_KB = H_B * DK_B
D_B = H_B * DV_B
GATE_RANK = 16
GATE_TAU = 16.0
DILATED = ((128, 1), (512, 4), (2048, 16))
MAX_DIL = 16
MAX_WINDOW = 2048
BAND = 128
ROPE_THETA = 10000.0
EPS = 1e-6
PAST_LEN = 8192
GLA_CHUNK = 64
GLA_SUB = 8
LANES = 128
VMEM_LIMIT = 56 * 1024 * 1024
PROJ_TOKENS = 512

_SECTIONS = (("q", D_A), ("k", D_A), ("v", D_A), ("qb", D_KB), ("kb", D_KB), ("vb", D_B),
             ("glr", LANES), ("ga", D_A), ("gb", D_B))
_OFF = {}
_o = 0
for _n, _w in _SECTIONS:
    _OFF[_n] = (_o, _o + _w)
    _o += _w
D_PAD = _o


def _sec(z, name):
    a, b = _OFF[name]
    return z[:, a:b]


def _mod2(x, n):
    assert n & (n - 1) == 0
    return jnp.bitwise_and(x, n - 1)


def _div2(x, n):
    assert n & (n - 1) == 0
    return jnp.right_shift(x, n.bit_length() - 1)


def _dot(a, b):
    return jnp.dot(a, b, preferred_element_type=F32)


def _dot_nt(a, b):
    return lax.dot_general(a, b, (((1,), (1,)), ((), ())), preferred_element_type=F32)


def _dot_tn(a, b):
    return lax.dot_general(a, b, (((0,), (0,)), ((), ())), preferred_element_type=F32)


def _proj_tile(x, cos, sin, nw, w, bd, wq, wk, wgu, bg):
    ms = jnp.mean(x * x, axis=-1, keepdims=True)
    h = (x * lax.rsqrt(ms + EPS) * nw).astype(BF16)
    z = _dot(h, w)
    lane = lax.broadcasted_iota(jnp.int32, (x.shape[0], LANES), 1)
    first_half = _mod2(lane, HD_A) < (HD_A // 2)

    def qk_norm_rope(zz, wn):
        ss = _dot((zz * zz).astype(BF16), bd)
        y = zz * lax.rsqrt(ss * (1.0 / HD_A) + EPS) * wn
        outs = []
        for j in range(D_A // LANES):
            yj = y[:, j * LANES:(j + 1) * LANES]
            swapped = jnp.where(first_half, pltpu.roll(yj, LANES - HD_A // 2, 1),
                                pltpu.roll(yj, HD_A // 2, 1))
            outs.append(yj * cos + swapped * sin)
        return jnp.concatenate(outs, axis=1)

    q = qk_norm_rope(_sec(z, "q"), wq)
    k = qk_norm_rope(_sec(z, "k"), wk)
    xg = _dot(_sec(z, "glr").astype(BF16), wgu) + bg
    log_a = (jnp.minimum(xg, 0.0) - jnp.log1p(jnp.exp(-jnp.abs(xg)))) * (1.0 / GATE_TAU)
    ga = _sec(z, "ga")
    gb = _sec(z, "gb")
    return dict(q=q * (HD_A ** -0.5), k=k, v=_sec(z, "v"),
                qb=_sec(z, "qb") * (DK_B ** -0.5), kb=_sec(z, "kb"), vb=_sec(z, "vb"), la=log_a,
                ga=ga / (1.0 + jnp.exp(-ga)), gb=gb / (1.0 + jnp.exp(-gb)))


def _proj_prompt_kernel(x_ref, cos_ref, sin_ref, nw_ref, w_ref, bd_ref, wq_ref, wk_ref, wgu_ref, bg_ref,
                        q16, k16, v16, q1, k1, v1, kt, vt, qb, kb, vb, la, ga, gb, slab, *, n_skip):
    r = _proj_tile(x_ref[...], cos_ref[...], sin_ref[...], nw_ref[...], w_ref[...], bd_ref[...],
                   wq_ref[...], wk_ref[...], wgu_ref[...], bg_ref[...])
    rows = x_ref.shape[0] // MAX_DIL
    for name, nat_ref, res_ref in (("q", q1, q16), ("k", k1, k16), ("v", v1, v16)):
        val = r[name]
        nat_ref[...] = val.astype(BF16)
        for j in range(D_A // LANES):
            slab[j] = val[:, j * LANES:(j + 1) * LANES]
        for c in range(MAX_DIL):
            for j in range(D_A // LANES):
                res_ref[c, :, j * LANES:(j + 1) * LANES] = slab[j, pl.ds(c, rows, stride=MAX_DIL), :].astype(BF16)
    for name, ref in (("qb", qb), ("kb", kb), ("vb", vb), ("la", la), ("ga", ga), ("gb", gb)):
        ref[...] = r[name]

    @pl.when(pl.program_id(1) >= n_skip)
    def _():
        kt[...] = r["k"].T
        vt[...] = r["v"].T


def _proj_sample_kernel(x_ref, cos_ref, sin_ref, nw_ref, w_ref, bd_ref, wq_ref, wk_ref, wgu_ref, bg_ref,
                        q, k, v, qb, kb, vb, la, ga, gb):
    r = _proj_tile(x_ref[...], cos_ref[...], sin_ref[...], nw_ref[...], w_ref[...], bd_ref[...],
                   wq_ref[...], wk_ref[...], wgu_ref[...], bg_ref[...])
    q[...] = r["q"].astype(BF16)
    for name, ref in (("k", k), ("v", v), ("qb", qb), ("kb", kb), ("vb", vb), ("la", la), ("ga", ga), ("gb", gb)):
        ref[...] = r[name]


def _rope_tables(pos):
    half = HD_A // 2
    inv_freq = ROPE_THETA ** (-jnp.arange(half, dtype=F32) / half)
    ang = pos.astype(F32)[:, None] * inv_freq[None, :]
    cos, sin = jnp.cos(ang), jnp.sin(ang)
    reps = LANES // HD_A
    cos_t = jnp.tile(jnp.concatenate([cos, cos], axis=1), (1, reps))
    sin_t = jnp.tile(jnp.concatenate([-sin, sin], axis=1), (1, reps))
    return cos_t, sin_t


def _const_spec(shape):
    return pl.BlockSpec(shape, lambda *_: (0,) * len(shape))


def _weight_args(p):
    return (p["nw"], p["w"], p["bd"], p["wq"], p["wk"], p["wgu"], p["bg"])


def _weight_specs(d_model):
    return [_const_spec((1, d_model)), _const_spec((d_model, D_PAD)), _const_spec((D_A, D_A)),
            _const_spec((1, D_A)), _const_spec((1, D_A)), _const_spec((LANES, D_KB)), _const_spec((1, D_KB))]


def _proj_prompt(x, p, keep):
    bsz, seq, d_model = x.shape
    tm = PROJ_TOKENS
    n_skip = (seq - keep) // tm
    cos_t, sin_t = _rope_tables(jnp.arange(seq))
    res = jax.ShapeDtypeStruct((bsz, MAX_DIL, seq // MAX_DIL, D_A), BF16)
    nat = lambda w, dt: jax.ShapeDtypeStruct((bsz, seq, w), dt)
    win_t = jax.ShapeDtypeStruct((bsz, D_A, keep), F32)
    res_spec = pl.BlockSpec((None, MAX_DIL, tm // MAX_DIL, D_A), lambda b, i: (b, 0, i, 0))
    nat_spec = lambda w: pl.BlockSpec((None, tm, w), lambda b, i: (b, i, 0))
    win_spec = pl.BlockSpec((None, D_A, tm), lambda b, i: (b, 0, jnp.maximum(i - n_skip, 0)))
    tab_spec = pl.BlockSpec((tm, LANES), lambda b, i: (i, 0))
    outs = pl.pallas_call(
        functools.partial(_proj_prompt_kernel, n_skip=n_skip),
        grid=(bsz, seq // tm),
        in_specs=[nat_spec(d_model), tab_spec, tab_spec] + _weight_specs(d_model),
        out_specs=[res_spec] * 3 + [nat_spec(D_A)] * 3 + [win_spec] * 2
                  + [nat_spec(D_KB), nat_spec(D_KB), nat_spec(D_B), nat_spec(D_KB), nat_spec(D_A), nat_spec(D_B)],
        out_shape=[res] * 3 + [nat(D_A, BF16)] * 3 + [win_t] * 2
                  + [nat(D_KB, F32), nat(D_KB, F32), nat(D_B, F32), nat(D_KB, F32), nat(D_A, F32), nat(D_B, F32)],
        scratch_shapes=[pltpu.VMEM((D_A // LANES, tm, LANES), F32)],
        compiler_params=pltpu.CompilerParams(dimension_semantics=("arbitrary", "arbitrary"),
                                             vmem_limit_bytes=VMEM_LIMIT),
        name="proj_prompt",
    )(x, cos_t, sin_t, *_weight_args(p))
    names = ("q16", "k16", "v16", "q1", "k1", "v1", "kt", "vt", "qb", "kb", "vb", "la", "ga", "gb")
    return dict(zip(names, outs))


def _proj_sample(x2, pos, p):
    n, d_model = x2.shape
    tm = 256 if n % 256 == 0 else n
    cos_t, sin_t = _rope_tables(pos)
    row = lambda w: pl.BlockSpec((tm, w), lambda i: (i, 0))
    widths = (D_A, D_A, D_A, D_KB, D_KB, D_B, D_KB, D_A, D_B)
    outs = pl.pallas_call(
        _proj_sample_kernel,
        grid=(n // tm,),
        in_specs=[row(d_model), row(LANES), row(LANES)] + _weight_specs(d_model),
        out_specs=[row(w) for w in widths],
        out_shape=[jax.ShapeDtypeStruct((n, w), BF16 if i == 0 else F32) for i, w in enumerate(widths)],
        compiler_params=pltpu.CompilerParams(dimension_semantics=("arbitrary",), vmem_limit_bytes=VMEM_LIMIT),
        name="proj_sample",
    )(x2, cos_t, sin_t, *_weight_args(p))
    return dict(zip(("q", "k", "v", "qb", "kb", "vb", "la", "ga", "gb"), outs))


def _band_attn_kernel(q_ref, kp_ref, kc_ref, vp_ref, vc_ref, o_ref, lse_ref, *, interleave, scatter):
    tq = BAND
    grp = tq // interleave
    first = pl.program_id(1) == 0
    q = q_ref[...].reshape(tq, D_A)
    k = jnp.concatenate([kp_ref[...].reshape(tq, D_A), kc_ref[...].reshape(tq, D_A)], axis=0)
    v = jnp.concatenate([vp_ref[...].reshape(tq, D_A), vc_ref[...].reshape(tq, D_A)], axis=0)
    rows = _mod2(lax.broadcasted_iota(jnp.int32, (2 * tq, 2 * tq), 0), tq)
    cols = lax.broadcasted_iota(jnp.int32, (2 * tq, 2 * tq), 1)
    coord = lambda i: interleave * _mod2(i, grp) + _div2(i, grp)
    dist = coord(rows) + tq - (coord(_mod2(cols, tq)) + tq * _div2(cols, tq))
    valid = (dist >= 0) & (dist <= BAND) & ((cols >= tq) | jnp.logical_not(first))
    lane = lax.broadcasted_iota(jnp.int32, (tq, LANES), 1)
    low = lane < HD_A

    def put(ref, val):
        if not scatter:
            ref[...] = val
            return
        for a in range(interleave):
            start = (MAX_DIL // interleave) * a + pl.program_id(2)
            ref[pl.ds(start, grp, stride=MAX_DIL), :] = val[a * grp:(a + 1) * grp]

    lse_all = jnp.zeros((tq, LANES), F32)
    for hp in range(D_A // LANES):
        cs = slice(hp * LANES, (hp + 1) * LANES)
        qp = q[:, cs]
        zero = jnp.zeros_like(qp)
        qs = jnp.concatenate([jnp.where(low, qp, zero), jnp.where(low, zero, qp)], axis=0)
        s = jnp.where(valid, _dot_nt(qs, k[:, cs]), -jnp.inf)
        m = jnp.max(s, axis=-1, keepdims=True)
        pr = jnp.exp(s - m)
        den = jnp.sum(pr, axis=-1, keepdims=True)
        pv = _dot(pr.astype(BF16), v[:, cs]) / den
        put(o_ref.at[hp] if scatter else o_ref.at[:, cs], jnp.where(low, pv[:tq], pv[tq:]))
        lse = m + jnp.log(den)
        lse_all = jnp.where(lane == 2 * hp, lse[:tq], lse_all)
        lse_all = jnp.where(lane == 2 * hp + 1, lse[tq:], lse_all)
    put(lse_ref, lse_all)


def _band_attn(q, k, v, dil, seq):
    bsz = q.shape[0]
    n_slab = D_A // LANES
    if dil == 1:
        interleave, ncls, nblk = 1, 1, seq // BAND
        blk = (None, BAND, D_A)
        cur = lambda b, i, c: (b, i, 0)
        prev = lambda b, i, c: (b, jnp.maximum(i - 1, 0), 0)
        o_spec = pl.BlockSpec(blk, cur)
        l_spec = pl.BlockSpec((None, BAND, LANES), cur)
        o_shape = (bsz, seq, D_A)
    else:
        interleave, ncls = MAX_DIL // dil, dil
        rows = BAND // interleave
        nblk = seq // MAX_DIL // rows
        if interleave == 1:
            blk = (None, None, rows, D_A)
            cur = lambda b, i, c: (b, c, i, 0)
            prev = lambda b, i, c: (b, c, jnp.maximum(i - 1, 0), 0)
        else:
            q, k, v = (a.reshape(bsz, interleave, ncls, seq // MAX_DIL, D_A) for a in (q, k, v))
            blk = (None, interleave, None, rows, D_A)
            cur = lambda b, i, c: (b, 0, c, i, 0)
            prev = lambda b, i, c: (b, 0, c, jnp.maximum(i - 1, 0), 0)
        ntok = rows * MAX_DIL
        o_spec = pl.BlockSpec((None, n_slab, ntok, LANES), lambda b, i, c: (b, 0, i, 0))
        l_spec = pl.BlockSpec((None, ntok, LANES), lambda b, i, c: (b, i, 0))
        o_shape = (bsz, n_slab, seq, LANES)
    return pl.pallas_call(
        functools.partial(_band_attn_kernel, interleave=interleave, scatter=dil > 1),
        grid=(bsz, nblk, ncls),
        in_specs=[pl.BlockSpec(blk, cur), pl.BlockSpec(blk, prev), pl.BlockSpec(blk, cur),
                  pl.BlockSpec(blk, prev), pl.BlockSpec(blk, cur)],
        out_specs=[o_spec, l_spec],
        out_shape=[jax.ShapeDtypeStruct(o_shape, F32), jax.ShapeDtypeStruct((bsz, seq, LANES), F32)],
        compiler_params=pltpu.CompilerParams(dimension_semantics=("arbitrary",) * 3, vmem_limit_bytes=VMEM_LIMIT),
        name=f"band_attn_d{dil}",
    )(q, k, k, v, v)


def _sample_attn_kernel(q_ref, kn_ref, vn_ref, kc_ref, vc_ref, wc_ref, wn_ref, o_ref):
    t = q_ref.shape[0]
    q = q_ref[...]
    head = _div2(lax.broadcasted_iota(jnp.int32, (t, D_A), 1), HD_A)
    zero = jnp.zeros_like(q)
    qbd = jnp.concatenate([jnp.where(head == h, q, zero) for h in range(H_A)], axis=0)
    wc, wn = wc_ref[...], wn_ref[...]
    sc = jnp.where(wc > 0, _dot(qbd, kc_ref[...].astype(BF16)), -jnp.inf)
    sn = jnp.where(wn > 0, _dot_nt(qbd, kn_ref[...].astype(BF16)), -jnp.inf)
    m = jnp.maximum(jnp.max(sc, axis=-1, keepdims=True), jnp.max(sn, axis=-1, keepdims=True))
    pc = wc * jnp.exp(sc - m)
    pn = wn * jnp.exp(sn - m)
    den = jnp.sum(pc, axis=-1, keepdims=True) + jnp.sum(pn, axis=-1, keepdims=True)
    o = (_dot_nt(pc.astype(BF16), vc_ref[...].astype(BF16)) + _dot(pn.astype(BF16), vn_ref[...].astype(BF16))) / den
    acc = jnp.zeros((t, D_A), F32)
    for h in range(H_A):
        acc = jnp.where(head == h, o[h * t:(h + 1) * t], acc)
    o_ref[...] = acc


def _sample_key_weights(win, t_len):
    tt = np.arange(t_len)[:, None]
    rel_c = win + tt - np.arange(win)[None, :]
    rel_n = tt - np.arange(t_len)[None, :]
    def count(rel):
        c = np.zeros(rel.shape, np.float32)
        for window, dil in DILATED:
            c += (rel >= 0) & (rel % dil == 0) & (rel // dil <= window // dil)
        return c
    return np.tile(count(rel_c), (H_A, 1)), np.tile(count(rel_n), (H_A, 1))


def _sample_attn(q, k_new, v_new, k_cache, v_cache):
    bsz, t_len, _ = q.shape
    win = k_cache.shape[2]
    wc, wn = _sample_key_weights(win, t_len)
    per_b = lambda rows: pl.BlockSpec((None, rows, D_A), lambda b: (b, 0, 0))
    cache = pl.BlockSpec((None, D_A, win), lambda b: (b, 0, 0))
    return pl.pallas_call(
        _sample_attn_kernel,
        grid=(bsz,),
        in_specs=[per_b(t_len), per_b(t_len), per_b(t_len), cache, cache,
                  _const_spec(wc.shape), _const_spec(wn.shape)],
        out_specs=per_b(t_len),
        out_shape=jax.ShapeDtypeStruct((bsz, t_len, D_A), F32),
        compiler_params=pltpu.CompilerParams(dimension_semantics=("arbitrary",), vmem_limit_bytes=VMEM_LIMIT),
        name="sample_attn",
    )(q, k_new, v_new, k_cache, v_cache, jnp.asarray(wc), jnp.asarray(wn))


def _stack_heads(x):
    low = lax.broadcasted_iota(jnp.int32, x.shape, 1) < DK_B
    zero = jnp.zeros_like(x)
    return jnp.concatenate([jnp.where(low, x, zero), jnp.where(low, zero, x)], axis=0)


def _gla_chunk(q, k, v, g, st_ref, gw, bdx):
    c = q.shape[0]
    row = lax.broadcasted_iota(jnp.int32, (c, c), 0)
    col = lax.broadcasted_iota(jnp.int32, (c, c), 1)
    tri = (row >= col).astype(BF16)
    g_hi = g.astype(BF16)
    g_lo = (g - g_hi.astype(F32)).astype(BF16)
    b = _dot(tri, g_hi) + _dot(tri, g_lo)
    b_last = b[c - 1:c, :]
    q_in = q * jnp.exp(b)
    k_dec = k * jnp.exp(b_last - b)
    trow = lax.broadcasted_iota(jnp.int32, (c, D_KB), 0)

    levels = []
    p_blk = c
    while p_blk > GLA_SUB:
        half = p_blk // 2
        pieces = [jnp.broadcast_to(b[s + half - 1:s + half, :], (p_blk, D_KB)) for s in range(0, c, p_blk)]
        bm = pieces[0] if len(pieces) == 1 else jnp.concatenate(pieces, axis=0)
        second = _mod2(trow, p_blk) >= half
        q_l = jnp.where(second, q * jnp.exp(jnp.where(second, b - bm, 0.0)), 0.0)
        k_l = jnp.where(second, 0.0, k * jnp.exp(jnp.where(second, 0.0, bm - b)))
        levels.append((q_l, k_l, _div2(row, p_blk) == _div2(col, p_blk)))
        p_blk = half

    sub = _mod2(trow, GLA_SUB)
    terms, v_shift = [], []
    for d in range(GLA_SUB):
        if d == 0:
            k_s, b_s, v_s = k, b, v
        else:
            k_s, b_s, v_s = pltpu.roll(k, d, 0), pltpu.roll(b, d, 0), pltpu.roll(v, d, 0)
        ok = sub >= d
        terms.append(jnp.where(ok, q * k_s * jnp.exp(jnp.where(ok, b - b_s, 0.0)), 0.0))
        v_shift.append(v_s)
    sc_d = _dot(jnp.concatenate(terms, axis=0).astype(BF16), bdx)
    o_diag = sc_d[0:c] * v_shift[0]
    for d in range(1, GLA_SUB):
        o_diag = o_diag + sc_d[d * c:(d + 1) * c] * v_shift[d]

    outs = []
    for p in range(H_B // 2):
        ks = slice(p * LANES, (p + 1) * LANES)
        st = st_ref[p]
        o_pair = _dot_nt(_stack_heads(q_in[:, ks]).astype(BF16), st.astype(BF16))
        if levels:
            sc = None
            for q_l, k_l, same in levels:
                s_l = _dot_nt(_stack_heads(q_l[:, ks]).astype(BF16), k_l[:, ks].astype(BF16))
                s_l = jnp.where(jnp.concatenate([same, same], axis=0), s_l, 0.0)
                sc = s_l if sc is None else sc + s_l
            sc = sc.astype(BF16)
        upd = []
        for e in range(2):
            hidx = 2 * p + e
            v_h = v[:, hidx * DV_B:(hidx + 1) * DV_B]
            o_h = o_pair[e * c:(e + 1) * c] + o_diag[:, hidx * DV_B:(hidx + 1) * DV_B]
            if levels:
                o_h = o_h + _dot(sc[e * c:(e + 1) * c], v_h.astype(BF16))
            ms = jnp.mean(o_h * o_h, axis=-1, keepdims=True)
            outs.append(o_h * lax.rsqrt(ms + EPS) * gw)
            upd.append(_dot_tn(v_h.astype(BF16), k_dec[:, ks].astype(BF16)))
        low = lax.broadcasted_iota(jnp.int32, (DV_B, LANES), 1) < DK_B
        st_ref[p] = st * jnp.exp(b_last[:, ks]) + jnp.where(low, upd[0], upd[1])
    return jnp.concatenate(outs, axis=1)


def _gla_kernel(q_ref, k_ref, v_ref, g_ref, s0_ref, gw_ref, bdx_ref, o_ref, sfin_ref, st_ref, *, chunk, n_chunks):
    step = pl.program_id(1)

    @pl.when(step == 0)
    def _():
        for p in range(H_B // 2):
            st_ref[p] = s0_ref[p * LANES:(p + 1) * LANES, :].T

    for ci in range(n_chunks):
        rs = slice(ci * chunk, (ci + 1) * chunk)
        o_ref[rs, :] = _gla_chunk(q_ref[rs, :], k_ref[rs, :], v_ref[rs, :], g_ref[rs, :], st_ref,
                                  gw_ref[...], bdx_ref[...])

    @pl.when(step == pl.num_programs(1) - 1)
    def _():
        for p in range(H_B // 2):
            sfin_ref[p * LANES:(p + 1) * LANES, :] = st_ref[p].T


def _gla(qb, kb, vb, la, s0, gw, bdx):
    bsz, length, _ = qb.shape
    chunk = int(np.gcd(length, GLA_CHUNK))
    n_chunks = 4 if length % (4 * chunk) == 0 else 1
    tl = chunk * n_chunks
    tok = lambda w: pl.BlockSpec((None, tl, w), lambda b, i: (b, i, 0))
    st = pl.BlockSpec((None, H_B * DK_B, DV_B), lambda b, i: (b, 0, 0))
    return pl.pallas_call(
        functools.partial(_gla_kernel, chunk=chunk, n_chunks=n_chunks),
        grid=(bsz, length // tl),
        in_specs=[tok(D_KB), tok(D_KB), tok(D_B), tok(D_KB), st, _const_spec((1, DV_B)), _const_spec((D_KB, D_B))],
        out_specs=[tok(D_B), st],
        out_shape=[jax.ShapeDtypeStruct((bsz, length, D_B), F32),
                   jax.ShapeDtypeStruct((bsz, H_B * DK_B, DV_B), F32)],
        scratch_shapes=[pltpu.VMEM((H_B // 2, DV_B, LANES), F32)],
        compiler_params=pltpu.CompilerParams(dimension_semantics=("arbitrary", "arbitrary"),
                                             vmem_limit_bytes=VMEM_LIMIT),
        name="gla",
    )(qb, kb, vb, la, s0, gw, bdx)


def _mix_and_project(o_a, ga, o_b, gb, x, wo):
    mixed = jnp.concatenate([o_a * ga, o_b * gb], axis=1).astype(BF16)
    return x + _dot(mixed, wo)


def _out_prompt_kernel(o16, l16, o4, l4, o1, l1, ga, ob, gb, x_ref, ex_ref, wo_ref, y_ref):
    la_, lb_, lc_ = l16[...], l4[...], l1[...]
    m = jnp.maximum(jnp.maximum(la_, lb_), lc_)
    ea, eb, ec = jnp.exp(la_ - m), jnp.exp(lb_ - m), jnp.exp(lc_ - m)
    inv = 1.0 / (ea + eb + ec)
    spread = lambda a: _dot((a * inv).astype(BF16), ex_ref[...])
    slabs = lambda ref: jnp.concatenate([ref[j] for j in range(D_A // LANES)], axis=1)
    o_a = spread(ea) * slabs(o16) + spread(eb) * slabs(o4) + spread(ec) * o1[...]
    y_ref[...] = _mix_and_project(o_a, ga[...], ob[...], gb[...], x_ref[...], wo_ref[...])


def _out_sample_kernel(oa, ga, ob, gb, x_ref, wo_ref, y_ref):
    y_ref[...] = _mix_and_project(oa[...], ga[...], ob[...], gb[...], x_ref[...], wo_ref[...])


def _out_prompt(att16, att4, att1, ga, o_b, gb, x, p):
    bsz, seq, d_model = x.shape
    tm = PROJ_TOKENS
    nat_spec = lambda w: pl.BlockSpec((None, tm, w), lambda b, i: (b, i, 0))
    slab_spec = pl.BlockSpec((None, D_A // LANES, tm, LANES), lambda b, i: (b, 0, i, 0))
    (o16, l16), (o4, l4), (o1, l1) = att16, att4, att1
    return pl.pallas_call(
        _out_prompt_kernel,
        grid=(bsz, seq // tm),
        in_specs=[slab_spec, nat_spec(LANES), slab_spec, nat_spec(LANES), nat_spec(D_A), nat_spec(LANES),
                  nat_spec(D_A), nat_spec(D_B), nat_spec(D_B), nat_spec(d_model),
                  _const_spec((LANES, D_A)), _const_spec((D_A + D_B, d_model))],
        out_specs=nat_spec(d_model),
        out_shape=jax.ShapeDtypeStruct((bsz, seq, d_model), F32),
        compiler_params=pltpu.CompilerParams(dimension_semantics=("arbitrary", "arbitrary"),
                                             vmem_limit_bytes=VMEM_LIMIT),
        name="out_prompt",
    )(o16, l16, o4, l4, o1, l1, ga, o_b, gb, x, p["ex"], p["wo"])


def _out_sample(o_a, ga, o_b, gb, x2, p):
    n, d_model = x2.shape
    tm = 256 if n % 256 == 0 else n
    row = lambda w: pl.BlockSpec((tm, w), lambda i: (i, 0))
    return pl.pallas_call(
        _out_sample_kernel,
        grid=(n // tm,),
        in_specs=[row(D_A), row(D_A), row(D_B), row(D_B), row(d_model), _const_spec((D_A + D_B, d_model))],
        out_specs=row(d_model),
        out_shape=jax.ShapeDtypeStruct((n, d_model), F32),
        compiler_params=pltpu.CompilerParams(dimension_semantics=("arbitrary",), vmem_limit_bytes=VMEM_LIMIT),
        name="out_sample",
    )(o_a, ga, o_b, gb, x2, p["wo"])


def _layer_params(norm_w, w_in, w_gate_up, b_gate, q_norm_w, k_norm_w, gla_norm_w, w_out):
    d_model = w_in.shape[0]
    split = np.cumsum((D_A, D_A, D_A, D_KB, D_KB, D_B, GATE_RANK, D_A, D_B))
    glr = w_in[:, split[5]:split[6]]
    w = jnp.concatenate([w_in[:, :split[5]], glr, jnp.zeros((d_model, LANES - GATE_RANK), w_in.dtype),
                         w_in[:, split[6]:]], axis=1).astype(BF16)
    head_of = np.arange(D_A) // HD_A
    bd = jnp.asarray(head_of[:, None] == head_of[None, :], BF16)
    ex = jnp.asarray(np.arange(LANES)[:, None] == head_of[None, :], BF16)
    bdx = jnp.asarray((np.arange(D_KB) // DK_B)[:, None] == (np.arange(D_B) // DV_B)[None, :], BF16)
    wgu = jnp.concatenate([w_gate_up, jnp.zeros((LANES - GATE_RANK, D_KB), w_gate_up.dtype)], axis=0).astype(BF16)
    return dict(nw=norm_w.reshape(1, d_model), w=w, bd=bd, ex=ex, bdx=bdx,
                wq=jnp.tile(q_norm_w, H_A).reshape(1, D_A), wk=jnp.tile(k_norm_w, H_A).reshape(1, D_A),
                wgu=wgu, bg=b_gate.reshape(1, D_KB), gw=gla_norm_w.reshape(1, DV_B), wo=w_out.astype(BF16))


def _prompt_layer(x, p, keep):
    bsz, seq, _ = x.shape
    r = _proj_prompt(x, p, keep)
    att = {dil: _band_attn(r["q16"], r["k16"], r["v16"], dil, seq) for _, dil in DILATED if dil > 1}
    att[1] = _band_attn(r["q1"], r["k1"], r["v1"], 1, seq)
    o_b, s_fin = _gla(r["qb"], r["kb"], r["vb"], r["la"], jnp.zeros((bsz, H_B * DK_B, DV_B), F32),
                      p["gw"], p["bdx"])
    y = _out_prompt(att[16], att[4], att[1], r["ga"], o_b, r["gb"], x, p)
    win = lambda a: a.reshape(bsz, H_A, HD_A, keep).transpose(0, 3, 1, 2)
    return y, win(r["kt"]), win(r["vt"]), s_fin.reshape(bsz, H_B, DK_B, DV_B)


def _sample_layer(x, p, k_cache, v_cache, s0):
    bsz, t_len, d_model = x.shape
    win = k_cache.shape[1]
    pos = PAST_LEN + jnp.tile(jnp.arange(t_len), bsz)
    r = _proj_sample(x.reshape(bsz * t_len, d_model), pos, p)
    tok = lambda a: a.reshape(bsz, t_len, a.shape[-1])
    by_pos = lambda a: a.transpose(0, 2, 3, 1).reshape(bsz, D_A, win)
    o_a = _sample_attn(tok(r["q"]), tok(r["k"]), tok(r["v"]), by_pos(k_cache), by_pos(v_cache))
    o_b, s_fin = _gla(tok(r["qb"]), tok(r["kb"]), tok(r["vb"]), tok(r["la"]),
                      s0.reshape(bsz, H_B * DK_B, DV_B), p["gw"], p["bdx"])
    y = _out_sample(o_a.reshape(bsz * t_len, D_A), r["ga"], o_b.reshape(bsz * t_len, D_B), r["gb"],
                    x.reshape(bsz * t_len, d_model), p)
    return (y.reshape(bsz, t_len, d_model), r["k"].reshape(bsz, t_len, H_A, HD_A),
            r["v"].reshape(bsz, t_len, H_A, HD_A), s_fin.reshape(bsz, H_B, DK_B, DV_B))


def kernel(x_prompt, x_sample, cache_k_win, cache_v_win, state_gla, norm_w, w_in, w_gate_up, b_gate,
           q_norm_w, k_norm_w, gla_norm_w, w_out):
    seq = x_prompt.shape[1]
    assert seq % (MAX_DIL * BAND) == 0, "prompt length must tile into residue-class bands"
    assert all(w // d == BAND and MAX_DIL % d == 0 for w, d in DILATED)
    keep = min(MAX_WINDOW, seq)
    hp, hs = x_prompt, x_sample
    outs = [[] for _ in range(6)]
    for layer in range(w_in.shape[0]):
        p = _layer_params(norm_w[layer], w_in[layer], w_gate_up[layer], b_gate[layer], q_norm_w[layer],
                          k_norm_w[layer], gla_norm_w[layer], w_out[layer])
        hp, kp, vp, st_p = _prompt_layer(hp, p, keep)
        hs, kn, vn, st_s = _sample_layer(hs, p, cache_k_win[layer], cache_v_win[layer], state_gla[layer])
        for lst, val in zip(outs, (kp, vp, st_p.astype(state_gla.dtype), kn, vn, st_s.astype(state_gla.dtype))):
            lst.append(val)
    return (hp, hs) + tuple(jnp.stack(o) for o in outs)
```

```python
import functools

import numpy as np
import jax
import jax.numpy as jnp
from jax import lax
from jax.experimental import pallas as pl
from jax.experimental.pallas import tpu as pltpu

F32 = jnp.float32
BF16 = jnp.bfloat16

H_A, HD_A = 8, 64
D_A = H_A * HD_A
H_B, DK_B, DV_B = 4, 64, 128
D_KB = H_B * DK_B
D_B = H_B * DV_B
GATE_RANK = 16
GATE_TAU = 16.0
DILATED = ((128, 1), (512, 4), (2048, 16))
MAX_DIL = 16
MAX_WINDOW = 2048
BAND = 128
ROPE_THETA = 10000.0
EPS = 1e-6
PAST_LEN = 8192
GLA_CHUNK = 64
GLA_SUB = 8
LANES = 128
VMEM_LIMIT = 56 * 1024 * 1024
PROJ_TOKENS = 512
ATTN_QUERIES = 512

_SECTIONS = (("q", D_A), ("k", D_A), ("v", D_A), ("qb", D_KB), ("kb", D_KB), ("vb", D_B),
             ("glr", LANES), ("ga", D_A), ("gb", D_B))
_OFF = {}
_o = 0
for _n, _w in _SECTIONS:
    _OFF[_n] = (_o, _o + _w)
    _o += _w
D_PAD = _o


def _sec(z, name):
    a, b = _OFF[name]
    return z[:, a:b]


def _mod2(x, n):
    assert n & (n - 1) == 0
    return jnp.bitwise_and(x, n - 1)


def _div2(x, n):
    assert n & (n - 1) == 0
    return jnp.right_shift(x, n.bit_length() - 1)


def _dot(a, b):
    return jnp.dot(a, b, preferred_element_type=F32)


def _dot_nt(a, b):
    return lax.dot_general(a, b, (((1,), (1,)), ((), ())), preferred_element_type=F32)


def _dot_tn(a, b):
    return lax.dot_general(a, b, (((0,), (0,)), ((), ())), preferred_element_type=F32)


def _proj_tile(x, cos, sin, nw, w, bd, wq, wk, wgu, bg):
    ms = jnp.mean(x * x, axis=-1, keepdims=True)
    h = (x * lax.rsqrt(ms + EPS) * nw).astype(BF16)
    z = _dot(h, w)
    lane = lax.broadcasted_iota(jnp.int32, (x.shape[0], LANES), 1)
    first_half = _mod2(lane, HD_A) < (HD_A // 2)

    def qk_norm_rope(zz, wn):
        ss = _dot((zz * zz).astype(BF16), bd)
        y = zz * lax.rsqrt(ss * (1.0 / HD_A) + EPS) * wn
        outs = []
        for j in range(D_A // LANES):
            yj = y[:, j * LANES:(j + 1) * LANES]
            swapped = jnp.where(first_half, pltpu.roll(yj, LANES - HD_A // 2, 1),
                                pltpu.roll(yj, HD_A // 2, 1))
            outs.append(yj * cos + swapped * sin)
        return jnp.concatenate(outs, axis=1)

    q = qk_norm_rope(_sec(z, "q"), wq)
    k = qk_norm_rope(_sec(z, "k"), wk)
    xg = _dot(_sec(z, "glr").astype(BF16), wgu) + bg
    log_a = (jnp.minimum(xg, 0.0) - jnp.log1p(jnp.exp(-jnp.abs(xg)))) * (1.0 / GATE_TAU)
    ga = _sec(z, "ga")
    gb = _sec(z, "gb")
    return dict(q=q * (HD_A ** -0.5), k=k, v=_sec(z, "v"),
                qb=_sec(z, "qb") * (DK_B ** -0.5), kb=_sec(z, "kb"), vb=_sec(z, "vb"), la=log_a,
                ga=ga / (1.0 + jnp.exp(-ga)), gb=gb / (1.0 + jnp.exp(-gb)))


def _proj_prompt_kernel(x_ref, cos_ref, sin_ref, nw_ref, w_ref, bd_ref, wq_ref, wk_ref, wgu_ref, bg_ref,
                        q1, k1, v1, q4, k4, v4, q16, k16, v16, kt, vt, qb, kb, vb, la, ga, gb, slab, *, n_skip):
    r = _proj_tile(x_ref[...], cos_ref[...], sin_ref[...], nw_ref[...], w_ref[...], bd_ref[...],
                   wq_ref[...], wk_ref[...], wgu_ref[...], bg_ref[...])
    tm = x_ref.shape[0]
    for name, refs in (("q", (q1, q4, q16)), ("k", (k1, k4, k16)), ("v", (v1, v4, v16))):
        val = r[name]
        for j in range(D_A // LANES):
            slab[j] = val[:, j * LANES:(j + 1) * LANES]
        for (_, dil), ref in zip(DILATED, refs):
            if dil == 1:
                ref[...] = val.astype(BF16)
                continue
            for c in range(dil):
                for j in range(D_A // LANES):
                    ref[c, :, j * LANES:(j + 1) * LANES] = slab[j, pl.ds(c, tm // dil, stride=dil), :].astype(BF16)
    for name, ref in (("qb", qb), ("kb", kb), ("vb", vb), ("la", la)):
        ref[...] = r[name]
    ga[...] = r["ga"].astype(BF16)
    gb[...] = r["gb"].astype(BF16)

    @pl.when(pl.program_id(1) >= n_skip)
    def _():
        kt[...] = r["k"].T
        vt[...] = r["v"].T


def _proj_sample_kernel(x_ref, cos_ref, sin_ref, nw_ref, w_ref, bd_ref, wq_ref, wk_ref, wgu_ref, bg_ref,
                        q, k, v, qb, kb, vb, la, ga, gb):
    r = _proj_tile(x_ref[...], cos_ref[...], sin_ref[...], nw_ref[...], w_ref[...], bd_ref[...],
                   wq_ref[...], wk_ref[...], wgu_ref[...], bg_ref[...])
    q[...] = r["q"].astype(BF16)
    for name, ref in (("k", k), ("v", v), ("qb", qb), ("kb", kb), ("vb", vb), ("la", la), ("ga", ga), ("gb", gb)):
        ref[...] = r[name]


def _rope_tables(pos):
    half = HD_A // 2
    inv_freq = ROPE_THETA ** (-jnp.arange(half, dtype=F32) / half)
    ang = pos.astype(F32)[:, None] * inv_freq[None, :]
    cos, sin = jnp.cos(ang), jnp.sin(ang)
    reps = LANES // HD_A
    cos_t = jnp.tile(jnp.concatenate([cos, cos], axis=1), (1, reps))
    sin_t = jnp.tile(jnp.concatenate([-sin, sin], axis=1), (1, reps))
    return cos_t, sin_t


def _const_spec(shape):
    return pl.BlockSpec(shape, lambda *_: (0,) * len(shape))


def _weight_args(p):
    return (p["nw"], p["w"], p["bd"], p["wq"], p["wk"], p["wgu"], p["bg"])


def _weight_specs(d_model):
    return [_const_spec((1, d_model)), _const_spec((d_model, D_PAD)), _const_spec((D_A, D_A)),
            _const_spec((1, D_A)), _const_spec((1, D_A)), _const_spec((LANES, D_KB)), _const_spec((1, D_KB))]


def _proj_prompt(x, p, keep):
    bsz, seq, d_model = x.shape
    tm = PROJ_TOKENS
    n_skip = (seq - keep) // tm
    cos_t, sin_t = _rope_tables(jnp.arange(seq))
    res = lambda dil: jax.ShapeDtypeStruct((bsz, dil, seq // dil, D_A), BF16)
    nat = lambda w, dt: jax.ShapeDtypeStruct((bsz, seq, w), dt)
    win_t = jax.ShapeDtypeStruct((bsz, D_A, keep), F32)
    res_spec = lambda dil: pl.BlockSpec((None, dil, tm // dil, D_A), lambda b, i: (b, 0, i, 0))
    nat_spec = lambda w: pl.BlockSpec((None, tm, w), lambda b, i: (b, i, 0))
    win_spec = pl.BlockSpec((None, D_A, tm), lambda b, i: (b, 0, jnp.maximum(i - n_skip, 0)))
    tab_spec = pl.BlockSpec((tm, LANES), lambda b, i: (i, 0))
    qkv_specs, qkv_shapes = [], []
    for _, dil in DILATED:
        qkv_specs += [nat_spec(D_A) if dil == 1 else res_spec(dil)] * 3
        qkv_shapes += [nat(D_A, BF16) if dil == 1 else res(dil)] * 3
    outs = pl.pallas_call(
        functools.partial(_proj_prompt_kernel, n_skip=n_skip),
        grid=(bsz, seq // tm),
        in_specs=[nat_spec(d_model), tab_spec, tab_spec] + _weight_specs(d_model),
        out_specs=qkv_specs + [win_spec] * 2
                  + [nat_spec(D_KB), nat_spec(D_KB), nat_spec(D_B), nat_spec(D_KB), nat_spec(D_A), nat_spec(D_B)],
        out_shape=qkv_shapes + [win_t] * 2
                  + [nat(D_KB, F32), nat(D_KB, F32), nat(D_B, F32), nat(D_KB, F32), nat(D_A, BF16), nat(D_B, BF16)],
        scratch_shapes=[pltpu.VMEM((D_A // LANES, tm, LANES), F32)],
        compiler_params=pltpu.CompilerParams(dimension_semantics=("arbitrary", "arbitrary"),
                                             vmem_limit_bytes=VMEM_LIMIT),
        name="proj_prompt",
    )(x, cos_t, sin_t, *_weight_args(p))
    names = [f"{n}{dil}" for _, dil in DILATED for n in "qkv"] + ["kt", "vt", "qb", "kb", "vb", "la", "ga", "gb"]
    return dict(zip(names, outs))


def _proj_sample(x2, pos, p):
    n, d_model = x2.shape
    tm = 256 if n % 256 == 0 else n
    cos_t, sin_t = _rope_tables(pos)
    row = lambda w: pl.BlockSpec((tm, w), lambda i: (i, 0))
    widths = (D_A, D_A, D_A, D_KB, D_KB, D_B, D_KB, D_A, D_B)
    outs = pl.pallas_call(
        _proj_sample_kernel,
        grid=(n // tm,),
        in_specs=[row(d_model), row(LANES), row(LANES)] + _weight_specs(d_model),
        out_specs=[row(w) for w in widths],
        out_shape=[jax.ShapeDtypeStruct((n, w), BF16 if i == 0 else F32) for i, w in enumerate(widths)],
        compiler_params=pltpu.CompilerParams(dimension_semantics=("arbitrary",), vmem_limit_bytes=VMEM_LIMIT),
        name="proj_sample",
    )(x2, cos_t, sin_t, *_weight_args(p))
    return dict(zip(("q", "k", "v", "qb", "kb", "vb", "la", "ga", "gb"), outs))


def _band_attn_kernel(bias_ref, q_ref, kp_ref, kc_ref, vp_ref, vc_ref, o_ref, lse_ref, kbuf, vbuf):
    tq = BAND
    kbuf[0:tq] = kp_ref[...]
    kbuf[tq:] = kc_ref[...]
    vbuf[0:tq] = vp_ref[...]
    vbuf[tq:] = vc_ref[...]
    table0 = jnp.where(pl.program_id(2) == 0, 0, 1)
    lane = lax.broadcasted_iota(jnp.int32, (tq, LANES), 1)
    low = lane < HD_A
    for j in range(q_ref.shape[0] // tq):
        qrows = slice(j * tq, (j + 1) * tq)
        krows = slice(j * tq, (j + 2) * tq)
        m_all = jnp.zeros((tq, LANES), F32)
        den_all = jnp.ones((tq, LANES), F32)
        for hp in range(D_A // LANES):
            cs = slice(hp * LANES, (hp + 1) * LANES)
            qp = q_ref[qrows, cs]
            zero = jnp.zeros_like(qp)
            qs = jnp.concatenate([jnp.where(low, qp, zero), jnp.where(low, zero, qp)], axis=0)
            s = _dot_nt(qs, kbuf[krows, cs]) + (bias_ref[table0] if j == 0 else bias_ref[1])
            m = jnp.max(s, axis=-1, keepdims=True)
            pr = jnp.exp(s - m)
            den = jnp.sum(pr, axis=-1, keepdims=True)
            pv = _dot(pr.astype(BF16), vbuf[krows, cs]) / den
            o_ref[qrows, cs] = jnp.where(low, pv[:tq], pv[tq:]).astype(BF16)
            for e in range(2):
                here = lane == 2 * hp + e
                m_all = jnp.where(here, m[e * tq:(e + 1) * tq], m_all)
                den_all = jnp.where(here, den[e * tq:(e + 1) * tq], den_all)
        lse_ref[qrows, :] = m_all + jnp.log(den_all)


def _band_bias():
    i = np.arange(2 * BAND)[:, None] % BAND
    j = np.arange(2 * BAND)[None, :]
    ok = (i + BAND - j >= 0) & (i + BAND - j <= BAND)
    tables = np.stack([ok & (j >= BAND), ok])
    return jnp.asarray(np.where(tables, 0.0, -np.inf), F32)


def _band_attn(q, k, v):
    bsz, ncls, length, _ = q.shape
    tq = min(ATTN_QUERIES, length)
    ratio = tq // BAND
    cur = lambda b, c, i: (b, c, i, 0)
    prev = lambda b, c, i: (b, c, jnp.maximum(i * ratio - 1, 0), 0)
    cur_spec = lambda w: pl.BlockSpec((None, None, tq, w), cur)
    prev_spec = pl.BlockSpec((None, None, BAND, D_A), prev)
    return pl.pallas_call(
        _band_attn_kernel,
        grid=(bsz, ncls, length // tq),
        in_specs=[_const_spec((2, 2 * BAND, 2 * BAND)), cur_spec(D_A), prev_spec, cur_spec(D_A),
                  prev_spec, cur_spec(D_A)],
        out_specs=[cur_spec(D_A), cur_spec(LANES)],
        out_shape=[jax.ShapeDtypeStruct(q.shape, BF16), jax.ShapeDtypeStruct((bsz, ncls, length, LANES), F32)],
        scratch_shapes=[pltpu.VMEM((BAND + tq, D_A), BF16)] * 2,
        compiler_params=pltpu.CompilerParams(dimension_semantics=("arbitrary",) * 3, vmem_limit_bytes=VMEM_LIMIT),
        name=f"band_attn_c{ncls}",
    )(_band_bias(), q, k, k, v, v)


def _sample_attn_kernel(q_ref, kn_ref, vn_ref, kc_ref, vc_ref, wc_ref, wn_ref, o_ref):
    t = q_ref.shape[0]
    q = q_ref[...]
    head = _div2(lax.broadcasted_iota(jnp.int32, (t, D_A), 1), HD_A)
    zero = jnp.zeros_like(q)
    qbd = jnp.concatenate([jnp.where(head == h, q, zero) for h in range(H_A)], axis=0)
    wc, wn = wc_ref[...], wn_ref[...]
    sc = jnp.where(wc > 0, _dot(qbd, kc_ref[...].astype(BF16)), -jnp.inf)
    sn = jnp.where(wn > 0, _dot_nt(qbd, kn_ref[...].astype(BF16)), -jnp.inf)
    m = jnp.maximum(jnp.max(sc, axis=-1, keepdims=True), jnp.max(sn, axis=-1, keepdims=True))
    pc = wc * jnp.exp(sc - m)
    pn = wn * jnp.exp(sn - m)
    den = jnp.sum(pc, axis=-1, keepdims=True) + jnp.sum(pn, axis=-1, keepdims=True)
    o = (_dot_nt(pc.astype(BF16), vc_ref[...].astype(BF16)) + _dot(pn.astype(BF16), vn_ref[...].astype(BF16))) / den
    acc = jnp.zeros((t, D_A), F32)
    for h in range(H_A):
        acc = jnp.where(head == h, o[h * t:(h + 1) * t], acc)
    o_ref[...] = acc


def _sample_key_weights(win, t_len):
    tt = np.arange(t_len)[:, None]
    rel_c = win + tt - np.arange(win)[None, :]
    rel_n = tt - np.arange(t_len)[None, :]
    def count(rel):
        c = np.zeros(rel.shape, np.float32)
        for window, dil in DILATED:
            c += (rel >= 0) & (rel % dil == 0) & (rel // dil <= window // dil)
        return c
    return np.tile(count(rel_c), (H_A, 1)), np.tile(count(rel_n), (H_A, 1))


def _sample_attn(q, k_new, v_new, k_cache, v_cache):
    bsz, t_len, _ = q.shape
    win = k_cache.shape[2]
    wc, wn = _sample_key_weights(win, t_len)
    per_b = lambda rows: pl.BlockSpec((None, rows, D_A), lambda b: (b, 0, 0))
    cache = pl.BlockSpec((None, D_A, win), lambda b: (b, 0, 0))
    return pl.pallas_call(
        _sample_attn_kernel,
        grid=(bsz,),
        in_specs=[per_b(t_len), per_b(t_len), per_b(t_len), cache, cache,
                  _const_spec(wc.shape), _const_spec(wn.shape)],
        out_specs=per_b(t_len),
        out_shape=jax.ShapeDtypeStruct((bsz, t_len, D_A), F32),
        compiler_params=pltpu.CompilerParams(dimension_semantics=("arbitrary",), vmem_limit_bytes=VMEM_LIMIT),
        name="sample_attn",
    )(q, k_new, v_new, k_cache, v_cache, jnp.asarray(wc), jnp.asarray(wn))


def _stack_heads(x):
    low = lax.broadcasted_iota(jnp.int32, x.shape, 1) < DK_B
    zero = jnp.zeros_like(x)
    return jnp.concatenate([jnp.where(low, x, zero), jnp.where(low, zero, x)], axis=0)


def _gla_chunk(q, k, v, g, st_ref, gw, bdx):
    c = q.shape[0]
    row = lax.broadcasted_iota(jnp.int32, (c, c), 0)
    col = lax.broadcasted_iota(jnp.int32, (c, c), 1)
    tri = (row >= col).astype(BF16)
    g_hi = g.astype(BF16)
    g_lo = (g - g_hi.astype(F32)).astype(BF16)
    b = _dot(tri, g_hi) + _dot(tri, g_lo)
    b_last = b[c - 1:c, :]
    q_in = q * jnp.exp(b)
    k_dec = k * jnp.exp(b_last - b)
    trow = lax.broadcasted_iota(jnp.int32, (c, D_KB), 0)

    levels = []
    p_blk = c
    while p_blk > GLA_SUB:
        half = p_blk // 2
        pieces = [jnp.broadcast_to(b[s + half - 1:s + half, :], (p_blk, D_KB)) for s in range(0, c, p_blk)]
        bm = pieces[0] if len(pieces) == 1 else jnp.concatenate(pieces, axis=0)
        second = _mod2(trow, p_blk) >= half
        q_l = jnp.where(second, q * jnp.exp(jnp.where(second, b - bm, 0.0)), 0.0)
        k_l = jnp.where(second, 0.0, k * jnp.exp(jnp.where(second, 0.0, bm - b)))
        levels.append((q_l, k_l, _div2(row, p_blk) == _div2(col, p_blk)))
        p_blk = half

    sub = _mod2(trow, GLA_SUB)
    terms, v_shift = [], []
    for d in range(GLA_SUB):
        if d == 0:
            k_s, b_s, v_s = k, b, v
        else:
            k_s, b_s, v_s = pltpu.roll(k, d, 0), pltpu.roll(b, d, 0), pltpu.roll(v, d, 0)
        ok = sub >= d
        terms.append(jnp.where(ok, q * k_s * jnp.exp(jnp.where(ok, b - b_s, 0.0)), 0.0))
        v_shift.append(v_s)
    sc_d = _dot(jnp.concatenate(terms, axis=0).astype(BF16), bdx)
    o_diag = sc_d[0:c] * v_shift[0]
    for d in range(1, GLA_SUB):
        o_diag = o_diag + sc_d[d * c:(d + 1) * c] * v_shift[d]

    outs = []
    for p in range(H_B // 2):
        ks = slice(p * LANES, (p + 1) * LANES)
        st = st_ref[p]
        o_pair = _dot_nt(_stack_heads(q_in[:, ks]).astype(BF16), st.astype(BF16))
        if levels:
            sc = None
            for q_l, k_l, same in levels:
                s_l = _dot_nt(_stack_heads(q_l[:, ks]).astype(BF16), k_l[:, ks].astype(BF16))
                s_l = jnp.where(jnp.concatenate([same, same], axis=0), s_l, 0.0)
                sc = s_l if sc is None else sc + s_l
            sc = sc.astype(BF16)
        upd = []
        for e in range(2):
            hidx = 2 * p + e
            v_h = v[:, hidx * DV_B:(hidx + 1) * DV_B]
            o_h = o_pair[e * c:(e + 1) * c] + o_diag[:, hidx * DV_B:(hidx + 1) * DV_B]
            if levels:
                o_h = o_h + _dot(sc[e * c:(e + 1) * c], v_h.astype(BF16))
            ms = jnp.mean(o_h * o_h, axis=-1, keepdims=True)
            outs.append(o_h * lax.rsqrt(ms + EPS) * gw)
            upd.append(_dot_tn(v_h.astype(BF16), k_dec[:, ks].astype(BF16)))
        low = lax.broadcasted_iota(jnp.int32, (DV_B, LANES), 1) < DK_B
        st_ref[p] = st * jnp.exp(b_last[:, ks]) + jnp.where(low, upd[0], upd[1])
    return jnp.concatenate(outs, axis=1)


def _gla_kernel(q_ref, k_ref, v_ref, g_ref, s0_ref, gw_ref, bdx_ref, o_ref, sfin_ref, st_ref, *, chunk, n_chunks):
    step = pl.program_id(1)

    @pl.when(step == 0)
    def _():
        for p in range(H_B // 2):
            st_ref[p] = s0_ref[p * LANES:(p + 1) * LANES, :].T

    for ci in range(n_chunks):
        rs = slice(ci * chunk, (ci + 1) * chunk)
        o_ref[rs, :] = _gla_chunk(q_ref[rs, :], k_ref[rs, :], v_ref[rs, :], g_ref[rs, :], st_ref,
                                  gw_ref[...], bdx_ref[...]).astype(o_ref.dtype)

    @pl.when(step == pl.num_programs(1) - 1)
    def _():
        for p in range(H_B // 2):
            sfin_ref[p * LANES:(p + 1) * LANES, :] = st_ref[p].T


def _gla(qb, kb, vb, la, s0, gw, bdx):
    bsz, length, _ = qb.shape
    chunk = int(np.gcd(length, GLA_CHUNK))
    n_chunks = 4 if length % (4 * chunk) == 0 else 1
    tl = chunk * n_chunks
    tok = lambda w: pl.BlockSpec((None, tl, w), lambda b, i: (b, i, 0))
    st = pl.BlockSpec((None, H_B * DK_B, DV_B), lambda b, i: (b, 0, 0))
    return pl.pallas_call(
        functools.partial(_gla_kernel, chunk=chunk, n_chunks=n_chunks),
        grid=(bsz, length // tl),
        in_specs=[tok(D_KB), tok(D_KB), tok(D_B), tok(D_KB), st, _const_spec((1, DV_B)), _const_spec((D_KB, D_B))],
        out_specs=[tok(D_B), st],
        out_shape=[jax.ShapeDtypeStruct((bsz, length, D_B), BF16 if tl % 16 == 0 else F32),
                   jax.ShapeDtypeStruct((bsz, H_B * DK_B, DV_B), F32)],
        scratch_shapes=[pltpu.VMEM((H_B // 2, DV_B, LANES), F32)],
        compiler_params=pltpu.CompilerParams(dimension_semantics=("arbitrary", "arbitrary"),
                                             vmem_limit_bytes=VMEM_LIMIT),
        name="gla",
    )(qb, kb, vb, la, s0, gw, bdx)


def _mix_and_project(o_a, ga, o_b, gb, x, wo):
    mixed = jnp.concatenate([o_a * ga, o_b * gb], axis=1).astype(BF16)
    return x + _dot(mixed, wo)


def _out_prompt_kernel(*refs):
    nb = len(DILATED)
    att = refs[:2 * nb]
    ga, ob, gb, x_ref, ex_ref, wo_ref, y_ref = refs[2 * nb:2 * nb + 7]
    scratch = list(refs[2 * nb + 7:])
    tm = x_ref.shape[0]
    outs, lses = [], []
    for bi, (_, dil) in enumerate(DILATED):
        o_ref, l_ref = att[2 * bi], att[2 * bi + 1]
        if dil == 1:
            outs.append(o_ref[0].astype(F32))
            lses.append(l_ref[0])
            continue
        oslab, lslab = scratch.pop(0), scratch.pop(0)
        for c in range(dil):
            rows = pl.ds(c, tm // dil, stride=dil)
            for j in range(D_A // LANES):
                oslab[j, rows, :] = o_ref[c, :, j * LANES:(j + 1) * LANES].astype(F32)
            lslab[rows, :] = l_ref[c]
        outs.append(jnp.concatenate([oslab[j] for j in range(D_A // LANES)], axis=1))
        lses.append(lslab[...])
    m = functools.reduce(jnp.maximum, lses)
    es = [jnp.exp(l - m) for l in lses]
    inv = 1.0 / functools.reduce(lambda a, b: a + b, es)
    spread = lambda a: _dot((a * inv).astype(BF16), ex_ref[...])
    o_a = functools.reduce(lambda a, b: a + b, [spread(e) * o for e, o in zip(es, outs)])
    y_ref[...] = _mix_and_project(o_a, ga[...], ob[...], gb[...], x_ref[...], wo_ref[...])


def _out_sample_kernel(oa, ga, ob, gb, x_ref, wo_ref, y_ref):
    y_ref[...] = _mix_and_project(oa[...], ga[...], ob[...], gb[...], x_ref[...], wo_ref[...])


def _out_prompt(att, ga, o_b, gb, x, p):
    bsz, seq, d_model = x.shape
    tm = PROJ_TOKENS
    nat_spec = lambda w: pl.BlockSpec((None, tm, w), lambda b, i: (b, i, 0))
    cls_spec = lambda dil, w: pl.BlockSpec((None, dil, tm // dil, w), lambda b, i: (b, 0, i, 0))
    att_specs, att_args, scratch = [], [], []
    for (_, dil), (o, lse) in zip(DILATED, att):
        att_specs += [cls_spec(dil, D_A), cls_spec(dil, LANES)]
        att_args += [o, lse]
        if dil > 1:
            scratch += [pltpu.VMEM((D_A // LANES, tm, LANES), F32), pltpu.VMEM((tm, LANES), F32)]
    return pl.pallas_call(
        _out_prompt_kernel,
        grid=(bsz, seq // tm),
        in_specs=att_specs + [nat_spec(D_A), nat_spec(D_B), nat_spec(D_B), nat_spec(d_model),
                              _const_spec((LANES, D_A)), _const_spec((D_A + D_B, d_model))],
        out_specs=nat_spec(d_model),
        out_shape=jax.ShapeDtypeStruct((bsz, seq, d_model), F32),
        scratch_shapes=scratch,
        compiler_params=pltpu.CompilerParams(dimension_semantics=("arbitrary", "arbitrary"),
                                             vmem_limit_bytes=VMEM_LIMIT),
        name="out_prompt",
    )(*att_args, ga, o_b, gb, x, p["ex"], p["wo"])


def _out_sample(o_a, ga, o_b, gb, x2, p):
    n, d_model = x2.shape
    tm = 256 if n % 256 == 0 else n
    row = lambda w: pl.BlockSpec((tm, w), lambda i: (i, 0))
    return pl.pallas_call(
        _out_sample_kernel,
        grid=(n // tm,),
        in_specs=[row(D_A), row(D_A), row(D_B), row(D_B), row(d_model), _const_spec((D_A + D_B, d_model))],
        out_specs=row(d_model),
        out_shape=jax.ShapeDtypeStruct((n, d_model), F32),
        compiler_params=pltpu.CompilerParams(dimension_semantics=("arbitrary",), vmem_limit_bytes=VMEM_LIMIT),
        name="out_sample",
    )(o_a, ga, o_b, gb, x2, p["wo"])


def _layer_params(norm_w, w_in, w_gate_up, b_gate, q_norm_w, k_norm_w, gla_norm_w, w_out):
    d_model = w_in.shape[0]
    split = np.cumsum((D_A, D_A, D_A, D_KB, D_KB, D_B, GATE_RANK, D_A, D_B))
    glr = w_in[:, split[5]:split[6]]
    w = jnp.concatenate([w_in[:, :split[5]], glr, jnp.zeros((d_model, LANES - GATE_RANK), w_in.dtype),
                         w_in[:, split[6]:]], axis=1).astype(BF16)
    head_of = np.arange(D_A) // HD_A
    bd = jnp.asarray(head_of[:, None] == head_of[None, :], BF16)
    ex = jnp.asarray(np.arange(LANES)[:, None] == head_of[None, :], BF16)
    bdx = jnp.asarray((np.arange(D_KB) // DK_B)[:, None] == (np.arange(D_B) // DV_B)[None, :], BF16)
    wgu = jnp.concatenate([w_gate_up, jnp.zeros((LANES - GATE_RANK, D_KB), w_gate_up.dtype)], axis=0).astype(BF16)
    return dict(nw=norm_w.reshape(1, d_model), w=w, bd=bd, ex=ex, bdx=bdx,
                wq=jnp.tile(q_norm_w, H_A).reshape(1, D_A), wk=jnp.tile(k_norm_w, H_A).reshape(1, D_A),
                wgu=wgu, bg=b_gate.reshape(1, D_KB), gw=gla_norm_w.reshape(1, DV_B), wo=w_out.astype(BF16))


def _prompt_layer(x, p, keep):
    bsz, seq, _ = x.shape
    r = _proj_prompt(x, p, keep)
    by_class = lambda a, dil: a.reshape(bsz, dil, seq // dil, D_A)
    att = [_band_attn(*(by_class(r[f"{n}{dil}"], dil) for n in "qkv")) for _, dil in DILATED]
    o_b, s_fin = _gla(r["qb"], r["kb"], r["vb"], r["la"], jnp.zeros((bsz, H_B * DK_B, DV_B), F32),
                      p["gw"], p["bdx"])
    y = _out_prompt(att, r["ga"], o_b, r["gb"], x, p)
    win = lambda a: a.reshape(bsz, H_A, HD_A, keep).transpose(0, 3, 1, 2)
    return y, win(r["kt"]), win(r["vt"]), s_fin.reshape(bsz, H_B, DK_B, DV_B)


def _sample_layer(x, p, k_cache, v_cache, s0):
    bsz, t_len, d_model = x.shape
    win = k_cache.shape[1]
    pos = PAST_LEN + jnp.tile(jnp.arange(t_len), bsz)
    r = _proj_sample(x.reshape(bsz * t_len, d_model), pos, p)
    tok = lambda a: a.reshape(bsz, t_len, a.shape[-1])
    by_pos = lambda a: a.transpose(0, 2, 3, 1).reshape(bsz, D_A, win)
    o_a = _sample_attn(tok(r["q"]), tok(r["k"]), tok(r["v"]), by_pos(k_cache), by_pos(v_cache))
    o_b, s_fin = _gla(tok(r["qb"]), tok(r["kb"]), tok(r["vb"]), tok(r["la"]),
                      s0.reshape(bsz, H_B * DK_B, DV_B), p["gw"], p["bdx"])
    y = _out_sample(o_a.reshape(bsz * t_len, D_A), r["ga"], o_b.reshape(bsz * t_len, D_B), r["gb"],
                    x.reshape(bsz * t_len, d_model), p)
    return (y.reshape(bsz, t_len, d_model), r["k"].reshape(bsz, t_len, H_A, HD_A),
            r["v"].reshape(bsz, t_len, H_A, HD_A), s_fin.reshape(bsz, H_B, DK_B, DV_B))


def kernel(x_prompt, x_sample, cache_k_win, cache_v_win, state_gla, norm_w, w_in, w_gate_up, b_gate,
           q_norm_w, k_norm_w, gla_norm_w, w_out):
    seq = x_prompt.shape[1]
    assert seq % (MAX_DIL * BAND) == 0, "prompt length must tile into residue-class bands"
    assert all(w // d == BAND and MAX_DIL % d == 0 for w, d in DILATED)
    keep = min(MAX_WINDOW, seq)
    hp, hs = x_prompt, x_sample
    outs = [[] for _ in range(6)]
    for layer in range(w_in.shape[0]):
        p = _layer_params(norm_w[layer], w_in[layer], w_gate_up[layer], b_gate[layer], q_norm_w[layer],
                          k_norm_w[layer], gla_norm_w[layer], w_out[layer])
        hp, kp, vp, st_p = _prompt_layer(hp, p, keep)
        hs, kn, vn, st_s = _sample_layer(hs, p, cache_k_win[layer], cache_v_win[layer], state_gla[layer])
        for lst, val in zip(outs, (kp, vp, st_p.astype(state_gla.dtype), kn, vn, st_s.astype(state_gla.dtype))):
            lst.append(val)
    return (hp, hs) + tuple(jnp.stack(o) for o in outs)
```

```python
import functools

import numpy as np
import jax
import jax.numpy as jnp
from jax import lax
from jax.experimental import pallas as pl
from jax.experimental.pallas import tpu as pltpu

F32 = jnp.float32
BF16 = jnp.bfloat16

H_A, HD_A = 8, 64
D_A = H_A * HD_A
H_B, DK_B, DV_B = 4, 64, 128
D_KB = H_B * DK_B
D_B = H_B * DV_B
GATE_RANK = 16
GATE_TAU = 16.0
DILATED = ((128, 1), (512, 4), (2048, 16))
MAX_DIL = 16
MAX_WINDOW = 2048
BAND = 128
ROPE_THETA = 10000.0
EPS = 1e-6
PAST_LEN = 8192
GLA_CHUNK = 128
GLA_STEP_CHUNKS = 4
GLA_SUB = 8
LANES = 128
VMEM_LIMIT = 56 * 1024 * 1024
PROJ_TOKENS = 512
ATTN_QUERIES = 512

_SECTIONS = (("q", D_A), ("k", D_A), ("v", D_A), ("qb", D_KB), ("kb", D_KB), ("vb", D_B),
             ("glr", LANES), ("ga", D_A), ("gb", D_B))
_OFF = {}
_o = 0
for _n, _w in _SECTIONS:
    _OFF[_n] = (_o, _o + _w)
    _o += _w
D_PAD = _o


def _mod2(x, n):
    assert n & (n - 1) == 0
    return jnp.bitwise_and(x, n - 1)


def _div2(x, n):
    assert n & (n - 1) == 0
    return jnp.right_shift(x, n.bit_length() - 1)


def _dot(a, b):
    return jnp.dot(a, b, preferred_element_type=F32)


def _dot_nt(a, b):
    return lax.dot_general(a, b, (((1,), (1,)), ((), ())), preferred_element_type=F32)


def _dot_tn(a, b):
    return lax.dot_general(a, b, (((0,), (0,)), ((), ())), preferred_element_type=F32)


def _proj_tile(x, cos, sin, nw, w_ref, bd, wq, wk, wgu, bg, emit):
    ms = jnp.mean(x * x, axis=-1, keepdims=True)
    h = (x * lax.rsqrt(ms + EPS) * nw).astype(BF16)
    sec = lambda name: _dot(h, w_ref[:, _OFF[name][0]:_OFF[name][1]])
    lane = lax.broadcasted_iota(jnp.int32, (x.shape[0], LANES), 1)
    first_half = _mod2(lane, HD_A) < (HD_A // 2)

    def qk_norm_rope(zz, wn):
        ss = _dot((zz * zz).astype(BF16), bd)
        y = zz * lax.rsqrt(ss * (1.0 / HD_A) + EPS) * wn
        outs = []
        for j in range(D_A // LANES):
            yj = y[:, j * LANES:(j + 1) * LANES]
            swapped = jnp.where(first_half, pltpu.roll(yj, LANES - HD_A // 2, 1),
                                pltpu.roll(yj, HD_A // 2, 1))
            outs.append(yj * cos + swapped * sin)
        return jnp.concatenate(outs, axis=1)

    emit("q", qk_norm_rope(sec("q"), wq) * (HD_A ** -0.5))
    emit("k", qk_norm_rope(sec("k"), wk))
    emit("v", sec("v"))
    emit("qb", sec("qb") * (DK_B ** -0.5))
    emit("kb", sec("kb"))
    emit("vb", sec("vb"))
    xg = _dot(sec("glr").astype(BF16), wgu) + bg
    emit("la", (jnp.minimum(xg, 0.0) - jnp.log1p(jnp.exp(-jnp.abs(xg)))) * (1.0 / GATE_TAU))
    for name in ("ga", "gb"):
        gate = sec(name)
        emit(name, gate / (1.0 + jnp.exp(-gate)))


def _proj_prompt_kernel(x_ref, cos_ref, sin_ref, nw_ref, w_ref, bd_ref, wq_ref, wk_ref, wgu_ref, bg_ref,
                        q1, k1, v1, q4, k4, v4, q16, k16, v16, kt, vt, qb, kb, vb, la, ga, gb, slabs):
    tm = x_ref.shape[0]
    plain = dict(qb=qb, kb=kb, vb=vb, la=la, ga=ga, gb=gb)
    attn = dict(q=(0, (q1, q4, q16), None), k=(1, (k1, k4, k16), kt), v=(2, (v1, v4, v16), vt))

    def emit(name, val):
        if name in plain:
            plain[name][...] = val.astype(plain[name].dtype)
            return
        si, refs, win_t = attn[name]
        slab = slabs.at[si]
        for j in range(D_A // LANES):
            slab[j] = val[:, j * LANES:(j + 1) * LANES]
        for (_, dil), ref in zip(DILATED, refs):
            if dil == 1:
                ref[...] = val.astype(BF16)
                continue
            for c in range(dil):
                for j in range(D_A // LANES):
                    ref[c, :, j * LANES:(j + 1) * LANES] = slab[j, pl.ds(c, tm // dil, stride=dil), :].astype(BF16)
        if win_t is not None:
            win_t[...] = val.T

    _proj_tile(x_ref[...], cos_ref[...], sin_ref[...], nw_ref[...], w_ref, bd_ref[...],
               wq_ref[...], wk_ref[...], wgu_ref[...], bg_ref[...], emit)


def _proj_sample_kernel(x_ref, cos_ref, sin_ref, nw_ref, w_ref, bd_ref, wq_ref, wk_ref, wgu_ref, bg_ref,
                        q, k, v, qb, kb, vb, la, ga, gb):
    refs = dict(q=q, k=k, v=v, qb=qb, kb=kb, vb=vb, la=la, ga=ga, gb=gb)

    def emit(name, val):
        refs[name][...] = val.astype(refs[name].dtype)

    _proj_tile(x_ref[...], cos_ref[...], sin_ref[...], nw_ref[...], w_ref, bd_ref[...],
               wq_ref[...], wk_ref[...], wgu_ref[...], bg_ref[...], emit)


def _rope_tables(pos):
    half = HD_A // 2
    inv_freq = ROPE_THETA ** (-jnp.arange(half, dtype=F32) / half)
    ang = pos.astype(F32)[:, None] * inv_freq[None, :]
    cos, sin = jnp.cos(ang), jnp.sin(ang)
    reps = LANES // HD_A
    cos_t = jnp.tile(jnp.concatenate([cos, cos], axis=1), (1, reps))
    sin_t = jnp.tile(jnp.concatenate([-sin, sin], axis=1), (1, reps))
    return cos_t, sin_t


def _const_spec(shape):
    return pl.BlockSpec(shape, lambda *_: (0,) * len(shape))


def _weight_args(p):
    return (p["nw"], p["w"], p["bd"], p["wq"], p["wk"], p["wgu"], p["bg"])


def _weight_specs(d_model):
    return [_const_spec((1, d_model)), _const_spec((d_model, D_PAD)), _const_spec((D_A, D_A)),
            _const_spec((1, D_A)), _const_spec((1, D_A)), _const_spec((LANES, D_KB)), _const_spec((1, D_KB))]


def _proj_prompt(x, p, keep):
    bsz, seq, d_model = x.shape
    tm = PROJ_TOKENS
    n_skip = (seq - keep) // tm
    cos_t, sin_t = _rope_tables(jnp.arange(seq))
    res = lambda dil: jax.ShapeDtypeStruct((bsz, dil, seq // dil, D_A), BF16)
    nat = lambda w, dt: jax.ShapeDtypeStruct((bsz, seq, w), dt)
    win_t = jax.ShapeDtypeStruct((bsz, D_A, keep), F32)
    res_spec = lambda dil: pl.BlockSpec((None, dil, tm // dil, D_A), lambda b, i: (b, 0, i, 0))
    nat_spec = lambda w: pl.BlockSpec((None, tm, w), lambda b, i: (b, i, 0))
    win_spec = pl.BlockSpec((None, D_A, tm), lambda b, i: (b, 0, jnp.maximum(i - n_skip, 0)))
    tab_spec = pl.BlockSpec((tm, LANES), lambda b, i: (i, 0))
    qkv_specs, qkv_shapes = [], []
    for _, dil in DILATED:
        qkv_specs += [nat_spec(D_A) if dil == 1 else res_spec(dil)] * 3
        qkv_shapes += [nat(D_A, BF16) if dil == 1 else res(dil)] * 3
    outs = pl.pallas_call(
        _proj_prompt_kernel,
        grid=(bsz, seq // tm),
        in_specs=[nat_spec(d_model), tab_spec, tab_spec] + _weight_specs(d_model),
        out_specs=qkv_specs + [win_spec] * 2
                  + [nat_spec(D_KB), nat_spec(D_KB), nat_spec(D_B), nat_spec(D_KB), nat_spec(D_A), nat_spec(D_B)],
        out_shape=qkv_shapes + [win_t] * 2
                  + [nat(D_KB, F32), nat(D_KB, F32), nat(D_B, F32), nat(D_KB, F32), nat(D_A, BF16), nat(D_B, BF16)],
        scratch_shapes=[pltpu.VMEM((3, D_A // LANES, tm, LANES), F32)],
        compiler_params=pltpu.CompilerParams(dimension_semantics=("arbitrary", "arbitrary"),
                                             vmem_limit_bytes=VMEM_LIMIT),
        name="proj_prompt",
    )(x, cos_t, sin_t, *_weight_args(p))
    names = [f"{n}{dil}" for _, dil in DILATED for n in "qkv"] + ["kt", "vt", "qb", "kb", "vb", "la", "ga", "gb"]
    return dict(zip(names, outs))


def _proj_sample(x2, pos, p):
    n, d_model = x2.shape
    tm = 256 if n % 256 == 0 else n
    cos_t, sin_t = _rope_tables(pos)
    row = lambda w: pl.BlockSpec((tm, w), lambda i: (i, 0))
    widths = (D_A, D_A, D_A, D_KB, D_KB, D_B, D_KB, D_A, D_B)
    outs = pl.pallas_call(
        _proj_sample_kernel,
        grid=(n // tm,),
        in_specs=[row(d_model), row(LANES), row(LANES)] + _weight_specs(d_model),
        out_specs=[row(w) for w in widths],
        out_shape=[jax.ShapeDtypeStruct((n, w), BF16 if i == 0 else F32) for i, w in enumerate(widths)],
        compiler_params=pltpu.CompilerParams(dimension_semantics=("arbitrary",), vmem_limit_bytes=VMEM_LIMIT),
        name="proj_sample",
    )(x2, cos_t, sin_t, *_weight_args(p))
    return dict(zip(("q", "k", "v", "qb", "kb", "vb", "la", "ga", "gb"), outs))


def _band_attn_kernel(bias_ref, q_ref, kp_ref, kc_ref, vp_ref, vc_ref, o_ref, lse_ref, kbuf, vbuf):
    tq = BAND
    kbuf[0:tq] = kp_ref[...]
    kbuf[tq:] = kc_ref[...]
    vbuf[0:tq] = vp_ref[...]
    vbuf[tq:] = vc_ref[...]
    table0 = jnp.where(pl.program_id(2) == 0, 0, 1)
    lane = lax.broadcasted_iota(jnp.int32, (tq, LANES), 1)
    low = lane < HD_A
    for j in range(q_ref.shape[0] // tq):
        qrows = slice(j * tq, (j + 1) * tq)
        krows = slice(j * tq, (j + 2) * tq)
        m_all = jnp.zeros((tq, LANES), F32)
        den_all = jnp.ones((tq, LANES), F32)
        for hp in range(D_A // LANES):
            cs = slice(hp * LANES, (hp + 1) * LANES)
            qp = q_ref[qrows, cs]
            zero = jnp.zeros_like(qp)
            qs = jnp.concatenate([jnp.where(low, qp, zero), jnp.where(low, zero, qp)], axis=0)
            s = _dot_nt(qs, kbuf[krows, cs]) + (bias_ref[table0] if j == 0 else bias_ref[1])
            m = jnp.max(s, axis=-1, keepdims=True)
            pr = jnp.exp(s - m)
            den = jnp.sum(pr, axis=-1, keepdims=True)
            pv = _dot(pr.astype(BF16), vbuf[krows, cs]) / den
            o_ref[qrows, cs] = jnp.where(low, pv[:tq], pv[tq:]).astype(BF16)
            for e in range(2):
                here = lane == 2 * hp + e
                m_all = jnp.where(here, m[e * tq:(e + 1) * tq], m_all)
                den_all = jnp.where(here, den[e * tq:(e + 1) * tq], den_all)
        lse_ref[qrows, :] = m_all + jnp.log(den_all)


def _band_bias():
    i = np.arange(2 * BAND)[:, None] % BAND
    j = np.arange(2 * BAND)[None, :]
    ok = (i + BAND - j >= 0) & (i + BAND - j <= BAND)
    tables = np.stack([ok & (j >= BAND), ok])
    return jnp.asarray(np.where(tables, 0.0, -np.inf), F32)


def _band_attn(q, k, v):
    bsz, ncls, length, _ = q.shape
    tq = min(ATTN_QUERIES, length)
    ratio = tq // BAND
    cur = lambda b, c, i: (b, c, i, 0)
    prev = lambda b, c, i: (b, c, jnp.maximum(i * ratio - 1, 0), 0)
    cur_spec = lambda w: pl.BlockSpec((None, None, tq, w), cur)
    prev_spec = pl.BlockSpec((None, None, BAND, D_A), prev)
    return pl.pallas_call(
        _band_attn_kernel,
        grid=(bsz, ncls, length // tq),
        in_specs=[_const_spec((2, 2 * BAND, 2 * BAND)), cur_spec(D_A), prev_spec, cur_spec(D_A),
                  prev_spec, cur_spec(D_A)],
        out_specs=[cur_spec(D_A), cur_spec(LANES)],
        out_shape=[jax.ShapeDtypeStruct(q.shape, BF16), jax.ShapeDtypeStruct((bsz, ncls, length, LANES), F32)],
        scratch_shapes=[pltpu.VMEM((BAND + tq, D_A), BF16)] * 2,
        compiler_params=pltpu.CompilerParams(dimension_semantics=("arbitrary",) * 3, vmem_limit_bytes=VMEM_LIMIT),
        name=f"band_attn_c{ncls}",
    )(_band_bias(), q, k, k, v, v)


def _sample_attn_kernel(q_ref, kn_ref, vn_ref, kc_ref, vc_ref, wc_ref, wn_ref, o_ref):
    t = q_ref.shape[0]
    q = q_ref[...]
    head = _div2(lax.broadcasted_iota(jnp.int32, (t, D_A), 1), HD_A)
    zero = jnp.zeros_like(q)
    qbd = jnp.concatenate([jnp.where(head == h, q, zero) for h in range(H_A)], axis=0)
    wc, wn = wc_ref[...], wn_ref[...]
    sc = jnp.where(wc > 0, _dot(qbd, kc_ref[...].astype(BF16)), -jnp.inf)
    sn = jnp.where(wn > 0, _dot_nt(qbd, kn_ref[...].astype(BF16)), -jnp.inf)
    m = jnp.maximum(jnp.max(sc, axis=-1, keepdims=True), jnp.max(sn, axis=-1, keepdims=True))
    pc = wc * jnp.exp(sc - m)
    pn = wn * jnp.exp(sn - m)
    den = jnp.sum(pc, axis=-1, keepdims=True) + jnp.sum(pn, axis=-1, keepdims=True)
    o = (_dot_nt(pc.astype(BF16), vc_ref[...].astype(BF16)) + _dot(pn.astype(BF16), vn_ref[...].astype(BF16))) / den
    acc = jnp.zeros((t, D_A), F32)
    for h in range(H_A):
        acc = jnp.where(head == h, o[h * t:(h + 1) * t], acc)
    o_ref[...] = acc


def _sample_key_weights(win, t_len):
    tt = np.arange(t_len)[:, None]
    rel_c = win + tt - np.arange(win)[None, :]
    rel_n = tt - np.arange(t_len)[None, :]
    def count(rel):
        c = np.zeros(rel.shape, np.float32)
        for window, dil in DILATED:
            c += (rel >= 0) & (rel % dil == 0) & (rel // dil <= window // dil)
        return c
    return np.tile(count(rel_c), (H_A, 1)), np.tile(count(rel_n), (H_A, 1))


def _sample_attn(q, k_new, v_new, k_cache, v_cache):
    bsz, t_len, _ = q.shape
    win = k_cache.shape[2]
    wc, wn = _sample_key_weights(win, t_len)
    per_b = lambda rows: pl.BlockSpec((None, rows, D_A), lambda b: (b, 0, 0))
    cache = pl.BlockSpec((None, D_A, win), lambda b: (b, 0, 0))
    return pl.pallas_call(
        _sample_attn_kernel,
        grid=(bsz,),
        in_specs=[per_b(t_len), per_b(t_len), per_b(t_len), cache, cache,
                  _const_spec(wc.shape), _const_spec(wn.shape)],
        out_specs=per_b(t_len),
        out_shape=jax.ShapeDtypeStruct((bsz, t_len, D_A), F32),
        compiler_params=pltpu.CompilerParams(dimension_semantics=("arbitrary",), vmem_limit_bytes=VMEM_LIMIT),
        name="sample_attn",
    )(q, k_new, v_new, k_cache, v_cache, jnp.asarray(wc), jnp.asarray(wn))


def _stack_heads(x):
    low = lax.broadcasted_iota(jnp.int32, x.shape, 1) < DK_B
    zero = jnp.zeros_like(x)
    return jnp.concatenate([jnp.where(low, x, zero), jnp.where(low, zero, x)], axis=0)


def _gla_chunk(q, k, v, g, state, slabs, gw, bdx):
    c = q.shape[0]
    row = lax.broadcasted_iota(jnp.int32, (c, c), 0)
    col = lax.broadcasted_iota(jnp.int32, (c, c), 1)
    tri = (row >= col).astype(BF16)
    g_hi = g.astype(BF16)
    g_lo = (g - g_hi.astype(F32)).astype(BF16)
    b = _dot(tri, g_hi) + _dot(tri, g_lo)
    b_last = b[c - 1:c, :]
    q_in = q * jnp.exp(b)
    k_dec = k * jnp.exp(b_last - b)
    trow2 = _mod2(lax.broadcasted_iota(jnp.int32, (2 * c, c), 0), c)
    col2 = lax.broadcasted_iota(jnp.int32, (2 * c, c), 1)

    levels = []
    p_blk = c
    while p_blk > GLA_SUB:
        half = p_blk // 2
        pieces = [jnp.broadcast_to(b[s + half - 1:s + half, :], (p_blk, D_KB)) for s in range(0, c, p_blk)]
        bm = pieces[0] if len(pieces) == 1 else jnp.concatenate(pieces, axis=0)
        q_l = q * jnp.exp(jnp.minimum(b - bm, 0.0))
        k_l = k * jnp.exp(jnp.minimum(bm - b, 0.0))
        region = ((_div2(trow2, p_blk) == _div2(col2, p_blk)) & (_mod2(trow2, p_blk) >= half)
                  & (_mod2(col2, p_blk) < half))
        levels.append((q_l, k_l, region))
        p_blk = half

    n_ks = D_KB // LANES
    for j in range(n_ks):
        slabs[j] = k[:, j * LANES:(j + 1) * LANES]
        slabs[n_ks + j] = b[:, j * LANES:(j + 1) * LANES]

    def block_row(first_slab, s):
        return jnp.concatenate(
            [jnp.concatenate([jnp.broadcast_to(slabs[first_slab + j, g + s:g + s + 1, :], (GLA_SUB, LANES))
                              for g in range(0, c, GLA_SUB)], axis=0) for j in range(n_ks)], axis=1)

    sub = _mod2(lax.broadcasted_iota(jnp.int32, (c, D_KB), 0), GLA_SUB)
    terms = []
    for s in range(GLA_SUB):
        b_s = block_row(n_ks, s)
        decay = jnp.exp(b - b_s if s == 0 else jnp.where(sub >= s, b - b_s, -jnp.inf))
        terms.append(q * block_row(0, s) * decay)
    sc_d = _dot(jnp.concatenate(terms, axis=0).astype(BF16), bdx)
    key_in_block = col2 - (trow2 - _mod2(trow2, GLA_SUB))

    outs, new_state = [], []
    for p in range(H_B // 2):
        ks = slice(p * LANES, (p + 1) * LANES)
        st = state[p]
        o_pair = _dot_nt(_stack_heads(q_in[:, ks]).astype(BF16), st.astype(BF16))
        sc = jnp.zeros((2 * c, c), F32)
        for q_l, k_l, region in levels:
            s_l = _dot_nt(_stack_heads(q_l[:, ks]).astype(BF16), k_l[:, ks].astype(BF16))
            sc = jnp.where(region, s_l, sc)
        for s in range(GLA_SUB):
            d_s = jnp.concatenate([sc_d[s * c:(s + 1) * c, (2 * p + e) * DV_B:(2 * p + e) * DV_B + c]
                                   for e in range(2)], axis=0)
            sc = jnp.where(key_in_block == s, d_s, sc)
        sc = sc.astype(BF16)
        upd = []
        for e in range(2):
            hidx = 2 * p + e
            v_h = v[:, hidx * DV_B:(hidx + 1) * DV_B].astype(BF16)
            o_h = o_pair[e * c:(e + 1) * c] + _dot(sc[e * c:(e + 1) * c], v_h)
            ms = jnp.mean(o_h * o_h, axis=-1, keepdims=True)
            outs.append(o_h * lax.rsqrt(ms + EPS) * gw)
            upd.append(_dot_tn(v_h, k_dec[:, ks].astype(BF16)))
        low = lax.broadcasted_iota(jnp.int32, (DV_B, LANES), 1) < DK_B
        new_state.append(st * jnp.exp(b_last[:, ks]) + jnp.where(low, upd[0], upd[1]))
    return jnp.concatenate(outs, axis=1), new_state


def _gla_kernel(q_ref, k_ref, v_ref, g_ref, s0_ref, gw_ref, bdx_ref, o_ref, sfin_ref, st_ref, slabs, *,
                chunk, n_chunks):
    step = pl.program_id(1)

    @pl.when(step == 0)
    def _():
        for p in range(H_B // 2):
            st_ref[p] = s0_ref[p * LANES:(p + 1) * LANES, :].T

    state = [st_ref[p] for p in range(H_B // 2)]
    for ci in range(n_chunks):
        rs = slice(ci * chunk, (ci + 1) * chunk)
        o, state = _gla_chunk(q_ref[rs, :], k_ref[rs, :], v_ref[rs, :], g_ref[rs, :], state, slabs.at[ci],
                              gw_ref[...], bdx_ref[...])
        o_ref[rs, :] = o.astype(o_ref.dtype)
    for p in range(H_B // 2):
        st_ref[p] = state[p]

    @pl.when(step == pl.num_programs(1) - 1)
    def _():
        for p in range(H_B // 2):
            sfin_ref[p * LANES:(p + 1) * LANES, :] = st_ref[p].T


def _gla(qb, kb, vb, la, s0, gw, bdx):
    bsz, length, _ = qb.shape
    chunk = int(np.gcd(length, GLA_CHUNK))
    n_chunks = GLA_STEP_CHUNKS if length % (GLA_STEP_CHUNKS * chunk) == 0 else 1
    tl = chunk * n_chunks
    tok = lambda w: pl.BlockSpec((None, tl, w), lambda b, i: (b, i, 0))
    st = pl.BlockSpec((None, H_B * DK_B, DV_B), lambda b, i: (b, 0, 0))
    return pl.pallas_call(
        functools.partial(_gla_kernel, chunk=chunk, n_chunks=n_chunks),
        grid=(bsz, length // tl),
        in_specs=[tok(D_KB), tok(D_KB), tok(D_B), tok(D_KB), st, _const_spec((1, DV_B)), _const_spec((D_KB, D_B))],
        out_specs=[tok(D_B), st],
        out_shape=[jax.ShapeDtypeStruct((bsz, length, D_B), BF16 if tl % 16 == 0 else F32),
                   jax.ShapeDtypeStruct((bsz, H_B * DK_B, DV_B), F32)],
        scratch_shapes=[pltpu.VMEM((H_B // 2, DV_B, LANES), F32),
                        pltpu.VMEM((n_chunks, 2 * D_KB // LANES, chunk, LANES), F32)],
        compiler_params=pltpu.CompilerParams(dimension_semantics=("arbitrary", "arbitrary"),
                                             vmem_limit_bytes=VMEM_LIMIT),
        name="gla",
    )(qb, kb, vb, la, s0, gw, bdx)


def _mix_and_project(o_a, ga, o_b, gb, x, wo):
    mixed = jnp.concatenate([o_a * ga, o_b * gb], axis=1).astype(BF16)
    return x + _dot(mixed, wo)


def _out_prompt_kernel(*refs):
    nb = len(DILATED)
    att = refs[:2 * nb]
    ga, ob, gb, x_ref, ex_ref, wo_ref, y_ref = refs[2 * nb:2 * nb + 7]
    scratch = list(refs[2 * nb + 7:])
    tm = x_ref.shape[0]
    outs, lses = [], []
    for bi, (_, dil) in enumerate(DILATED):
        o_ref, l_ref = att[2 * bi], att[2 * bi + 1]
        if dil == 1:
            outs.append(o_ref[0].astype(F32))
            lses.append(l_ref[0])
            continue
        oslab, lslab = scratch.pop(0), scratch.pop(0)
        for c in range(dil):
            rows = pl.ds(c, tm // dil, stride=dil)
            for j in range(D_A // LANES):
                oslab[j, rows, :] = o_ref[c, :, j * LANES:(j + 1) * LANES].astype(F32)
            lslab[rows, :] = l_ref[c]
        outs.append(jnp.concatenate([oslab[j] for j in range(D_A // LANES)], axis=1))
        lses.append(lslab[...])
    m = functools.reduce(jnp.maximum, lses)
    es = [jnp.exp(l - m) for l in lses]
    inv = 1.0 / functools.reduce(lambda a, b: a + b, es)
    spread = lambda a: _dot((a * inv).astype(BF16), ex_ref[...])
    o_a = functools.reduce(lambda a, b: a + b, [spread(e) * o for e, o in zip(es, outs)])
    y_ref[...] = _mix_and_project(o_a, ga[...], ob[...], gb[...], x_ref[...], wo_ref[...])


def _out_sample_kernel(oa, ga, ob, gb, x_ref, wo_ref, y_ref):
    y_ref[...] = _mix_and_project(oa[...], ga[...], ob[...], gb[...], x_ref[...], wo_ref[...])


def _out_prompt(att, ga, o_b, gb, x, p):
    bsz, seq, d_model = x.shape
    tm = PROJ_TOKENS
    nat_spec = lambda w: pl.BlockSpec((None, tm, w), lambda b, i: (b, i, 0))
    cls_spec = lambda dil, w: pl.BlockSpec((None, dil, tm // dil, w), lambda b, i: (b, 0, i, 0))
    att_specs, att_args, scratch = [], [], []
    for (_, dil), (o, lse) in zip(DILATED, att):
        att_specs += [cls_spec(dil, D_A), cls_spec(dil, LANES)]
        att_args += [o, lse]
        if dil > 1:
            scratch += [pltpu.VMEM((D_A // LANES, tm, LANES), F32), pltpu.VMEM((tm, LANES), F32)]
    return pl.pallas_call(
        _out_prompt_kernel,
        grid=(bsz, seq // tm),
        in_specs=att_specs + [nat_spec(D_A), nat_spec(D_B), nat_spec(D_B), nat_spec(d_model),
                              _const_spec((LANES, D_A)), _const_spec((D_A + D_B, d_model))],
        out_specs=nat_spec(d_model),
        out_shape=jax.ShapeDtypeStruct((bsz, seq, d_model), F32),
        scratch_shapes=scratch,
        compiler_params=pltpu.CompilerParams(dimension_semantics=("arbitrary", "arbitrary"),
                                             vmem_limit_bytes=VMEM_LIMIT),
        name="out_prompt",
    )(*att_args, ga, o_b, gb, x, p["ex"], p["wo"])


def _out_sample(o_a, ga, o_b, gb, x2, p):
    n, d_model = x2.shape
    tm = 256 if n % 256 == 0 else n
    row = lambda w: pl.BlockSpec((tm, w), lambda i: (i, 0))
    return pl.pallas_call(
        _out_sample_kernel,
        grid=(n // tm,),
        in_specs=[row(D_A), row(D_A), row(D_B), row(D_B), row(d_model), _const_spec((D_A + D_B, d_model))],
        out_specs=row(d_model),
        out_shape=jax.ShapeDtypeStruct((n, d_model), F32),
        compiler_params=pltpu.CompilerParams(dimension_semantics=("arbitrary",), vmem_limit_bytes=VMEM_LIMIT),
        name="out_sample",
    )(o_a, ga, o_b, gb, x2, p["wo"])


def _layer_params(norm_w, w_in, w_gate_up, b_gate, q_norm_w, k_norm_w, gla_norm_w, w_out):
    d_model = w_in.shape[0]
    split = np.cumsum((D_A, D_A, D_A, D_KB, D_KB, D_B, GATE_RANK, D_A, D_B))
    glr = w_in[:, split[5]:split[6]]
    w = jnp.concatenate([w_in[:, :split[5]], glr, jnp.zeros((d_model, LANES - GATE_RANK), w_in.dtype),
                         w_in[:, split[6]:]], axis=1).astype(BF16)
    head_of = np.arange(D_A) // HD_A
    bd = jnp.asarray(head_of[:, None] == head_of[None, :], BF16)
    ex = jnp.asarray(np.arange(LANES)[:, None] == head_of[None, :], BF16)
    bdx = jnp.asarray((np.arange(D_KB) // DK_B)[:, None] == (np.arange(D_B) // DV_B)[None, :], BF16)
    wgu = jnp.concatenate([w_gate_up, jnp.zeros((LANES - GATE_RANK, D_KB), w_gate_up.dtype)], axis=0).astype(BF16)
    return dict(nw=norm_w.reshape(1, d_model), w=w, bd=bd, ex=ex, bdx=bdx,
                wq=jnp.tile(q_norm_w, H_A).reshape(1, D_A), wk=jnp.tile(k_norm_w, H_A).reshape(1, D_A),
                wgu=wgu, bg=b_gate.reshape(1, D_KB), gw=gla_norm_w.reshape(1, DV_B), wo=w_out.astype(BF16))


def _prompt_layer(x, p, keep):
    bsz, seq, _ = x.shape
    r = _proj_prompt(x, p, keep)
    by_class = lambda a, dil: a.reshape(bsz, dil, seq // dil, D_A)
    att = [_band_attn(*(by_class(r[f"{n}{dil}"], dil) for n in "qkv")) for _, dil in DILATED]
    o_b, s_fin = _gla(r["qb"], r["kb"], r["vb"], r["la"], jnp.zeros((bsz, H_B * DK_B, DV_B), F32),
                      p["gw"], p["bdx"])
    y = _out_prompt(att, r["ga"], o_b, r["gb"], x, p)
    win = lambda a: a.reshape(bsz, H_A, HD_A, keep).transpose(0, 3, 1, 2)
    return y, win(r["kt"]), win(r["vt"]), s_fin.reshape(bsz, H_B, DK_B, DV_B)


def _sample_layer(x, p, k_cache, v_cache, s0):
    bsz, t_len, d_model = x.shape
    win = k_cache.shape[1]
    pos = PAST_LEN + jnp.tile(jnp.arange(t_len), bsz)
    r = _proj_sample(x.reshape(bsz * t_len, d_model), pos, p)
    tok = lambda a: a.reshape(bsz, t_len, a.shape[-1])
    by_pos = lambda a: a.transpose(0, 2, 3, 1).reshape(bsz, D_A, win)
    o_a = _sample_attn(tok(r["q"]), tok(r["k"]), tok(r["v"]), by_pos(k_cache), by_pos(v_cache))
    o_b, s_fin = _gla(tok(r["qb"]), tok(r["kb"]), tok(r["vb"]), tok(r["la"]),
                      s0.reshape(bsz, H_B * DK_B, DV_B), p["gw"], p["bdx"])
    y = _out_sample(o_a.reshape(bsz * t_len, D_A), r["ga"], o_b.reshape(bsz * t_len, D_B), r["gb"],
                    x.reshape(bsz * t_len, d_model), p)
    return (y.reshape(bsz, t_len, d_model), r["k"].reshape(bsz, t_len, H_A, HD_A),
            r["v"].reshape(bsz, t_len, H_A, HD_A), s_fin.reshape(bsz, H_B, DK_B, DV_B))


def kernel(x_prompt, x_sample, cache_k_win, cache_v_win, state_gla, norm_w, w_in, w_gate_up, b_gate,
           q_norm_w, k_norm_w, gla_norm_w, w_out):
    seq = x_prompt.shape[1]
    assert seq % (MAX_DIL * BAND) == 0, "prompt length must tile into residue-class bands"
    assert all(w // d == BAND and MAX_DIL % d == 0 for w, d in DILATED)
    keep = min(MAX_WINDOW, seq)
    hp, hs = x_prompt, x_sample
    outs = [[] for _ in range(6)]
    for layer in range(w_in.shape[0]):
        p = _layer_params(norm_w[layer], w_in[layer], w_gate_up[layer], b_gate[layer], q_norm_w[layer],
                          k_norm_w[layer], gla_norm_w[layer], w_out[layer])
        hp, kp, vp, st_p = _prompt_layer(hp, p, keep)
        hs, kn, vn, st_s = _sample_layer(hs, p, cache_k_win[layer], cache_v_win[layer], state_gla[layer])
        for lst, val in zip(outs, (kp, vp, st_p.astype(state_gla.dtype), kn, vn, st_s.astype(state_gla.dtype))):
            lst.append(val)
    return (hp, hs) + tuple(jnp.stack(o) for o in outs)
```

```python
import functools

import numpy as np
import jax
import jax.numpy as jnp
from jax import lax
from jax.experimental import pallas as pl
from jax.experimental.pallas import tpu as pltpu

F32 = jnp.float32
BF16 = jnp.bfloat16

H_A, HD_A = 8, 64
D_A = H_A * HD_A
H_B, DK_B, DV_B = 4, 64, 128
D_KB = H_B * DK_B
D_B = H_B * DV_B
GATE_RANK = 16
GATE_TAU = 16.0
DILATED = ((128, 1), (512, 4), (2048, 16))
MAX_DIL = 16
MAX_WINDOW = 2048
BAND = 128
ROPE_THETA = 10000.0
EPS = 1e-6
PAST_LEN = 8192
GLA_CHUNK = 128
GLA_STEP_CHUNKS = 4
GLA_SHORT_BATCH = 8
GLA_SUB = 8
LANES = 128
MXU_DIM = 256
VMEM_LIMIT = 56 * 1024 * 1024
PROJ_TOKENS = 512
ATTN_QUERIES = 512

_SECTIONS = (("q", D_A), ("k", D_A), ("v", D_A), ("qb", D_KB), ("kb", D_KB), ("vb", D_B),
             ("glr", LANES), ("ga", D_A), ("gb", D_B))
_OFF = {}
_o = 0
for _n, _w in _SECTIONS:
    _OFF[_n] = (_o, _o + _w)
    _o += _w
D_PAD = _o


def _mod2(x, n):
    assert n & (n - 1) == 0
    return jnp.bitwise_and(x, n - 1)


def _div2(x, n):
    assert n & (n - 1) == 0
    return jnp.right_shift(x, n.bit_length() - 1)


def _dot(a, b):
    return jnp.dot(a, b, preferred_element_type=F32)


def _dot_nt(a, b):
    return lax.dot_general(a, b, (((1,), (1,)), ((), ())), preferred_element_type=F32)


def _dot_tn(a, b):
    return lax.dot_general(a, b, (((0,), (0,)), ((), ())), preferred_element_type=F32)


def _proj_tile(x, cos, sin, nw, w_ref, bd, wq, wk, wgu, bg, emit):
    ms = jnp.mean(x * x, axis=-1, keepdims=True)
    h = (x * lax.rsqrt(ms + EPS) * nw).astype(BF16)
    sec = lambda name: _dot(h, w_ref[:, _OFF[name][0]:_OFF[name][1]])
    lane = lax.broadcasted_iota(jnp.int32, (x.shape[0], LANES), 1)
    first_half = _mod2(lane, HD_A) < (HD_A // 2)

    def qk_norm_rope(zz, wn):
        sq = (zz * zz).astype(BF16)
        ss = jnp.concatenate([_dot(sq[:, c:c + MXU_DIM], bd) for c in range(0, D_A, MXU_DIM)], axis=1)
        y = zz * lax.rsqrt(ss * (1.0 / HD_A) + EPS) * wn
        outs = []
        for j in range(D_A // LANES):
            yj = y[:, j * LANES:(j + 1) * LANES]
            swapped = jnp.where(first_half, pltpu.roll(yj, LANES - HD_A // 2, 1),
                                pltpu.roll(yj, HD_A // 2, 1))
            outs.append(yj * cos + swapped * sin)
        return jnp.concatenate(outs, axis=1)

    emit("q", qk_norm_rope(sec("q"), wq) * (HD_A ** -0.5))
    emit("k", qk_norm_rope(sec("k"), wk))
    emit("v", sec("v"))
    emit("qb", sec("qb") * (DK_B ** -0.5))
    emit("kb", sec("kb"))
    emit("vb", sec("vb"))
    xg = _dot(sec("glr").astype(BF16), wgu) + bg
    emit("la", (jnp.minimum(xg, 0.0) - jnp.log1p(jnp.exp(-jnp.abs(xg)))) * (1.0 / GATE_TAU))
    for name in ("ga", "gb"):
        gate = sec(name)
        emit(name, gate / (1.0 + jnp.exp(-gate)))


def _proj_prompt_kernel(x_ref, cos_ref, sin_ref, nw_ref, w_ref, bd_ref, wq_ref, wk_ref, wgu_ref, bg_ref,
                        q1, k1, v1, q4, k4, v4, q16, k16, v16, kt, vt, qb, kb, vb, la, ga, gb, slabs):
    tm = x_ref.shape[0]
    plain = dict(qb=qb, kb=kb, vb=vb, la=la, ga=ga, gb=gb)
    attn = dict(q=(0, (q1, q4, q16), None), k=(1, (k1, k4, k16), kt), v=(2, (v1, v4, v16), vt))

    def emit(name, val):
        if name in plain:
            plain[name][...] = val.astype(plain[name].dtype)
            return
        si, refs, win_t = attn[name]
        slab = slabs.at[si]
        for j in range(D_A // LANES):
            slab[j] = val[:, j * LANES:(j + 1) * LANES]
        for (_, dil), ref in zip(DILATED, refs):
            if dil == 1:
                ref[...] = val.astype(BF16)
                continue
            for c in range(dil):
                for j in range(D_A // LANES):
                    ref[c, :, j * LANES:(j + 1) * LANES] = slab[j, pl.ds(c, tm // dil, stride=dil), :].astype(BF16)
        if win_t is not None:
            win_t[...] = val.T

    _proj_tile(x_ref[...], cos_ref[...], sin_ref[...], nw_ref[...], w_ref, bd_ref[...],
               wq_ref[...], wk_ref[...], wgu_ref[...], bg_ref[...], emit)


def _proj_sample_kernel(x_ref, cos_ref, sin_ref, nw_ref, w_ref, bd_ref, wq_ref, wk_ref, wgu_ref, bg_ref,
                        q, k, v, qb, kb, vb, la, ga, gb):
    refs = dict(q=q, k=k, v=v, qb=qb, kb=kb, vb=vb, la=la, ga=ga, gb=gb)

    def emit(name, val):
        refs[name][...] = val.astype(refs[name].dtype)

    _proj_tile(x_ref[...], cos_ref[...], sin_ref[...], nw_ref[...], w_ref, bd_ref[...],
               wq_ref[...], wk_ref[...], wgu_ref[...], bg_ref[...], emit)


def _rope_tables(pos):
    half = HD_A // 2
    inv_freq = ROPE_THETA ** (-jnp.arange(half, dtype=F32) / half)
    ang = pos.astype(F32)[:, None] * inv_freq[None, :]
    cos, sin = jnp.cos(ang), jnp.sin(ang)
    reps = LANES // HD_A
    cos_t = jnp.tile(jnp.concatenate([cos, cos], axis=1), (1, reps))
    sin_t = jnp.tile(jnp.concatenate([-sin, sin], axis=1), (1, reps))
    return cos_t, sin_t


def _const_spec(shape):
    return pl.BlockSpec(shape, lambda *_: (0,) * len(shape))


def _weight_args(p):
    return (p["nw"], p["w"], p["bd"], p["wq"], p["wk"], p["wgu"], p["bg"])


def _weight_specs(d_model):
    return [_const_spec((1, d_model)), _const_spec((d_model, D_PAD)), _const_spec((MXU_DIM, MXU_DIM)),
            _const_spec((1, D_A)), _const_spec((1, D_A)), _const_spec((LANES, D_KB)), _const_spec((1, D_KB))]


def _proj_prompt(x, p, keep):
    bsz, seq, d_model = x.shape
    tm = PROJ_TOKENS
    n_skip = (seq - keep) // tm
    cos_t, sin_t = _rope_tables(jnp.arange(seq))
    res = lambda dil: jax.ShapeDtypeStruct((bsz, dil, seq // dil, D_A), BF16)
    nat = lambda w, dt: jax.ShapeDtypeStruct((bsz, seq, w), dt)
    win_t = jax.ShapeDtypeStruct((bsz, D_A, keep), F32)
    res_spec = lambda dil: pl.BlockSpec((None, dil, tm // dil, D_A), lambda b, i: (b, 0, i, 0))
    nat_spec = lambda w: pl.BlockSpec((None, tm, w), lambda b, i: (b, i, 0))
    win_spec = pl.BlockSpec((None, D_A, tm), lambda b, i: (b, 0, jnp.maximum(i - n_skip, 0)))
    tab_spec = pl.BlockSpec((tm, LANES), lambda b, i: (i, 0))
    qkv_specs, qkv_shapes = [], []
    for _, dil in DILATED:
        qkv_specs += [nat_spec(D_A) if dil == 1 else res_spec(dil)] * 3
        qkv_shapes += [nat(D_A, BF16) if dil == 1 else res(dil)] * 3
    outs = pl.pallas_call(
        _proj_prompt_kernel,
        grid=(bsz, seq // tm),
        in_specs=[nat_spec(d_model), tab_spec, tab_spec] + _weight_specs(d_model),
        out_specs=qkv_specs + [win_spec] * 2
                  + [nat_spec(D_KB), nat_spec(D_KB), nat_spec(D_B), nat_spec(D_KB), nat_spec(D_A), nat_spec(D_B)],
        out_shape=qkv_shapes + [win_t] * 2
                  + [nat(D_KB, F32), nat(D_KB, F32), nat(D_B, F32), nat(D_KB, F32), nat(D_A, BF16), nat(D_B, BF16)],
        scratch_shapes=[pltpu.VMEM((3, D_A // LANES, tm, LANES), F32)],
        compiler_params=pltpu.CompilerParams(dimension_semantics=("arbitrary", "arbitrary"),
                                             vmem_limit_bytes=VMEM_LIMIT),
        name="proj_prompt",
    )(x, cos_t, sin_t, *_weight_args(p))
    names = [f"{n}{dil}" for _, dil in DILATED for n in "qkv"] + ["kt", "vt", "qb", "kb", "vb", "la", "ga", "gb"]
    return dict(zip(names, outs))


def _proj_sample(x2, pos, p):
    n, d_model = x2.shape
    tm = 256 if n % 256 == 0 else n
    cos_t, sin_t = _rope_tables(pos)
    row = lambda w: pl.BlockSpec((tm, w), lambda i: (i, 0))
    widths = (D_A, D_A, D_A, D_KB, D_KB, D_B, D_KB, D_A, D_B)
    outs = pl.pallas_call(
        _proj_sample_kernel,
        grid=(n // tm,),
        in_specs=[row(d_model), row(LANES), row(LANES)] + _weight_specs(d_model),
        out_specs=[row(w) for w in widths],
        out_shape=[jax.ShapeDtypeStruct((n, w), BF16 if i == 0 else F32) for i, w in enumerate(widths)],
        compiler_params=pltpu.CompilerParams(dimension_semantics=("arbitrary",), vmem_limit_bytes=VMEM_LIMIT),
        name="proj_sample",
    )(x2, cos_t, sin_t, *_weight_args(p))
    return dict(zip(("q", "k", "v", "qb", "kb", "vb", "la", "ga", "gb"), outs))


def _band_attn_kernel(bias_ref, q_ref, kp_ref, kc_ref, vp_ref, vc_ref, o_ref, lse_ref, kbuf, vbuf):
    tq = BAND
    kbuf[0:tq] = kp_ref[...]
    kbuf[tq:] = kc_ref[...]
    vbuf[0:tq] = vp_ref[...]
    vbuf[tq:] = vc_ref[...]
    table0 = jnp.where(pl.program_id(2) == 0, 0, 1)
    lane = lax.broadcasted_iota(jnp.int32, (tq, LANES), 1)
    low = lane < HD_A
    pairs = [slice(hp * LANES, (hp + 1) * LANES) for hp in range(D_A // LANES)]
    n_blocks = q_ref.shape[0] // tq

    def score_phase(j):
        scores = []
        for cs in pairs:
            qp = q_ref[j * tq:(j + 1) * tq, cs]
            zero = jnp.zeros_like(qp)
            qs = jnp.concatenate([jnp.where(low, qp, zero), jnp.where(low, zero, qp)], axis=0)
            scores.append(_dot_nt(qs, kbuf[j * tq:(j + 2) * tq, cs]) + (bias_ref[table0] if j == 0 else bias_ref[1]))
        return scores

    def softmax_phase(scores):
        probs = []
        for s in scores:
            m = jnp.max(s, axis=-1, keepdims=True)
            pr = jnp.exp(s - m)
            probs.append((pr.astype(BF16), m, jnp.sum(pr, axis=-1, keepdims=True)))
        return probs

    def value_phase(j, probs):
        m_all = jnp.zeros((tq, LANES), F32)
        den_all = jnp.ones((tq, LANES), F32)
        for hp, (cs, (pr, m, den)) in enumerate(zip(pairs, probs)):
            pv = _dot(pr, vbuf[j * tq:(j + 2) * tq, cs]) / den
            o_ref[j * tq:(j + 1) * tq, cs] = jnp.where(low, pv[:tq], pv[tq:]).astype(BF16)
            for e in range(2):
                here = lane == 2 * hp + e
                m_all = jnp.where(here, m[e * tq:(e + 1) * tq], m_all)
                den_all = jnp.where(here, den[e * tq:(e + 1) * tq], den_all)
        lse_ref[j * tq:(j + 1) * tq, :] = m_all + jnp.log(den_all)

    for j in range(n_blocks):
        value_phase(j, softmax_phase(score_phase(j)))


def _band_bias():
    i = np.arange(2 * BAND)[:, None] % BAND
    j = np.arange(2 * BAND)[None, :]
    ok = (i + BAND - j >= 0) & (i + BAND - j <= BAND)
    tables = np.stack([ok & (j >= BAND), ok])
    return jnp.asarray(np.where(tables, 0.0, -np.inf), F32)


def _band_attn(q, k, v):
    bsz, ncls, length, _ = q.shape
    tq = min(ATTN_QUERIES, length)
    ratio = tq // BAND
    cur = lambda b, c, i: (b, c, i, 0)
    prev = lambda b, c, i: (b, c, jnp.maximum(i * ratio - 1, 0), 0)
    cur_spec = lambda w: pl.BlockSpec((None, None, tq, w), cur)
    prev_spec = pl.BlockSpec((None, None, BAND, D_A), prev)
    return pl.pallas_call(
        _band_attn_kernel,
        grid=(bsz, ncls, length // tq),
        in_specs=[_const_spec((2, 2 * BAND, 2 * BAND)), cur_spec(D_A), prev_spec, cur_spec(D_A),
                  prev_spec, cur_spec(D_A)],
        out_specs=[cur_spec(D_A), cur_spec(LANES)],
        out_shape=[jax.ShapeDtypeStruct(q.shape, BF16), jax.ShapeDtypeStruct((bsz, ncls, length, LANES), F32)],
        scratch_shapes=[pltpu.VMEM((BAND + tq, D_A), BF16)] * 2,
        compiler_params=pltpu.CompilerParams(dimension_semantics=("arbitrary",) * 3, vmem_limit_bytes=VMEM_LIMIT),
        name=f"band_attn_c{ncls}",
    )(_band_bias(), q, k, k, v, v)


def _sample_attn_kernel(q_ref, kn_ref, vn_ref, kc_ref, vc_ref, wc_ref, wn_ref, o_ref):
    t = q_ref.shape[0]
    q = q_ref[...]
    head = _div2(lax.broadcasted_iota(jnp.int32, (t, D_A), 1), HD_A)
    zero = jnp.zeros_like(q)
    qbd = jnp.concatenate([jnp.where(head == h, q, zero) for h in range(H_A)], axis=0)
    wc, wn = wc_ref[...], wn_ref[...]
    sc = jnp.where(wc > 0, _dot(qbd, kc_ref[...].astype(BF16)), -jnp.inf)
    sn = jnp.where(wn > 0, _dot_nt(qbd, kn_ref[...].astype(BF16)), -jnp.inf)
    m = jnp.maximum(jnp.max(sc, axis=-1, keepdims=True), jnp.max(sn, axis=-1, keepdims=True))
    pc = wc * jnp.exp(sc - m)
    pn = wn * jnp.exp(sn - m)
    den = jnp.sum(pc, axis=-1, keepdims=True) + jnp.sum(pn, axis=-1, keepdims=True)
    o = (_dot_nt(pc.astype(BF16), vc_ref[...].astype(BF16)) + _dot(pn.astype(BF16), vn_ref[...].astype(BF16))) / den
    acc = jnp.zeros((t, D_A), F32)
    for h in range(H_A):
        acc = jnp.where(head == h, o[h * t:(h + 1) * t], acc)
    o_ref[...] = acc


def _sample_key_weights(win, t_len):
    tt = np.arange(t_len)[:, None]
    rel_c = win + tt - np.arange(win)[None, :]
    rel_n = tt - np.arange(t_len)[None, :]
    def count(rel):
        c = np.zeros(rel.shape, np.float32)
        for window, dil in DILATED:
            c += (rel >= 0) & (rel % dil == 0) & (rel // dil <= window // dil)
        return c
    return np.tile(count(rel_c), (H_A, 1)), np.tile(count(rel_n), (H_A, 1))


def _sample_attn(q, k_new, v_new, k_cache, v_cache):
    bsz, t_len, _ = q.shape
    win = k_cache.shape[2]
    wc, wn = _sample_key_weights(win, t_len)
    per_b = lambda rows: pl.BlockSpec((None, rows, D_A), lambda b: (b, 0, 0))
    cache = pl.BlockSpec((None, D_A, win), lambda b: (b, 0, 0))
    return pl.pallas_call(
        _sample_attn_kernel,
        grid=(bsz,),
        in_specs=[per_b(t_len), per_b(t_len), per_b(t_len), cache, cache,
                  _const_spec(wc.shape), _const_spec(wn.shape)],
        out_specs=per_b(t_len),
        out_shape=jax.ShapeDtypeStruct((bsz, t_len, D_A), F32),
        compiler_params=pltpu.CompilerParams(dimension_semantics=("arbitrary",), vmem_limit_bytes=VMEM_LIMIT),
        name="sample_attn",
    )(q, k_new, v_new, k_cache, v_cache, jnp.asarray(wc), jnp.asarray(wn))


def _stack_heads(x):
    low = lax.broadcasted_iota(jnp.int32, x.shape, 1) < DK_B
    zero = jnp.zeros_like(x)
    return jnp.concatenate([jnp.where(low, x, zero), jnp.where(low, zero, x)], axis=0)


def _gla_chunk(q, k, v, g, state, slabs, gw, bdx):
    c = q.shape[0]
    row = lax.broadcasted_iota(jnp.int32, (c, c), 0)
    col = lax.broadcasted_iota(jnp.int32, (c, c), 1)
    tri = (row >= col).astype(BF16)
    g_hi = g.astype(BF16)
    g_lo = (g - g_hi.astype(F32)).astype(BF16)
    b = _dot(tri, g_hi) + _dot(tri, g_lo)
    b_last = b[c - 1:c, :]
    q_in = q * jnp.exp(b)
    k_dec = k * jnp.exp(b_last - b)
    trow2 = _mod2(lax.broadcasted_iota(jnp.int32, (2 * c, c), 0), c)
    col2 = lax.broadcasted_iota(jnp.int32, (2 * c, c), 1)

    levels = []
    p_blk = c
    while p_blk > GLA_SUB:
        half = p_blk // 2
        pieces = [jnp.broadcast_to(b[s + half - 1:s + half, :], (p_blk, D_KB)) for s in range(0, c, p_blk)]
        bm = pieces[0] if len(pieces) == 1 else jnp.concatenate(pieces, axis=0)
        q_l = q * jnp.exp(jnp.minimum(b - bm, 0.0))
        k_l = k * jnp.exp(jnp.minimum(bm - b, 0.0))
        region = ((_div2(trow2, p_blk) == _div2(col2, p_blk)) & (_mod2(trow2, p_blk) >= half)
                  & (_mod2(col2, p_blk) < half))
        levels.append((q_l, k_l, region))
        p_blk = half

    n_ks = D_KB // LANES
    for j in range(n_ks):
        slabs[j] = k[:, j * LANES:(j + 1) * LANES]
        slabs[n_ks + j] = b[:, j * LANES:(j + 1) * LANES]

    def block_row(first_slab, s):
        return jnp.concatenate(
            [jnp.concatenate([jnp.broadcast_to(slabs[first_slab + j, g + s:g + s + 1, :], (GLA_SUB, LANES))
                              for g in range(0, c, GLA_SUB)], axis=0) for j in range(n_ks)], axis=1)

    sub = _mod2(lax.broadcasted_iota(jnp.int32, (c, D_KB), 0), GLA_SUB)
    terms = []
    for s in range(GLA_SUB):
        b_s = block_row(n_ks, s)
        decay = jnp.exp(b - b_s if s == 0 else jnp.where(sub >= s, b - b_s, -jnp.inf))
        terms.append(q * block_row(0, s) * decay)
    sc_d = _dot(jnp.concatenate(terms, axis=0).astype(BF16), bdx)
    key_in_block = col2 - (trow2 - _mod2(trow2, GLA_SUB))

    outs, new_state = [], []
    for p in range(H_B // 2):
        ks = slice(p * LANES, (p + 1) * LANES)
        st = state[p]
        o_pair = _dot_nt(_stack_heads(q_in[:, ks]).astype(BF16), st.astype(BF16))
        sc = jnp.zeros((2 * c, c), F32)
        for q_l, k_l, region in levels:
            s_l = _dot_nt(_stack_heads(q_l[:, ks]).astype(BF16), k_l[:, ks].astype(BF16))
            sc = jnp.where(region, s_l, sc)
        for s in range(GLA_SUB):
            d_s = jnp.concatenate([sc_d[s * c:(s + 1) * c, (2 * p + e) * DV_B:(2 * p + e) * DV_B + c]
                                   for e in range(2)], axis=0)
            sc = jnp.where(key_in_block == s, d_s, sc)
        sc = sc.astype(BF16)
        upd = []
        for e in range(2):
            hidx = 2 * p + e
            v_h = v[:, hidx * DV_B:(hidx + 1) * DV_B].astype(BF16)
            o_h = o_pair[e * c:(e + 1) * c] + _dot(sc[e * c:(e + 1) * c], v_h)
            ms = jnp.mean(o_h * o_h, axis=-1, keepdims=True)
            outs.append(o_h * lax.rsqrt(ms + EPS) * gw)
            upd.append(_dot_tn(v_h, k_dec[:, ks].astype(BF16)))
        low = lax.broadcasted_iota(jnp.int32, (DV_B, LANES), 1) < DK_B
        new_state.append(st * jnp.exp(b_last[:, ks]) + jnp.where(low, upd[0], upd[1]))
    return jnp.concatenate(outs, axis=1), new_state


def _gla_kernel(q_ref, k_ref, v_ref, g_ref, s0_ref, gw_ref, bdx_ref, o_ref, sfin_ref, st_ref, slabs, *,
                chunk, n_chunks):
    step = pl.program_id(1)
    n_seq = q_ref.shape[0]

    @pl.when(step == 0)
    def _():
        for i in range(n_seq):
            for p in range(H_B // 2):
                st_ref[i, p] = s0_ref[i, p * LANES:(p + 1) * LANES, :].T

    for i in range(n_seq):
        state = [st_ref[i, p] for p in range(H_B // 2)]
        for ci in range(n_chunks):
            rs = slice(ci * chunk, (ci + 1) * chunk)
            o, state = _gla_chunk(q_ref[i, rs, :], k_ref[i, rs, :], v_ref[i, rs, :], g_ref[i, rs, :], state,
                                  slabs.at[i * n_chunks + ci], gw_ref[...], bdx_ref[...])
            o_ref[i, rs, :] = o.astype(o_ref.dtype)
        for p in range(H_B // 2):
            st_ref[i, p] = state[p]

    @pl.when(step == pl.num_programs(1) - 1)
    def _():
        for i in range(n_seq):
            for p in range(H_B // 2):
                sfin_ref[i, p * LANES:(p + 1) * LANES, :] = st_ref[i, p].T


def _gla(qb, kb, vb, la, s0, gw, bdx):
    bsz, length, _ = qb.shape
    chunk = int(np.gcd(length, GLA_CHUNK))
    n_chunks = GLA_STEP_CHUNKS if length % (GLA_STEP_CHUNKS * chunk) == 0 else 1
    tl = chunk * n_chunks
    n_seq = GLA_SHORT_BATCH if length == tl and bsz % GLA_SHORT_BATCH == 0 else 1
    tok = lambda w: pl.BlockSpec((n_seq, tl, w), lambda b, i: (b, i, 0))
    st = pl.BlockSpec((n_seq, H_B * DK_B, DV_B), lambda b, i: (b, 0, 0))
    return pl.pallas_call(
        functools.partial(_gla_kernel, chunk=chunk, n_chunks=n_chunks),
        grid=(bsz // n_seq, length // tl),
        in_specs=[tok(D_KB), tok(D_KB), tok(D_B), tok(D_KB), st, _const_spec((1, DV_B)), _const_spec((D_KB, D_B))],
        out_specs=[tok(D_B), st],
        out_shape=[jax.ShapeDtypeStruct((bsz, length, D_B), BF16 if tl % 16 == 0 else F32),
                   jax.ShapeDtypeStruct((bsz, H_B * DK_B, DV_B), F32)],
        scratch_shapes=[pltpu.VMEM((n_seq, H_B // 2, DV_B, LANES), F32),
                        pltpu.VMEM((n_seq * n_chunks, 2 * D_KB // LANES, chunk, LANES), F32)],
        compiler_params=pltpu.CompilerParams(dimension_semantics=("arbitrary", "arbitrary"),
                                             vmem_limit_bytes=VMEM_LIMIT),
        name="gla",
    )(qb, kb, vb, la, s0, gw, bdx)


def _mix_and_project(o_a, ga, o_b, gb, x, wo):
    mixed = jnp.concatenate([o_a * ga, o_b * gb], axis=1).astype(BF16)
    return x + _dot(mixed, wo)


def _out_prompt_kernel(*refs):
    nb = len(DILATED)
    att = refs[:2 * nb]
    ga, ob, gb, x_ref, ex_ref, wo_ref, y_ref = refs[2 * nb:2 * nb + 7]
    scratch = list(refs[2 * nb + 7:])
    tm = x_ref.shape[0]
    outs, lses = [], []
    for bi, (_, dil) in enumerate(DILATED):
        o_ref, l_ref = att[2 * bi], att[2 * bi + 1]
        if dil == 1:
            outs.append(o_ref[0].astype(F32))
            lses.append(l_ref[0])
            continue
        oslab, lslab = scratch.pop(0), scratch.pop(0)
        for c in range(dil):
            rows = pl.ds(c, tm // dil, stride=dil)
            for j in range(D_A // LANES):
                oslab[j, rows, :] = o_ref[c, :, j * LANES:(j + 1) * LANES].astype(F32)
            lslab[rows, :] = l_ref[c]
        outs.append(jnp.concatenate([oslab[j] for j in range(D_A // LANES)], axis=1))
        lses.append(lslab[...])
    m = functools.reduce(jnp.maximum, lses)
    es = [jnp.exp(l - m) for l in lses]
    inv = 1.0 / functools.reduce(lambda a, b: a + b, es)
    spread = lambda a: _dot((a * inv).astype(BF16), ex_ref[...])
    o_a = functools.reduce(lambda a, b: a + b, [spread(e) * o for e, o in zip(es, outs)])
    y_ref[...] = _mix_and_project(o_a, ga[...], ob[...], gb[...], x_ref[...], wo_ref[...])


def _out_sample_kernel(oa, ga, ob, gb, x_ref, wo_ref, y_ref):
    y_ref[...] = _mix_and_project(oa[...], ga[...], ob[...], gb[...], x_ref[...], wo_ref[...])


def _out_prompt(att, ga, o_b, gb, x, p):
    bsz, seq, d_model = x.shape
    tm = PROJ_TOKENS
    nat_spec = lambda w: pl.BlockSpec((None, tm, w), lambda b, i: (b, i, 0))
    cls_spec = lambda dil, w: pl.BlockSpec((None, dil, tm // dil, w), lambda b, i: (b, 0, i, 0))
    att_specs, att_args, scratch = [], [], []
    for (_, dil), (o, lse) in zip(DILATED, att):
        att_specs += [cls_spec(dil, D_A), cls_spec(dil, LANES)]
        att_args += [o, lse]
        if dil > 1:
            scratch += [pltpu.VMEM((D_A // LANES, tm, LANES), F32), pltpu.VMEM((tm, LANES), F32)]
    return pl.pallas_call(
        _out_prompt_kernel,
        grid=(bsz, seq // tm),
        in_specs=att_specs + [nat_spec(D_A), nat_spec(D_B), nat_spec(D_B), nat_spec(d_model),
                              _const_spec((LANES, D_A)), _const_spec((D_A + D_B, d_model))],
        out_specs=nat_spec(d_model),
        out_shape=jax.ShapeDtypeStruct((bsz, seq, d_model), F32),
        scratch_shapes=scratch,
        compiler_params=pltpu.CompilerParams(dimension_semantics=("arbitrary", "arbitrary"),
                                             vmem_limit_bytes=VMEM_LIMIT),
        name="out_prompt",
    )(*att_args, ga, o_b, gb, x, p["ex"], p["wo"])


def _out_sample(o_a, ga, o_b, gb, x2, p):
    n, d_model = x2.shape
    tm = 256 if n % 256 == 0 else n
    row = lambda w: pl.BlockSpec((tm, w), lambda i: (i, 0))
    return pl.pallas_call(
        _out_sample_kernel,
        grid=(n // tm,),
        in_specs=[row(D_A), row(D_A), row(D_B), row(D_B), row(d_model), _const_spec((D_A + D_B, d_model))],
        out_specs=row(d_model),
        out_shape=jax.ShapeDtypeStruct((n, d_model), F32),
        compiler_params=pltpu.CompilerParams(dimension_semantics=("arbitrary",), vmem_limit_bytes=VMEM_LIMIT),
        name="out_sample",
    )(o_a, ga, o_b, gb, x2, p["wo"])


def _layer_params(norm_w, w_in, w_gate_up, b_gate, q_norm_w, k_norm_w, gla_norm_w, w_out):
    d_model = w_in.shape[0]
    split = np.cumsum((D_A, D_A, D_A, D_KB, D_KB, D_B, GATE_RANK, D_A, D_B))
    glr = w_in[:, split[5]:split[6]]
    w = jnp.concatenate([w_in[:, :split[5]], glr, jnp.zeros((d_model, LANES - GATE_RANK), w_in.dtype),
                         w_in[:, split[6]:]], axis=1).astype(BF16)
    head_of = np.arange(D_A) // HD_A
    bd = jnp.asarray(head_of[:MXU_DIM, None] == head_of[None, :MXU_DIM], BF16)
    ex = jnp.asarray(np.arange(LANES)[:, None] == head_of[None, :], BF16)
    bdx = jnp.asarray((np.arange(D_KB) // DK_B)[:, None] == (np.arange(D_B) // DV_B)[None, :], BF16)
    wgu = jnp.concatenate([w_gate_up, jnp.zeros((LANES - GATE_RANK, D_KB), w_gate_up.dtype)], axis=0).astype(BF16)
    return dict(nw=norm_w.reshape(1, d_model), w=w, bd=bd, ex=ex, bdx=bdx,
                wq=jnp.tile(q_norm_w, H_A).reshape(1, D_A), wk=jnp.tile(k_norm_w, H_A).reshape(1, D_A),
                wgu=wgu, bg=b_gate.reshape(1, D_KB), gw=gla_norm_w.reshape(1, DV_B), wo=w_out.astype(BF16))


def _prompt_layer(x, p, keep):
    bsz, seq, _ = x.shape
    r = _proj_prompt(x, p, keep)
    by_class = lambda a, dil: a.reshape(bsz, dil, seq // dil, D_A)
    att = [_band_attn(*(by_class(r[f"{n}{dil}"], dil) for n in "qkv")) for _, dil in DILATED]
    o_b, s_fin = _gla(r["qb"], r["kb"], r["vb"], r["la"], jnp.zeros((bsz, H_B * DK_B, DV_B), F32),
                      p["gw"], p["bdx"])
    y = _out_prompt(att, r["ga"], o_b, r["gb"], x, p)
    win = lambda a: a.reshape(bsz, H_A, HD_A, keep).transpose(0, 3, 1, 2)
    return y, win(r["kt"]), win(r["vt"]), s_fin.reshape(bsz, H_B, DK_B, DV_B)


def _sample_layer(x, p, k_cache, v_cache, s0):
    bsz, t_len, d_model = x.shape
    win = k_cache.shape[1]
    pos = PAST_LEN + jnp.tile(jnp.arange(t_len), bsz)
    r = _proj_sample(x.reshape(bsz * t_len, d_model), pos, p)
    tok = lambda a: a.reshape(bsz, t_len, a.shape[-1])
    by_pos = lambda a: a.transpose(0, 2, 3, 1).reshape(bsz, D_A, win)
    o_a = _sample_attn(tok(r["q"]), tok(r["k"]), tok(r["v"]), by_pos(k_cache), by_pos(v_cache))
    o_b, s_fin = _gla(tok(r["qb"]), tok(r["kb"]), tok(r["vb"]), tok(r["la"]),
                      s0.reshape(bsz, H_B * DK_B, DV_B), p["gw"], p["bdx"])
    y = _out_sample(o_a.reshape(bsz * t_len, D_A), r["ga"], o_b.reshape(bsz * t_len, D_B), r["gb"],
                    x.reshape(bsz * t_len, d_model), p)
    return (y.reshape(bsz, t_len, d_model), r["k"].reshape(bsz, t_len, H_A, HD_A),
            r["v"].reshape(bsz, t_len, H_A, HD_A), s_fin.reshape(bsz, H_B, DK_B, DV_B))


def kernel(x_prompt, x_sample, cache_k_win, cache_v_win, state_gla, norm_w, w_in, w_gate_up, b_gate,
           q_norm_w, k_norm_w, gla_norm_w, w_out):
    seq = x_prompt.shape[1]
    assert seq % (MAX_DIL * BAND) == 0, "prompt length must tile into residue-class bands"
    assert all(w // d == BAND and MAX_DIL % d == 0 for w, d in DILATED)
    keep = min(MAX_WINDOW, seq)
    hp, hs = x_prompt, x_sample
    outs = [[] for _ in range(6)]
    for layer in range(w_in.shape[0]):
        p = _layer_params(norm_w[layer], w_in[layer], w_gate_up[layer], b_gate[layer], q_norm_w[layer],
                          k_norm_w[layer], gla_norm_w[layer], w_out[layer])
        hp, kp, vp, st_p = _prompt_layer(hp, p, keep)
        hs, kn, vn, st_s = _sample_layer(hs, p, cache_k_win[layer], cache_v_win[layer], state_gla[layer])
        for lst, val in zip(outs, (kp, vp, st_p.astype(state_gla.dtype), kn, vn, st_s.astype(state_gla.dtype))):
            lst.append(val)
    return (hp, hs) + tuple(jnp.stack(o) for o in outs)
```

```python
import functools

import numpy as np
import jax
import jax.numpy as jnp
from jax import lax
from jax.experimental import pallas as pl
from jax.experimental.pallas import tpu as pltpu

F32 = jnp.float32
BF16 = jnp.bfloat16

H_A, HD_A = 8, 64
D_A = H_A * HD_A
H_B, DK_B, DV_B = 4, 64, 128
D_KB = H_B * DK_B
D_B = H_B * DV_B
GATE_RANK = 16
GATE_TAU = 16.0
DILATED = ((128, 1), (512, 4), (2048, 16))
MAX_DIL = 16
MAX_WINDOW = 2048
BAND = 128
ROPE_THETA = 10000.0
EPS = 1e-6
PAST_LEN = 8192
GLA_CHUNK = 128
GLA_STEP_CHUNKS = 4
GLA_SHORT_BATCH = 8
GLA_SUB = 8
LANES = 128
MXU_DIM = 256
VMEM_LIMIT = 56 * 1024 * 1024
PROJ_TOKENS = 512
ATTN_QUERIES = 1024

_SECTIONS = (("q", D_A), ("k", D_A), ("v", D_A), ("qb", D_KB), ("kb", D_KB), ("vb", D_B),
             ("glr", LANES), ("ga", D_A), ("gb", D_B))
_OFF = {}
_o = 0
for _n, _w in _SECTIONS:
    _OFF[_n] = (_o, _o + _w)
    _o += _w
D_PAD = _o


def _mod2(x, n):
    assert n & (n - 1) == 0
    return jnp.bitwise_and(x, n - 1)


def _div2(x, n):
    assert n & (n - 1) == 0
    return jnp.right_shift(x, n.bit_length() - 1)


def _dot(a, b):
    return jnp.dot(a, b, preferred_element_type=F32)


def _dot_nt(a, b):
    return lax.dot_general(a, b, (((1,), (1,)), ((), ())), preferred_element_type=F32)


def _dot_tn(a, b):
    return lax.dot_general(a, b, (((0,), (0,)), ((), ())), preferred_element_type=F32)


def _proj_tile(x, cos, sin, nw, w_ref, bd, wq, wk, wgu, bg, emit):
    ms = jnp.mean(x * x, axis=-1, keepdims=True)
    h = (x * lax.rsqrt(ms + EPS) * nw).astype(BF16)
    sec = lambda name: _dot(h, w_ref[:, _OFF[name][0]:_OFF[name][1]])
    lane = lax.broadcasted_iota(jnp.int32, (x.shape[0], LANES), 1)
    first_half = _mod2(lane, HD_A) < (HD_A // 2)

    def qk_norm_rope(zz, wn):
        sq = (zz * zz).astype(BF16)
        ss = jnp.concatenate([_dot(sq[:, c:c + MXU_DIM], bd) for c in range(0, D_A, MXU_DIM)], axis=1)
        y = zz * lax.rsqrt(ss * (1.0 / HD_A) + EPS) * wn
        outs = []
        for j in range(D_A // LANES):
            yj = y[:, j * LANES:(j + 1) * LANES]
            swapped = jnp.where(first_half, pltpu.roll(yj, LANES - HD_A // 2, 1),
                                pltpu.roll(yj, HD_A // 2, 1))
            outs.append(yj * cos + swapped * sin)
        return jnp.concatenate(outs, axis=1)

    emit("q", qk_norm_rope(sec("q"), wq) * (HD_A ** -0.5))
    emit("k", qk_norm_rope(sec("k"), wk))
    emit("v", sec("v"))
    emit("qb", sec("qb") * (DK_B ** -0.5))
    emit("kb", sec("kb"))
    emit("vb", sec("vb"))
    xg = _dot(sec("glr").astype(BF16), wgu) + bg
    emit("la", (jnp.minimum(xg, 0.0) - jnp.log1p(jnp.exp(-jnp.abs(xg)))) * (1.0 / GATE_TAU))
    for name in ("ga", "gb"):
        gate = sec(name)
        emit(name, gate / (1.0 + jnp.exp(-gate)))


def _proj_prompt_kernel(x_ref, cos_ref, sin_ref, nw_ref, w_ref, bd_ref, wq_ref, wk_ref, wgu_ref, bg_ref,
                        q1, k1, v1, q4, k4, v4, q16, k16, v16, kt, vt, qb, kb, vb, la, ga, gb, slabs):
    tm = x_ref.shape[0]
    plain = dict(qb=qb, kb=kb, vb=vb, la=la, ga=ga, gb=gb)
    attn = dict(q=(0, (q1, q4, q16), None), k=(1, (k1, k4, k16), kt), v=(2, (v1, v4, v16), vt))

    def emit(name, val):
        if name in plain:
            plain[name][...] = val.astype(plain[name].dtype)
            return
        si, refs, win_t = attn[name]
        slab = slabs.at[si]
        for j in range(D_A // LANES):
            slab[j] = val[:, j * LANES:(j + 1) * LANES]
        for (_, dil), ref in zip(DILATED, refs):
            if dil == 1:
                ref[...] = val.astype(BF16)
                continue
            for c in range(dil):
                for j in range(D_A // LANES):
                    ref[c, :, j * LANES:(j + 1) * LANES] = slab[j, pl.ds(c, tm // dil, stride=dil), :].astype(BF16)
        if win_t is not None:
            win_t[...] = val.T

    _proj_tile(x_ref[...], cos_ref[...], sin_ref[...], nw_ref[...], w_ref, bd_ref[...],
               wq_ref[...], wk_ref[...], wgu_ref[...], bg_ref[...], emit)


def _proj_sample_kernel(x_ref, cos_ref, sin_ref, nw_ref, w_ref, bd_ref, wq_ref, wk_ref, wgu_ref, bg_ref,
                        q, k, v, qb, kb, vb, la, ga, gb):
    refs = dict(q=q, k=k, v=v, qb=qb, kb=kb, vb=vb, la=la, ga=ga, gb=gb)

    def emit(name, val):
        refs[name][...] = val.astype(refs[name].dtype)

    _proj_tile(x_ref[...], cos_ref[...], sin_ref[...], nw_ref[...], w_ref, bd_ref[...],
               wq_ref[...], wk_ref[...], wgu_ref[...], bg_ref[...], emit)


def _rope_tables(pos):
    half = HD_A // 2
    inv_freq = ROPE_THETA ** (-jnp.arange(half, dtype=F32) / half)
    ang = pos.astype(F32)[:, None] * inv_freq[None, :]
    cos, sin = jnp.cos(ang), jnp.sin(ang)
    reps = LANES // HD_A
    cos_t = jnp.tile(jnp.concatenate([cos, cos], axis=1), (1, reps))
    sin_t = jnp.tile(jnp.concatenate([-sin, sin], axis=1), (1, reps))
    return cos_t, sin_t


def _const_spec(shape):
    return pl.BlockSpec(shape, lambda *_: (0,) * len(shape))


def _weight_args(p):
    return (p["nw"], p["w"], p["bd"], p["wq"], p["wk"], p["wgu"], p["bg"])


def _weight_specs(d_model):
    return [_const_spec((1, d_model)), _const_spec((d_model, D_PAD)), _const_spec((MXU_DIM, MXU_DIM)),
            _const_spec((1, D_A)), _const_spec((1, D_A)), _const_spec((LANES, D_KB)), _const_spec((1, D_KB))]


def _proj_prompt(x, p, keep):
    bsz, seq, d_model = x.shape
    tm = PROJ_TOKENS
    n_skip = (seq - keep) // tm
    cos_t, sin_t = _rope_tables(jnp.arange(seq))
    res = lambda dil: jax.ShapeDtypeStruct((bsz, dil, seq // dil, D_A), BF16)
    nat = lambda w, dt: jax.ShapeDtypeStruct((bsz, seq, w), dt)
    win_t = jax.ShapeDtypeStruct((bsz, D_A, keep), F32)
    res_spec = lambda dil: pl.BlockSpec((None, dil, tm // dil, D_A), lambda b, i: (b, 0, i, 0))
    nat_spec = lambda w: pl.BlockSpec((None, tm, w), lambda b, i: (b, i, 0))
    win_spec = pl.BlockSpec((None, D_A, tm), lambda b, i: (b, 0, jnp.maximum(i - n_skip, 0)))
    tab_spec = pl.BlockSpec((tm, LANES), lambda b, i: (i, 0))
    qkv_specs, qkv_shapes = [], []
    for _, dil in DILATED:
        qkv_specs += [nat_spec(D_A) if dil == 1 else res_spec(dil)] * 3
        qkv_shapes += [nat(D_A, BF16) if dil == 1 else res(dil)] * 3
    outs = pl.pallas_call(
        _proj_prompt_kernel,
        grid=(bsz, seq // tm),
        in_specs=[nat_spec(d_model), tab_spec, tab_spec] + _weight_specs(d_model),
        out_specs=qkv_specs + [win_spec] * 2
                  + [nat_spec(D_KB), nat_spec(D_KB), nat_spec(D_B), nat_spec(D_KB), nat_spec(D_A), nat_spec(D_B)],
        out_shape=qkv_shapes + [win_t] * 2
                  + [nat(D_KB, F32), nat(D_KB, F32), nat(D_B, F32), nat(D_KB, F32), nat(D_A, BF16), nat(D_B, BF16)],
        scratch_shapes=[pltpu.VMEM((3, D_A // LANES, tm, LANES), F32)],
        compiler_params=pltpu.CompilerParams(dimension_semantics=("arbitrary", "arbitrary"),
                                             vmem_limit_bytes=VMEM_LIMIT),
        name="proj_prompt",
    )(x, cos_t, sin_t, *_weight_args(p))
    names = [f"{n}{dil}" for _, dil in DILATED for n in "qkv"] + ["kt", "vt", "qb", "kb", "vb", "la", "ga", "gb"]
    return dict(zip(names, outs))


def _proj_sample(x2, pos, p):
    n, d_model = x2.shape
    tm = 256 if n % 256 == 0 else n
    cos_t, sin_t = _rope_tables(pos)
    row = lambda w: pl.BlockSpec((tm, w), lambda i: (i, 0))
    widths = (D_A, D_A, D_A, D_KB, D_KB, D_B, D_KB, D_A, D_B)
    outs = pl.pallas_call(
        _proj_sample_kernel,
        grid=(n // tm,),
        in_specs=[row(d_model), row(LANES), row(LANES)] + _weight_specs(d_model),
        out_specs=[row(w) for w in widths],
        out_shape=[jax.ShapeDtypeStruct((n, w), BF16 if i == 0 else F32) for i, w in enumerate(widths)],
        compiler_params=pltpu.CompilerParams(dimension_semantics=("arbitrary",), vmem_limit_bytes=VMEM_LIMIT),
        name="proj_sample",
    )(x2, cos_t, sin_t, *_weight_args(p))
    return dict(zip(("q", "k", "v", "qb", "kb", "vb", "la", "ga", "gb"), outs))


def _band_attn_kernel(bias_ref, q_ref, kp_ref, kc_ref, vp_ref, vc_ref, o_ref, lse_ref, kbuf, vbuf):
    tq = BAND
    kbuf[:, 0:tq] = kp_ref[...]
    kbuf[:, tq:] = kc_ref[...]
    vbuf[:, 0:tq] = vp_ref[...]
    vbuf[:, tq:] = vc_ref[...]
    table0 = jnp.where(pl.program_id(2) == 0, 0, 1)
    lane = lax.broadcasted_iota(jnp.int32, (tq, LANES), 1)
    low = lane < HD_A
    pairs = [slice(hp * LANES, (hp + 1) * LANES) for hp in range(D_A // LANES)]
    n_cls, n_blocks = q_ref.shape[0], q_ref.shape[1] // tq

    def score_phase(c, j):
        scores = []
        for cs in pairs:
            qp = q_ref[c, j * tq:(j + 1) * tq, cs]
            zero = jnp.zeros_like(qp)
            qs = jnp.concatenate([jnp.where(low, qp, zero), jnp.where(low, zero, qp)], axis=0)
            bias = bias_ref[table0] if j == 0 else bias_ref[1]
            scores.append(_dot_nt(qs, kbuf[c, j * tq:(j + 2) * tq, cs]) + bias)
        return scores

    def softmax_phase(scores):
        probs = []
        for s in scores:
            m = jnp.max(s, axis=-1, keepdims=True)
            pr = jnp.exp(s - m)
            probs.append((pr.astype(BF16), m, jnp.sum(pr, axis=-1, keepdims=True)))
        return probs

    def value_phase(c, j, probs):
        m_all = jnp.zeros((tq, LANES), F32)
        den_all = jnp.ones((tq, LANES), F32)
        for hp, (cs, (pr, m, den)) in enumerate(zip(pairs, probs)):
            pv = _dot(pr, vbuf[c, j * tq:(j + 2) * tq, cs]) / den
            o_ref[c, j * tq:(j + 1) * tq, cs] = jnp.where(low, pv[:tq], pv[tq:]).astype(BF16)
            for e in range(2):
                here = lane == 2 * hp + e
                m_all = jnp.where(here, m[e * tq:(e + 1) * tq], m_all)
                den_all = jnp.where(here, den[e * tq:(e + 1) * tq], den_all)
        lse_ref[c, j * tq:(j + 1) * tq, :] = m_all + jnp.log(den_all)

    for c in range(n_cls):
        for j in range(n_blocks):
            value_phase(c, j, softmax_phase(score_phase(c, j)))


def _band_bias():
    i = np.arange(2 * BAND)[:, None] % BAND
    j = np.arange(2 * BAND)[None, :]
    ok = (i + BAND - j >= 0) & (i + BAND - j <= BAND)
    tables = np.stack([ok & (j >= BAND), ok])
    return jnp.asarray(np.where(tables, 0.0, -np.inf), F32)


def _band_attn(q, k, v):
    bsz, ncls, length, _ = q.shape
    tq = min(ATTN_QUERIES, length)
    nc = min(ncls, ATTN_QUERIES // tq)
    ratio = tq // BAND
    cur = lambda b, c, i: (b, c, i, 0)
    prev = lambda b, c, i: (b, c, jnp.maximum(i * ratio - 1, 0), 0)
    cur_spec = lambda w: pl.BlockSpec((None, nc, tq, w), cur)
    prev_spec = pl.BlockSpec((None, nc, BAND, D_A), prev)
    return pl.pallas_call(
        _band_attn_kernel,
        grid=(bsz, ncls // nc, length // tq),
        in_specs=[_const_spec((2, 2 * BAND, 2 * BAND)), cur_spec(D_A), prev_spec, cur_spec(D_A),
                  prev_spec, cur_spec(D_A)],
        out_specs=[cur_spec(D_A), cur_spec(LANES)],
        out_shape=[jax.ShapeDtypeStruct(q.shape, BF16), jax.ShapeDtypeStruct((bsz, ncls, length, LANES), F32)],
        scratch_shapes=[pltpu.VMEM((nc, BAND + tq, D_A), BF16)] * 2,
        compiler_params=pltpu.CompilerParams(dimension_semantics=("arbitrary",) * 3, vmem_limit_bytes=VMEM_LIMIT),
        name=f"band_attn_c{ncls}",
    )(_band_bias(), q, k, k, v, v)


def _sample_attn_kernel(q_ref, kn_ref, vn_ref, kc_ref, vc_ref, wc_ref, wn_ref, o_ref):
    t = q_ref.shape[0]
    q = q_ref[...]
    head = _div2(lax.broadcasted_iota(jnp.int32, (t, D_A), 1), HD_A)
    zero = jnp.zeros_like(q)
    qbd = jnp.concatenate([jnp.where(head == h, q, zero) for h in range(H_A)], axis=0)
    wc, wn = wc_ref[...], wn_ref[...]
    sc = jnp.where(wc > 0, _dot(qbd, kc_ref[...].astype(BF16)), -jnp.inf)
    sn = jnp.where(wn > 0, _dot_nt(qbd, kn_ref[...].astype(BF16)), -jnp.inf)
    m = jnp.maximum(jnp.max(sc, axis=-1, keepdims=True), jnp.max(sn, axis=-1, keepdims=True))
    pc = wc * jnp.exp(sc - m)
    pn = wn * jnp.exp(sn - m)
    den = jnp.sum(pc, axis=-1, keepdims=True) + jnp.sum(pn, axis=-1, keepdims=True)
    o = (_dot_nt(pc.astype(BF16), vc_ref[...].astype(BF16)) + _dot(pn.astype(BF16), vn_ref[...].astype(BF16))) / den
    acc = jnp.zeros((t, D_A), F32)
    for h in range(H_A):
        acc = jnp.where(head == h, o[h * t:(h + 1) * t], acc)
    o_ref[...] = acc


def _sample_key_weights(win, t_len):
    tt = np.arange(t_len)[:, None]
    rel_c = win + tt - np.arange(win)[None, :]
    rel_n = tt - np.arange(t_len)[None, :]
    def count(rel):
        c = np.zeros(rel.shape, np.float32)
        for window, dil in DILATED:
            c += (rel >= 0) & (rel % dil == 0) & (rel // dil <= window // dil)
        return c
    return np.tile(count(rel_c), (H_A, 1)), np.tile(count(rel_n), (H_A, 1))


def _sample_attn(q, k_new, v_new, k_cache, v_cache):
    bsz, t_len, _ = q.shape
    win = k_cache.shape[2]
    wc, wn = _sample_key_weights(win, t_len)
    per_b = lambda rows: pl.BlockSpec((None, rows, D_A), lambda b: (b, 0, 0))
    cache = pl.BlockSpec((None, D_A, win), lambda b: (b, 0, 0))
    return pl.pallas_call(
        _sample_attn_kernel,
        grid=(bsz,),
        in_specs=[per_b(t_len), per_b(t_len), per_b(t_len), cache, cache,
                  _const_spec(wc.shape), _const_spec(wn.shape)],
        out_specs=per_b(t_len),
        out_shape=jax.ShapeDtypeStruct((bsz, t_len, D_A), F32),
        compiler_params=pltpu.CompilerParams(dimension_semantics=("arbitrary",), vmem_limit_bytes=VMEM_LIMIT),
        name="sample_attn",
    )(q, k_new, v_new, k_cache, v_cache, jnp.asarray(wc), jnp.asarray(wn))


def _stack_heads(x):
    low = lax.broadcasted_iota(jnp.int32, x.shape, 1) < DK_B
    zero = jnp.zeros_like(x)
    return jnp.concatenate([jnp.where(low, x, zero), jnp.where(low, zero, x)], axis=0)


def _gla_chunk(q, k, v, g, state, slabs, gw, bdx):
    c = q.shape[0]
    row = lax.broadcasted_iota(jnp.int32, (c, c), 0)
    col = lax.broadcasted_iota(jnp.int32, (c, c), 1)
    tri = (row >= col).astype(BF16)
    g_hi = g.astype(BF16)
    g_lo = (g - g_hi.astype(F32)).astype(BF16)
    b = _dot(tri, g_hi) + _dot(tri, g_lo)
    b_last = b[c - 1:c, :]
    q_in = q * jnp.exp(b)
    k_dec = k * jnp.exp(b_last - b)
    trow2 = _mod2(lax.broadcasted_iota(jnp.int32, (2 * c, c), 0), c)
    col2 = lax.broadcasted_iota(jnp.int32, (2 * c, c), 1)

    levels = []
    p_blk = c
    while p_blk > GLA_SUB:
        half = p_blk // 2
        pieces = [jnp.broadcast_to(b[s + half - 1:s + half, :], (p_blk, D_KB)) for s in range(0, c, p_blk)]
        bm = pieces[0] if len(pieces) == 1 else jnp.concatenate(pieces, axis=0)
        q_l = q * jnp.exp(jnp.minimum(b - bm, 0.0))
        k_l = k * jnp.exp(jnp.minimum(bm - b, 0.0))
        region = ((_div2(trow2, p_blk) == _div2(col2, p_blk)) & (_mod2(trow2, p_blk) >= half)
                  & (_mod2(col2, p_blk) < half))
        levels.append((q_l, k_l, region))
        p_blk = half

    n_ks = D_KB // LANES
    for j in range(n_ks):
        slabs[j] = k[:, j * LANES:(j + 1) * LANES]
        slabs[n_ks + j] = b[:, j * LANES:(j + 1) * LANES]

    def block_row(first_slab, s):
        return jnp.concatenate(
            [jnp.concatenate([jnp.broadcast_to(slabs[first_slab + j, g + s:g + s + 1, :], (GLA_SUB, LANES))
                              for g in range(0, c, GLA_SUB)], axis=0) for j in range(n_ks)], axis=1)

    sub = _mod2(lax.broadcasted_iota(jnp.int32, (c, D_KB), 0), GLA_SUB)
    terms = []
    for s in range(GLA_SUB):
        b_s = block_row(n_ks, s)
        decay = jnp.exp(b - b_s if s == 0 else jnp.where(sub >= s, b - b_s, -jnp.inf))
        terms.append(q * block_row(0, s) * decay)
    sc_d = _dot(jnp.concatenate(terms, axis=0).astype(BF16), bdx)
    key_in_block = col2 - (trow2 - _mod2(trow2, GLA_SUB))

    pair_lanes = [slice(p * LANES, (p + 1) * LANES) for p in range(H_B // 2)]
    o_pairs = [_dot_nt(_stack_heads(q_in[:, ks]).astype(BF16), state[p].astype(BF16))
               for p, ks in enumerate(pair_lanes)]
    level_scores = [[_dot_nt(_stack_heads(q_l[:, ks]).astype(BF16), k_l[:, ks].astype(BF16))
                     for q_l, k_l, _ in levels] for ks in pair_lanes]
    upds = [[_dot_tn(v[:, (2 * p + e) * DV_B:(2 * p + e + 1) * DV_B].astype(BF16), k_dec[:, ks].astype(BF16))
             for e in range(2)] for p, ks in enumerate(pair_lanes)]
    outs, new_state = [], []
    for p, ks in enumerate(pair_lanes):
        sc = jnp.zeros((2 * c, c), F32)
        for s_l, (_, _, region) in zip(level_scores[p], levels):
            sc = jnp.where(region, s_l, sc)
        for s in range(GLA_SUB):
            d_s = jnp.concatenate([sc_d[s * c:(s + 1) * c, (2 * p + e) * DV_B:(2 * p + e) * DV_B + c]
                                   for e in range(2)], axis=0)
            sc = jnp.where(key_in_block == s, d_s, sc)
        sc = sc.astype(BF16)
        for e in range(2):
            hidx = 2 * p + e
            v_h = v[:, hidx * DV_B:(hidx + 1) * DV_B].astype(BF16)
            o_h = o_pairs[p][e * c:(e + 1) * c] + _dot(sc[e * c:(e + 1) * c], v_h)
            ms = jnp.mean(o_h * o_h, axis=-1, keepdims=True)
            outs.append(o_h * lax.rsqrt(ms + EPS) * gw)
        low = lax.broadcasted_iota(jnp.int32, (DV_B, LANES), 1) < DK_B
        new_state.append(state[p] * jnp.exp(b_last[:, ks]) + jnp.where(low, upds[p][0], upds[p][1]))
    return jnp.concatenate(outs, axis=1), new_state


def _gla_kernel(q_ref, k_ref, v_ref, g_ref, s0_ref, gw_ref, bdx_ref, o_ref, sfin_ref, st_ref, slabs, *,
                chunk, n_chunks):
    step = pl.program_id(1)
    n_seq = q_ref.shape[0]

    @pl.when(step == 0)
    def _():
        for i in range(n_seq):
            for p in range(H_B // 2):
                st_ref[i, p] = s0_ref[i, p * LANES:(p + 1) * LANES, :].T

    for i in range(n_seq):
        state = [st_ref[i, p] for p in range(H_B // 2)]
        for ci in range(n_chunks):
            rs = slice(ci * chunk, (ci + 1) * chunk)
            o, state = _gla_chunk(q_ref[i, rs, :], k_ref[i, rs, :], v_ref[i, rs, :], g_ref[i, rs, :], state,
                                  slabs.at[i * n_chunks + ci], gw_ref[...], bdx_ref[...])
            o_ref[i, rs, :] = o.astype(o_ref.dtype)
        for p in range(H_B // 2):
            st_ref[i, p] = state[p]

    @pl.when(step == pl.num_programs(1) - 1)
    def _():
        for i in range(n_seq):
            for p in range(H_B // 2):
                sfin_ref[i, p * LANES:(p + 1) * LANES, :] = st_ref[i, p].T


def _gla(qb, kb, vb, la, s0, gw, bdx):
    bsz, length, _ = qb.shape
    chunk = int(np.gcd(length, GLA_CHUNK))
    n_chunks = GLA_STEP_CHUNKS if length % (GLA_STEP_CHUNKS * chunk) == 0 else 1
    tl = chunk * n_chunks
    n_seq = GLA_SHORT_BATCH if length == tl and bsz % GLA_SHORT_BATCH == 0 else 1
    tok = lambda w: pl.BlockSpec((n_seq, tl, w), lambda b, i: (b, i, 0))
    st = pl.BlockSpec((n_seq, H_B * DK_B, DV_B), lambda b, i: (b, 0, 0))
    return pl.pallas_call(
        functools.partial(_gla_kernel, chunk=chunk, n_chunks=n_chunks),
        grid=(bsz // n_seq, length // tl),
        in_specs=[tok(D_KB), tok(D_KB), tok(D_B), tok(D_KB), st, _const_spec((1, DV_B)), _const_spec((D_KB, D_B))],
        out_specs=[tok(D_B), st],
        out_shape=[jax.ShapeDtypeStruct((bsz, length, D_B), BF16 if tl % 16 == 0 else F32),
                   jax.ShapeDtypeStruct((bsz, H_B * DK_B, DV_B), F32)],
        scratch_shapes=[pltpu.VMEM((n_seq, H_B // 2, DV_B, LANES), F32),
                        pltpu.VMEM((n_seq * n_chunks, 2 * D_KB // LANES, chunk, LANES), F32)],
        compiler_params=pltpu.CompilerParams(dimension_semantics=("arbitrary", "arbitrary"),
                                             vmem_limit_bytes=VMEM_LIMIT),
        name="gla",
    )(qb, kb, vb, la, s0, gw, bdx)


def _mix_and_project(o_a, ga, o_b, gb, x, wo):
    mixed = jnp.concatenate([o_a * ga, o_b * gb], axis=1).astype(BF16)
    return x + _dot(mixed, wo)


def _out_prompt_kernel(*refs):
    nb = len(DILATED)
    att = refs[:2 * nb]
    ga, ob, gb, x_ref, ex_ref, wo_ref, y_ref = refs[2 * nb:2 * nb + 7]
    scratch = list(refs[2 * nb + 7:])
    tm = x_ref.shape[0]
    outs, lses = [], []
    for bi, (_, dil) in enumerate(DILATED):
        o_ref, l_ref = att[2 * bi], att[2 * bi + 1]
        if dil == 1:
            outs.append(o_ref[0].astype(F32))
            lses.append(l_ref[0])
            continue
        oslab, lslab = scratch.pop(0), scratch.pop(0)
        for c in range(dil):
            rows = pl.ds(c, tm // dil, stride=dil)
            for j in range(D_A // LANES):
                oslab[j, rows, :] = o_ref[c, :, j * LANES:(j + 1) * LANES].astype(F32)
            lslab[rows, :] = l_ref[c]
        outs.append(jnp.concatenate([oslab[j] for j in range(D_A // LANES)], axis=1))
        lses.append(lslab[...])
    m = functools.reduce(jnp.maximum, lses)
    es = [jnp.exp(l - m) for l in lses]
    inv = 1.0 / functools.reduce(lambda a, b: a + b, es)
    spread = lambda a: _dot((a * inv).astype(BF16), ex_ref[...])
    o_a = functools.reduce(lambda a, b: a + b, [spread(e) * o for e, o in zip(es, outs)])
    y_ref[...] = _mix_and_project(o_a, ga[...], ob[...], gb[...], x_ref[...], wo_ref[...])


def _out_sample_kernel(oa, ga, ob, gb, x_ref, wo_ref, y_ref):
    y_ref[...] = _mix_and_project(oa[...], ga[...], ob[...], gb[...], x_ref[...], wo_ref[...])


def _out_prompt(att, ga, o_b, gb, x, p):
    bsz, seq, d_model = x.shape
    tm = PROJ_TOKENS
    nat_spec = lambda w: pl.BlockSpec((None, tm, w), lambda b, i: (b, i, 0))
    cls_spec = lambda dil, w: pl.BlockSpec((None, dil, tm // dil, w), lambda b, i: (b, 0, i, 0))
    att_specs, att_args, scratch = [], [], []
    for (_, dil), (o, lse) in zip(DILATED, att):
        att_specs += [cls_spec(dil, D_A), cls_spec(dil, LANES)]
        att_args += [o, lse]
        if dil > 1:
            scratch += [pltpu.VMEM((D_A // LANES, tm, LANES), F32), pltpu.VMEM((tm, LANES), F32)]
    return pl.pallas_call(
        _out_prompt_kernel,
        grid=(bsz, seq // tm),
        in_specs=att_specs + [nat_spec(D_A), nat_spec(D_B), nat_spec(D_B), nat_spec(d_model),
                              _const_spec((LANES, D_A)), _const_spec((D_A + D_B, d_model))],
        out_specs=nat_spec(d_model),
        out_shape=jax.ShapeDtypeStruct((bsz, seq, d_model), F32),
        scratch_shapes=scratch,
        compiler_params=pltpu.CompilerParams(dimension_semantics=("arbitrary", "arbitrary"),
                                             vmem_limit_bytes=VMEM_LIMIT),
        name="out_prompt",
    )(*att_args, ga, o_b, gb, x, p["ex"], p["wo"])


def _out_sample(o_a, ga, o_b, gb, x2, p):
    n, d_model = x2.shape
    tm = 256 if n % 256 == 0 else n
    row = lambda w: pl.BlockSpec((tm, w), lambda i: (i, 0))
    return pl.pallas_call(
        _out_sample_kernel,
        grid=(n // tm,),
        in_specs=[row(D_A), row(D_A), row(D_B), row(D_B), row(d_model), _const_spec((D_A + D_B, d_model))],
        out_specs=row(d_model),
        out_shape=jax.ShapeDtypeStruct((n, d_model), F32),
        compiler_params=pltpu.CompilerParams(dimension_semantics=("arbitrary",), vmem_limit_bytes=VMEM_LIMIT),
        name="out_sample",
    )(o_a, ga, o_b, gb, x2, p["wo"])


def _layer_params(norm_w, w_in, w_gate_up, b_gate, q_norm_w, k_norm_w, gla_norm_w, w_out):
    d_model = w_in.shape[0]
    split = np.cumsum((D_A, D_A, D_A, D_KB, D_KB, D_B, GATE_RANK, D_A, D_B))
    glr = w_in[:, split[5]:split[6]]
    w = jnp.concatenate([w_in[:, :split[5]], glr, jnp.zeros((d_model, LANES - GATE_RANK), w_in.dtype),
                         w_in[:, split[6]:]], axis=1).astype(BF16)
    head_of = np.arange(D_A) // HD_A
    bd = jnp.asarray(head_of[:MXU_DIM, None] == head_of[None, :MXU_DIM], BF16)
    ex = jnp.asarray(np.arange(LANES)[:, None] == head_of[None, :], BF16)
    bdx = jnp.asarray((np.arange(D_KB) // DK_B)[:, None] == (np.arange(D_B) // DV_B)[None, :], BF16)
    wgu = jnp.concatenate([w_gate_up, jnp.zeros((LANES - GATE_RANK, D_KB), w_gate_up.dtype)], axis=0).astype(BF16)
    return dict(nw=norm_w.reshape(1, d_model), w=w, bd=bd, ex=ex, bdx=bdx,
                wq=jnp.tile(q_norm_w, H_A).reshape(1, D_A), wk=jnp.tile(k_norm_w, H_A).reshape(1, D_A),
                wgu=wgu, bg=b_gate.reshape(1, D_KB), gw=gla_norm_w.reshape(1, DV_B), wo=w_out.astype(BF16))


def _prompt_layer(x, p, keep):
    bsz, seq, _ = x.shape
    r = _proj_prompt(x, p, keep)
    by_class = lambda a, dil: a.reshape(bsz, dil, seq // dil, D_A)
    att = [_band_attn(*(by_class(r[f"{n}{dil}"], dil) for n in "qkv")) for _, dil in DILATED]
    o_b, s_fin = _gla(r["qb"], r["kb"], r["vb"], r["la"], jnp.zeros((bsz, H_B * DK_B, DV_B), F32),
                      p["gw"], p["bdx"])
    y = _out_prompt(att, r["ga"], o_b, r["gb"], x, p)
    win = lambda a: a.reshape(bsz, H_A, HD_A, keep).transpose(0, 3, 1, 2)
    return y, win(r["kt"]), win(r["vt"]), s_fin.reshape(bsz, H_B, DK_B, DV_B)


def _sample_layer(x, p, k_cache, v_cache, s0):
    bsz, t_len, d_model = x.shape
    win = k_cache.shape[1]
    pos = PAST_LEN + jnp.tile(jnp.arange(t_len), bsz)
    r = _proj_sample(x.reshape(bsz * t_len, d_model), pos, p)
    tok = lambda a: a.reshape(bsz, t_len, a.shape[-1])
    by_pos = lambda a: a.transpose(0, 2, 3, 1).reshape(bsz, D_A, win)
    o_a = _sample_attn(tok(r["q"]), tok(r["k"]), tok(r["v"]), by_pos(k_cache), by_pos(v_cache))
    o_b, s_fin = _gla(tok(r["qb"]), tok(r["kb"]), tok(r["vb"]), tok(r["la"]),
                      s0.reshape(bsz, H_B * DK_B, DV_B), p["gw"], p["bdx"])
    y = _out_sample(o_a.reshape(bsz * t_len, D_A), r["ga"], o_b.reshape(bsz * t_len, D_B), r["gb"],
                    x.reshape(bsz * t_len, d_model), p)
    return (y.reshape(bsz, t_len, d_model), r["k"].reshape(bsz, t_len, H_A, HD_A),
            r["v"].reshape(bsz, t_len, H_A, HD_A), s_fin.reshape(bsz, H_B, DK_B, DV_B))


def kernel(x_prompt, x_sample, cache_k_win, cache_v_win, state_gla, norm_w, w_in, w_gate_up, b_gate,
           q_norm_w, k_norm_w, gla_norm_w, w_out):
    seq = x_prompt.shape[1]
    assert seq % (MAX_DIL * BAND) == 0, "prompt length must tile into residue-class bands"
    assert all(w // d == BAND and MAX_DIL % d == 0 for w, d in DILATED)
    keep = min(MAX_WINDOW, seq)
    hp, hs = x_prompt, x_sample
    outs = [[] for _ in range(6)]
    for layer in range(w_in.shape[0]):
        p = _layer_params(norm_w[layer], w_in[layer], w_gate_up[layer], b_gate[layer], q_norm_w[layer],
                          k_norm_w[layer], gla_norm_w[layer], w_out[layer])
        hp, kp, vp, st_p = _prompt_layer(hp, p, keep)
        hs, kn, vn, st_s = _sample_layer(hs, p, cache_k_win[layer], cache_v_win[layer], state_gla[layer])
        for lst, val in zip(outs, (kp, vp, st_p.astype(state_gla.dtype), kn, vn, st_s.astype(state_gla.dtype))):
            lst.append(val)
    return (hp, hs) + tuple(jnp.stack(o) for o in outs)
```

```python
import functools

import numpy as np
import jax
import jax.numpy as jnp
from jax import lax
from jax.experimental import pallas as pl
from jax.experimental.pallas import tpu as pltpu

F32 = jnp.float32
BF16 = jnp.bfloat16

H_A, HD_A = 8, 64
D_A = H_A * HD_A
H_B, DK_B, DV_B = 4, 64, 128
D_KB = H_B * DK_B
D_B = H_B * DV_B
GATE_RANK = 16
GATE_TAU = 16.0
DILATED = ((128, 1), (512, 4), (2048, 16))
MAX_DIL = 16
MAX_WINDOW = 2048
BAND = 128
ROPE_THETA = 10000.0
EPS = 1e-6
PAST_LEN = 8192
GLA_CHUNK = 128
GLA_STEP_CHUNKS = 4
GLA_SHORT_BATCH = 8
GLA_SUB = 8
LANES = 128
MXU_DIM = 256
VMEM_LIMIT = 56 * 1024 * 1024
PROJ_TOKENS = 512
ATTN_QUERIES = 1024

_SECTIONS = (("q", D_A), ("k", D_A), ("v", D_A), ("qb", D_KB), ("kb", D_KB), ("vb", D_B),
             ("glr", LANES), ("ga", D_A), ("gb", D_B))
_OFF = {}
_o = 0
for _n, _w in _SECTIONS:
    _OFF[_n] = (_o, _o + _w)
    _o += _w
D_PAD = _o


def _mod2(x, n):
    assert n & (n - 1) == 0
    return jnp.bitwise_and(x, n - 1)


def _div2(x, n):
    assert n & (n - 1) == 0
    return jnp.right_shift(x, n.bit_length() - 1)


def _dot(a, b):
    return jnp.dot(a, b, preferred_element_type=F32)


def _dot_nt(a, b):
    return lax.dot_general(a, b, (((1,), (1,)), ((), ())), preferred_element_type=F32)


def _dot_tn(a, b):
    return lax.dot_general(a, b, (((0,), (0,)), ((), ())), preferred_element_type=F32)


def _proj_tile(x, cos, sin, nw, w_ref, bd, wq, wk, wgu, bg, emit):
    ms = jnp.mean(x * x, axis=-1, keepdims=True)
    h = (x * lax.rsqrt(ms + EPS) * nw).astype(BF16)
    sec = lambda name: _dot(h, w_ref[:, _OFF[name][0]:_OFF[name][1]])
    lane = lax.broadcasted_iota(jnp.int32, (x.shape[0], LANES), 1)
    first_half = _mod2(lane, HD_A) < (HD_A // 2)

    def qk_norm_rope(zz, wn):
        sq = (zz * zz).astype(BF16)
        ss = jnp.concatenate([_dot(sq[:, c:c + MXU_DIM], bd) for c in range(0, D_A, MXU_DIM)], axis=1)
        y = zz * lax.rsqrt(ss * (1.0 / HD_A) + EPS) * wn
        outs = []
        for j in range(D_A // LANES):
            yj = y[:, j * LANES:(j + 1) * LANES]
            swapped = jnp.where(first_half, pltpu.roll(yj, LANES - HD_A // 2, 1),
                                pltpu.roll(yj, HD_A // 2, 1))
            outs.append(yj * cos + swapped * sin)
        return jnp.concatenate(outs, axis=1)

    emit("q", qk_norm_rope(sec("q"), wq) * (HD_A ** -0.5))
    emit("k", qk_norm_rope(sec("k"), wk))
    emit("v", sec("v"))
    emit("qb", sec("qb") * (DK_B ** -0.5))
    emit("kb", sec("kb"))
    emit("vb", sec("vb"))
    xg = _dot(sec("glr").astype(BF16), wgu) + bg
    emit("la", (jnp.minimum(xg, 0.0) - jnp.log1p(jnp.exp(-jnp.abs(xg)))) * (1.0 / GATE_TAU))
    for name in ("ga", "gb"):
        gate = sec(name)
        emit(name, gate / (1.0 + jnp.exp(-gate)))


def _proj_prompt_kernel(x_ref, cos_ref, sin_ref, nw_ref, w_ref, bd_ref, wq_ref, wk_ref, wgu_ref, bg_ref,
                        q1, k1, v1, q4, k4, v4, q16, k16, v16, kt, vt, qb, kb, vb, la, ga, gb, slabs):
    tm = x_ref.shape[0]
    plain = dict(qb=qb, kb=kb, vb=vb, la=la, ga=ga, gb=gb)
    attn = dict(q=(0, (q1, q4, q16), None), k=(1, (k1, k4, k16), kt), v=(2, (v1, v4, v16), vt))

    def emit(name, val):
        if name in plain:
            plain[name][...] = val.astype(plain[name].dtype)
            return
        si, refs, win_t = attn[name]
        slab = slabs.at[si]
        for j in range(D_A // LANES):
            slab[j] = val[:, j * LANES:(j + 1) * LANES]
        for (_, dil), ref in zip(DILATED, refs):
            if dil == 1:
                ref[...] = val.astype(BF16)
                continue
            for c in range(dil):
                for j in range(D_A // LANES):
                    ref[c, :, j * LANES:(j + 1) * LANES] = slab[j, pl.ds(c, tm // dil, stride=dil), :].astype(BF16)
        if win_t is not None:
            win_t[...] = val.T

    _proj_tile(x_ref[...], cos_ref[...], sin_ref[...], nw_ref[...], w_ref, bd_ref[...],
               wq_ref[...], wk_ref[...], wgu_ref[...], bg_ref[...], emit)


def _proj_sample_kernel(x_ref, cos_ref, sin_ref, nw_ref, w_ref, bd_ref, wq_ref, wk_ref, wgu_ref, bg_ref,
                        q, k, v, qb, kb, vb, la, ga, gb):
    refs = dict(q=q, k=k, v=v, qb=qb, kb=kb, vb=vb, la=la, ga=ga, gb=gb)

    def emit(name, val):
        refs[name][...] = val.astype(refs[name].dtype)

    _proj_tile(x_ref[...], cos_ref[...], sin_ref[...], nw_ref[...], w_ref, bd_ref[...],
               wq_ref[...], wk_ref[...], wgu_ref[...], bg_ref[...], emit)


def _rope_tables(pos):
    half = HD_A // 2
    inv_freq = ROPE_THETA ** (-jnp.arange(half, dtype=F32) / half)
    ang = pos.astype(F32)[:, None] * inv_freq[None, :]
    cos, sin = jnp.cos(ang), jnp.sin(ang)
    reps = LANES // HD_A
    cos_t = jnp.tile(jnp.concatenate([cos, cos], axis=1), (1, reps))
    sin_t = jnp.tile(jnp.concatenate([-sin, sin], axis=1), (1, reps))
    return cos_t, sin_t


def _const_spec(shape):
    return pl.BlockSpec(shape, lambda *_: (0,) * len(shape))


def _weight_args(p):
    return (p["nw"], p["w"], p["bd"], p["wq"], p["wk"], p["wgu"], p["bg"])


def _weight_specs(d_model):
    return [_const_spec((1, d_model)), _const_spec((d_model, D_PAD)), _const_spec((MXU_DIM, MXU_DIM)),
            _const_spec((1, D_A)), _const_spec((1, D_A)), _const_spec((LANES, D_KB)), _const_spec((1, D_KB))]


def _proj_prompt(x, p, keep):
    bsz, seq, d_model = x.shape
    tm = PROJ_TOKENS
    n_skip = (seq - keep) // tm
    cos_t, sin_t = _rope_tables(jnp.arange(seq))
    res = lambda dil: jax.ShapeDtypeStruct((bsz, dil, seq // dil, D_A), BF16)
    nat = lambda w, dt: jax.ShapeDtypeStruct((bsz, seq, w), dt)
    win_t = jax.ShapeDtypeStruct((bsz, D_A, keep), F32)
    res_spec = lambda dil: pl.BlockSpec((None, dil, tm // dil, D_A), lambda b, i: (b, 0, i, 0))
    nat_spec = lambda w: pl.BlockSpec((None, tm, w), lambda b, i: (b, i, 0))
    win_spec = pl.BlockSpec((None, D_A, tm), lambda b, i: (b, 0, jnp.maximum(i - n_skip, 0)))
    tab_spec = pl.BlockSpec((tm, LANES), lambda b, i: (i, 0))
    qkv_specs, qkv_shapes = [], []
    for _, dil in DILATED:
        qkv_specs += [nat_spec(D_A) if dil == 1 else res_spec(dil)] * 3
        qkv_shapes += [nat(D_A, BF16) if dil == 1 else res(dil)] * 3
    outs = pl.pallas_call(
        _proj_prompt_kernel,
        grid=(bsz, seq // tm),
        in_specs=[nat_spec(d_model), tab_spec, tab_spec] + _weight_specs(d_model),
        out_specs=qkv_specs + [win_spec] * 2
                  + [nat_spec(D_KB), nat_spec(D_KB), nat_spec(D_B), nat_spec(D_KB), nat_spec(D_A), nat_spec(D_B)],
        out_shape=qkv_shapes + [win_t] * 2
                  + [nat(D_KB, F32), nat(D_KB, F32), nat(D_B, F32), nat(D_KB, F32), nat(D_A, BF16), nat(D_B, BF16)],
        scratch_shapes=[pltpu.VMEM((3, D_A // LANES, tm, LANES), F32)],
        compiler_params=pltpu.CompilerParams(dimension_semantics=("arbitrary", "arbitrary"),
                                             vmem_limit_bytes=VMEM_LIMIT),
        name="proj_prompt",
    )(x, cos_t, sin_t, *_weight_args(p))
    names = [f"{n}{dil}" for _, dil in DILATED for n in "qkv"] + ["kt", "vt", "qb", "kb", "vb", "la", "ga", "gb"]
    return dict(zip(names, outs))


def _proj_sample(x2, pos, p):
    n, d_model = x2.shape
    tm = 256 if n % 256 == 0 else n
    cos_t, sin_t = _rope_tables(pos)
    row = lambda w: pl.BlockSpec((tm, w), lambda i: (i, 0))
    widths = (D_A, D_A, D_A, D_KB, D_KB, D_B, D_KB, D_A, D_B)
    outs = pl.pallas_call(
        _proj_sample_kernel,
        grid=(n // tm,),
        in_specs=[row(d_model), row(LANES), row(LANES)] + _weight_specs(d_model),
        out_specs=[row(w) for w in widths],
        out_shape=[jax.ShapeDtypeStruct((n, w), BF16 if i == 0 else F32) for i, w in enumerate(widths)],
        compiler_params=pltpu.CompilerParams(dimension_semantics=("arbitrary",), vmem_limit_bytes=VMEM_LIMIT),
        name="proj_sample",
    )(x2, cos_t, sin_t, *_weight_args(p))
    return dict(zip(("q", "k", "v", "qb", "kb", "vb", "la", "ga", "gb"), outs))


def _band_attn_kernel(bias_ref, q_ref, kp_ref, kc_ref, vp_ref, vc_ref, o_ref, lse_ref, kbuf, vbuf):
    tq = BAND
    kbuf[:, 0:tq] = kp_ref[...]
    kbuf[:, tq:] = kc_ref[...]
    vbuf[:, 0:tq] = vp_ref[...]
    vbuf[:, tq:] = vc_ref[...]
    table0 = jnp.where(pl.program_id(2) == 0, 0, 1)
    lane = lax.broadcasted_iota(jnp.int32, (tq, LANES), 1)
    low = lane < HD_A
    pairs = [slice(hp * LANES, (hp + 1) * LANES) for hp in range(D_A // LANES)]
    n_cls, n_blocks = q_ref.shape[0], q_ref.shape[1] // tq

    def score_phase(c, j):
        scores = []
        for cs in pairs:
            qp = q_ref[c, j * tq:(j + 1) * tq, cs]
            zero = jnp.zeros_like(qp)
            qs = jnp.concatenate([jnp.where(low, qp, zero), jnp.where(low, zero, qp)], axis=0)
            bias = bias_ref[table0] if j == 0 else bias_ref[1]
            scores.append(_dot_nt(qs, kbuf[c, j * tq:(j + 2) * tq, cs]) + bias)
        return scores

    def softmax_phase(scores):
        probs = []
        for s in scores:
            m = jnp.max(s, axis=-1, keepdims=True)
            pr = jnp.exp(s - m)
            probs.append((pr.astype(BF16), m, jnp.sum(pr, axis=-1, keepdims=True)))
        return probs

    def value_phase(c, j, probs):
        m_all = jnp.zeros((tq, LANES), F32)
        den_all = jnp.ones((tq, LANES), F32)
        for hp, (cs, (pr, m, den)) in enumerate(zip(pairs, probs)):
            pv = _dot(pr, vbuf[c, j * tq:(j + 2) * tq, cs]) / den
            o_ref[c, j * tq:(j + 1) * tq, cs] = jnp.where(low, pv[:tq], pv[tq:]).astype(BF16)
            for e in range(2):
                here = lane == 2 * hp + e
                m_all = jnp.where(here, m[e * tq:(e + 1) * tq], m_all)
                den_all = jnp.where(here, den[e * tq:(e + 1) * tq], den_all)
        lse_ref[c, j * tq:(j + 1) * tq, :] = m_all + jnp.log(den_all)

    for c in range(n_cls):
        for j in range(n_blocks):
            value_phase(c, j, softmax_phase(score_phase(c, j)))


def _band_bias():
    i = np.arange(2 * BAND)[:, None] % BAND
    j = np.arange(2 * BAND)[None, :]
    ok = (i + BAND - j >= 0) & (i + BAND - j <= BAND)
    tables = np.stack([ok & (j >= BAND), ok])
    return jnp.asarray(np.where(tables, 0.0, -np.inf), F32)


def _band_attn(q, k, v):
    bsz, ncls, length, _ = q.shape
    tq = min(ATTN_QUERIES, length)
    nc = min(ncls, ATTN_QUERIES // tq)
    ratio = tq // BAND
    cur = lambda b, c, i: (b, c, i, 0)
    prev = lambda b, c, i: (b, c, jnp.maximum(i * ratio - 1, 0), 0)
    cur_spec = lambda w: pl.BlockSpec((None, nc, tq, w), cur)
    prev_spec = pl.BlockSpec((None, nc, BAND, D_A), prev)
    return pl.pallas_call(
        _band_attn_kernel,
        grid=(bsz, ncls // nc, length // tq),
        in_specs=[_const_spec((2, 2 * BAND, 2 * BAND)), cur_spec(D_A), prev_spec, cur_spec(D_A),
                  prev_spec, cur_spec(D_A)],
        out_specs=[cur_spec(D_A), cur_spec(LANES)],
        out_shape=[jax.ShapeDtypeStruct(q.shape, BF16), jax.ShapeDtypeStruct((bsz, ncls, length, LANES), F32)],
        scratch_shapes=[pltpu.VMEM((nc, BAND + tq, D_A), BF16)] * 2,
        compiler_params=pltpu.CompilerParams(dimension_semantics=("arbitrary",) * 3, vmem_limit_bytes=VMEM_LIMIT),
        name=f"band_attn_c{ncls}",
    )(_band_bias(), q, k, k, v, v)


def _sample_attn(q, k_new, v_new, k_cache, v_cache, wc, wn):
    t = q.shape[0]
    head = _div2(lax.broadcasted_iota(jnp.int32, (t, D_A), 1), HD_A)
    zero = jnp.zeros_like(q)
    qbd = jnp.concatenate([jnp.where(head == h, q, zero) for h in range(H_A)], axis=0)
    sc = jnp.where(wc > 0, _dot(qbd, k_cache.astype(BF16)), -jnp.inf)
    sn = jnp.where(wn > 0, _dot_nt(qbd, k_new.astype(BF16)), -jnp.inf)
    m = jnp.maximum(jnp.max(sc, axis=-1, keepdims=True), jnp.max(sn, axis=-1, keepdims=True))
    pc = wc * jnp.exp(sc - m)
    pn = wn * jnp.exp(sn - m)
    den = jnp.sum(pc, axis=-1, keepdims=True) + jnp.sum(pn, axis=-1, keepdims=True)
    o = (_dot_nt(pc.astype(BF16), v_cache.astype(BF16)) + _dot(pn.astype(BF16), v_new.astype(BF16))) / den
    acc = jnp.zeros((t, D_A), F32)
    for h in range(H_A):
        acc = jnp.where(head == h, o[h * t:(h + 1) * t], acc)
    return acc


def _sample_key_weights(win, t_len):
    tt = np.arange(t_len)[:, None]
    rel_c = win + tt - np.arange(win)[None, :]
    rel_n = tt - np.arange(t_len)[None, :]
    def count(rel):
        c = np.zeros(rel.shape, np.float32)
        for window, dil in DILATED:
            c += (rel >= 0) & (rel % dil == 0) & (rel // dil <= window // dil)
        return c
    return np.tile(count(rel_c), (H_A, 1)), np.tile(count(rel_n), (H_A, 1))


def _stack_heads(x):
    low = lax.broadcasted_iota(jnp.int32, x.shape, 1) < DK_B
    zero = jnp.zeros_like(x)
    return jnp.concatenate([jnp.where(low, x, zero), jnp.where(low, zero, x)], axis=0)


def _gla_chunk(q, k, v, g, state, slabs, gw, bdx):
    c = q.shape[0]
    row = lax.broadcasted_iota(jnp.int32, (c, c), 0)
    col = lax.broadcasted_iota(jnp.int32, (c, c), 1)
    tri = (row >= col).astype(BF16)
    g_hi = g.astype(BF16)
    g_lo = (g - g_hi.astype(F32)).astype(BF16)
    b = _dot(tri, g_hi) + _dot(tri, g_lo)
    b_last = b[c - 1:c, :]
    q_in = q * jnp.exp(b)
    k_dec = k * jnp.exp(b_last - b)
    trow2 = _mod2(lax.broadcasted_iota(jnp.int32, (2 * c, c), 0), c)
    col2 = lax.broadcasted_iota(jnp.int32, (2 * c, c), 1)

    levels = []
    p_blk = c
    while p_blk > GLA_SUB:
        half = p_blk // 2
        pieces = [jnp.broadcast_to(b[s + half - 1:s + half, :], (p_blk, D_KB)) for s in range(0, c, p_blk)]
        bm = pieces[0] if len(pieces) == 1 else jnp.concatenate(pieces, axis=0)
        q_l = q * jnp.exp(jnp.minimum(b - bm, 0.0))
        k_l = k * jnp.exp(jnp.minimum(bm - b, 0.0))
        region = ((_div2(trow2, p_blk) == _div2(col2, p_blk)) & (_mod2(trow2, p_blk) >= half)
                  & (_mod2(col2, p_blk) < half))
        levels.append((q_l, k_l, region))
        p_blk = half

    n_ks = D_KB // LANES
    for j in range(n_ks):
        slabs[j] = k[:, j * LANES:(j + 1) * LANES]
        slabs[n_ks + j] = b[:, j * LANES:(j + 1) * LANES]

    def block_row(first_slab, s):
        return jnp.concatenate(
            [jnp.concatenate([jnp.broadcast_to(slabs[first_slab + j, g + s:g + s + 1, :], (GLA_SUB, LANES))
                              for g in range(0, c, GLA_SUB)], axis=0) for j in range(n_ks)], axis=1)

    sub = _mod2(lax.broadcasted_iota(jnp.int32, (c, D_KB), 0), GLA_SUB)
    terms = []
    for s in range(GLA_SUB):
        b_s = block_row(n_ks, s)
        decay = jnp.exp(b - b_s if s == 0 else jnp.where(sub >= s, b - b_s, -jnp.inf))
        terms.append(q * block_row(0, s) * decay)
    sc_d = _dot(jnp.concatenate(terms, axis=0).astype(BF16), bdx)
    key_in_block = col2 - (trow2 - _mod2(trow2, GLA_SUB))

    pair_lanes = [slice(p * LANES, (p + 1) * LANES) for p in range(H_B // 2)]
    o_pairs = [_dot_nt(_stack_heads(q_in[:, ks]).astype(BF16), state[p].astype(BF16))
               for p, ks in enumerate(pair_lanes)]
    level_scores = [[_dot_nt(_stack_heads(q_l[:, ks]).astype(BF16), k_l[:, ks].astype(BF16))
                     for q_l, k_l, _ in levels] for ks in pair_lanes]
    upds = [[_dot_tn(v[:, (2 * p + e) * DV_B:(2 * p + e + 1) * DV_B].astype(BF16), k_dec[:, ks].astype(BF16))
             for e in range(2)] for p, ks in enumerate(pair_lanes)]
    outs, new_state = [], []
    for p, ks in enumerate(pair_lanes):
        sc = jnp.zeros((2 * c, c), F32)
        for s_l, (_, _, region) in zip(level_scores[p], levels):
            sc = jnp.where(region, s_l, sc)
        for s in range(GLA_SUB):
            d_s = jnp.concatenate([sc_d[s * c:(s + 1) * c, (2 * p + e) * DV_B:(2 * p + e) * DV_B + c]
                                   for e in range(2)], axis=0)
            sc = jnp.where(key_in_block == s, d_s, sc)
        sc = sc.astype(BF16)
        for e in range(2):
            hidx = 2 * p + e
            v_h = v[:, hidx * DV_B:(hidx + 1) * DV_B].astype(BF16)
            o_h = o_pairs[p][e * c:(e + 1) * c] + _dot(sc[e * c:(e + 1) * c], v_h)
            ms = jnp.mean(o_h * o_h, axis=-1, keepdims=True)
            outs.append(o_h * lax.rsqrt(ms + EPS) * gw)
        low = lax.broadcasted_iota(jnp.int32, (DV_B, LANES), 1) < DK_B
        new_state.append(state[p] * jnp.exp(b_last[:, ks]) + jnp.where(low, upds[p][0], upds[p][1]))
    return jnp.concatenate(outs, axis=1), new_state


def _gla_kernel(*refs, chunk, n_chunks, n_decode):
    q_ref, k_ref, v_ref, g_ref, s0_ref, gw_ref, bdx_ref = refs[:7]
    n_in = 14 if n_decode else 7
    o_ref, sfin_ref = refs[n_in:n_in + 2]
    st_ref, slabs = refs[-2:]
    step = pl.program_id(1)
    n_seq = q_ref.shape[0]

    @pl.when(step == 0)
    def _():
        for i in range(n_seq):
            for p in range(H_B // 2):
                st_ref[i, p] = s0_ref[i, p * LANES:(p + 1) * LANES, :].T

    if n_decode:
        dq, dk, dv, kcache, vcache, wc, wn = refs[7:14]
        d_out = refs[n_in + 2]
        for i in range(n_decode):
            d_out[i] = _sample_attn(dq[i], dk[i], dv[i], kcache[i], vcache[i], wc[...], wn[...])

    for i in range(n_seq):
        state = [st_ref[i, p] for p in range(H_B // 2)]
        for ci in range(n_chunks):
            rs = slice(ci * chunk, (ci + 1) * chunk)
            o, state = _gla_chunk(q_ref[i, rs, :], k_ref[i, rs, :], v_ref[i, rs, :], g_ref[i, rs, :], state,
                                  slabs.at[i * n_chunks + ci], gw_ref[...], bdx_ref[...])
            o_ref[i, rs, :] = o.astype(o_ref.dtype)
        for p in range(H_B // 2):
            st_ref[i, p] = state[p]

    @pl.when(step == pl.num_programs(1) - 1)
    def _():
        for i in range(n_seq):
            for p in range(H_B // 2):
                sfin_ref[i, p * LANES:(p + 1) * LANES, :] = st_ref[i, p].T


def _gla(qb, kb, vb, la, s0, gw, bdx, decode=None):
    bsz, length, _ = qb.shape
    chunk = int(np.gcd(length, GLA_CHUNK))
    n_chunks = GLA_STEP_CHUNKS if length % (GLA_STEP_CHUNKS * chunk) == 0 else 1
    tl = chunk * n_chunks
    n_seq = GLA_SHORT_BATCH if length == tl and bsz % GLA_SHORT_BATCH == 0 else 1
    grid = (bsz // n_seq, length // tl)
    tok = lambda w: pl.BlockSpec((n_seq, tl, w), lambda b, i: (b, i, 0))
    st = pl.BlockSpec((n_seq, H_B * DK_B, DV_B), lambda b, i: (b, 0, 0))
    in_specs = [tok(D_KB), tok(D_KB), tok(D_B), tok(D_KB), st, _const_spec((1, DV_B)), _const_spec((D_KB, D_B))]
    out_specs = [tok(D_B), st]
    out_shape = [jax.ShapeDtypeStruct((bsz, length, D_B), BF16 if tl % 16 == 0 else F32),
                 jax.ShapeDtypeStruct((bsz, H_B * DK_B, DV_B), F32)]
    args = [qb, kb, vb, la, s0, gw, bdx]
    n_decode = 0
    if decode is not None:
        dq, dk, dv, kcache, vcache = decode
        n_dec, t_len, _ = dq.shape
        win = kcache.shape[2]
        assert n_dec % (grid[0] * grid[1]) == 0, "decode sequences must spread evenly over the grid steps"
        n_decode = n_dec // (grid[0] * grid[1])
        wc, wn = _sample_key_weights(win, t_len)
        by_step = lambda b, i: (b * grid[1] + i, 0, 0)
        new = pl.BlockSpec((n_decode, t_len, D_A), by_step)
        cache = pl.BlockSpec((n_decode, D_A, win), by_step)
        in_specs += [new, new, new, cache, cache, _const_spec(wc.shape), _const_spec(wn.shape)]
        out_specs.append(new)
        out_shape.append(jax.ShapeDtypeStruct((n_dec, t_len, D_A), F32))
        args += [dq, dk, dv, kcache, vcache, jnp.asarray(wc), jnp.asarray(wn)]
    return pl.pallas_call(
        functools.partial(_gla_kernel, chunk=chunk, n_chunks=n_chunks, n_decode=n_decode),
        grid=grid,
        in_specs=in_specs,
        out_specs=out_specs,
        out_shape=out_shape,
        scratch_shapes=[pltpu.VMEM((n_seq, H_B // 2, DV_B, LANES), F32),
                        pltpu.VMEM((n_seq * n_chunks, 2 * D_KB // LANES, chunk, LANES), F32)],
        compiler_params=pltpu.CompilerParams(dimension_semantics=("arbitrary", "arbitrary"),
                                             vmem_limit_bytes=VMEM_LIMIT),
        name="gla_decode_attn" if n_decode else "gla",
    )(*args)


def _mix_and_project(o_a, ga, o_b, gb, x, wo):
    mixed = jnp.concatenate([o_a * ga, o_b * gb], axis=1).astype(BF16)
    return x + _dot(mixed, wo)


def _out_prompt_kernel(*refs):
    nb = len(DILATED)
    att = refs[:2 * nb]
    ga, ob, gb, x_ref, ex_ref, wo_ref, y_ref = refs[2 * nb:2 * nb + 7]
    scratch = list(refs[2 * nb + 7:])
    tm = x_ref.shape[0]
    outs, lses = [], []
    for bi, (_, dil) in enumerate(DILATED):
        o_ref, l_ref = att[2 * bi], att[2 * bi + 1]
        if dil == 1:
            outs.append(o_ref[0].astype(F32))
            lses.append(l_ref[0])
            continue
        oslab, lslab = scratch.pop(0), scratch.pop(0)
        for c in range(dil):
            rows = pl.ds(c, tm // dil, stride=dil)
            for j in range(D_A // LANES):
                oslab[j, rows, :] = o_ref[c, :, j * LANES:(j + 1) * LANES].astype(F32)
            lslab[rows, :] = l_ref[c]
        outs.append(jnp.concatenate([oslab[j] for j in range(D_A // LANES)], axis=1))
        lses.append(lslab[...])
    m = functools.reduce(jnp.maximum, lses)
    es = [jnp.exp(l - m) for l in lses]
    inv = 1.0 / functools.reduce(lambda a, b: a + b, es)
    spread = lambda a: _dot((a * inv).astype(BF16), ex_ref[...])
    o_a = functools.reduce(lambda a, b: a + b, [spread(e) * o for e, o in zip(es, outs)])
    y_ref[...] = _mix_and_project(o_a, ga[...], ob[...], gb[...], x_ref[...], wo_ref[...])


def _out_sample_kernel(oa, ga, ob, gb, x_ref, wo_ref, y_ref):
    y_ref[...] = _mix_and_project(oa[...], ga[...], ob[...], gb[...], x_ref[...], wo_ref[...])


def _out_prompt(att, ga, o_b, gb, x, p):
    bsz, seq, d_model = x.shape
    tm = PROJ_TOKENS
    nat_spec = lambda w: pl.BlockSpec((None, tm, w), lambda b, i: (b, i, 0))
    cls_spec = lambda dil, w: pl.BlockSpec((None, dil, tm // dil, w), lambda b, i: (b, 0, i, 0))
    att_specs, att_args, scratch = [], [], []
    for (_, dil), (o, lse) in zip(DILATED, att):
        att_specs += [cls_spec(dil, D_A), cls_spec(dil, LANES)]
        att_args += [o, lse]
        if dil > 1:
            scratch += [pltpu.VMEM((D_A // LANES, tm, LANES), F32), pltpu.VMEM((tm, LANES), F32)]
    return pl.pallas_call(
        _out_prompt_kernel,
        grid=(bsz, seq // tm),
        in_specs=att_specs + [nat_spec(D_A), nat_spec(D_B), nat_spec(D_B), nat_spec(d_model),
                              _const_spec((LANES, D_A)), _const_spec((D_A + D_B, d_model))],
        out_specs=nat_spec(d_model),
        out_shape=jax.ShapeDtypeStruct((bsz, seq, d_model), F32),
        scratch_shapes=scratch,
        compiler_params=pltpu.CompilerParams(dimension_semantics=("arbitrary", "arbitrary"),
                                             vmem_limit_bytes=VMEM_LIMIT),
        name="out_prompt",
    )(*att_args, ga, o_b, gb, x, p["ex"], p["wo"])


def _out_sample(o_a, ga, o_b, gb, x2, p):
    n, d_model = x2.shape
    tm = 256 if n % 256 == 0 else n
    row = lambda w: pl.BlockSpec((tm, w), lambda i: (i, 0))
    return pl.pallas_call(
        _out_sample_kernel,
        grid=(n // tm,),
        in_specs=[row(D_A), row(D_A), row(D_B), row(D_B), row(d_model), _const_spec((D_A + D_B, d_model))],
        out_specs=row(d_model),
        out_shape=jax.ShapeDtypeStruct((n, d_model), F32),
        compiler_params=pltpu.CompilerParams(dimension_semantics=("arbitrary",), vmem_limit_bytes=VMEM_LIMIT),
        name="out_sample",
    )(o_a, ga, o_b, gb, x2, p["wo"])


def _layer_params(norm_w, w_in, w_gate_up, b_gate, q_norm_w, k_norm_w, gla_norm_w, w_out):
    d_model = w_in.shape[0]
    split = np.cumsum((D_A, D_A, D_A, D_KB, D_KB, D_B, GATE_RANK, D_A, D_B))
    glr = w_in[:, split[5]:split[6]]
    w = jnp.concatenate([w_in[:, :split[5]], glr, jnp.zeros((d_model, LANES - GATE_RANK), w_in.dtype),
                         w_in[:, split[6]:]], axis=1).astype(BF16)
    head_of = np.arange(D_A) // HD_A
    bd = jnp.asarray(head_of[:MXU_DIM, None] == head_of[None, :MXU_DIM], BF16)
    ex = jnp.asarray(np.arange(LANES)[:, None] == head_of[None, :], BF16)
    bdx = jnp.asarray((np.arange(D_KB) // DK_B)[:, None] == (np.arange(D_B) // DV_B)[None, :], BF16)
    wgu = jnp.concatenate([w_gate_up, jnp.zeros((LANES - GATE_RANK, D_KB), w_gate_up.dtype)], axis=0).astype(BF16)
    return dict(nw=norm_w.reshape(1, d_model), w=w, bd=bd, ex=ex, bdx=bdx,
                wq=jnp.tile(q_norm_w, H_A).reshape(1, D_A), wk=jnp.tile(k_norm_w, H_A).reshape(1, D_A),
                wgu=wgu, bg=b_gate.reshape(1, D_KB), gw=gla_norm_w.reshape(1, DV_B), wo=w_out.astype(BF16))


def _layer(xp, xs, p, keep, k_cache, v_cache, s0):
    bsz, seq, _ = xp.shape
    bs, t_len, d_model = xs.shape
    win = k_cache.shape[1]
    rs = _proj_sample(xs.reshape(bs * t_len, d_model), PAST_LEN + jnp.tile(jnp.arange(t_len), bs), p)
    tok = lambda a: a.reshape(bs, t_len, a.shape[-1])
    by_pos = lambda a: a.transpose(0, 2, 3, 1).reshape(bs, D_A, win)

    r = _proj_prompt(xp, p, keep)
    by_class = lambda a, dil: a.reshape(bsz, dil, seq // dil, D_A)
    att = [_band_attn(*(by_class(r[f"{n}{dil}"], dil) for n in "qkv")) for _, dil in DILATED]
    o_b, s_fin, o_as = _gla(r["qb"], r["kb"], r["vb"], r["la"], jnp.zeros((bsz, H_B * DK_B, DV_B), F32),
                            p["gw"], p["bdx"],
                            decode=(tok(rs["q"]), tok(rs["k"]), tok(rs["v"]), by_pos(k_cache), by_pos(v_cache)))
    yp = _out_prompt(att, r["ga"], o_b, r["gb"], xp, p)
    window = lambda a: a.reshape(bsz, H_A, HD_A, keep).transpose(0, 3, 1, 2)

    o_bs, s_fin_s = _gla(tok(rs["qb"]), tok(rs["kb"]), tok(rs["vb"]), tok(rs["la"]),
                         s0.reshape(bs, H_B * DK_B, DV_B), p["gw"], p["bdx"])
    ys = _out_sample(o_as.reshape(bs * t_len, D_A), rs["ga"], o_bs.reshape(bs * t_len, D_B), rs["gb"],
                     xs.reshape(bs * t_len, d_model), p)
    heads = lambda a: a.reshape(bs, t_len, H_A, HD_A)
    state = lambda a, n: a.reshape(n, H_B, DK_B, DV_B)
    return (yp, ys.reshape(bs, t_len, d_model), window(r["kt"]), window(r["vt"]), state(s_fin, bsz),
            heads(rs["k"]), heads(rs["v"]), state(s_fin_s, bs))


def kernel(x_prompt, x_sample, cache_k_win, cache_v_win, state_gla, norm_w, w_in, w_gate_up, b_gate,
           q_norm_w, k_norm_w, gla_norm_w, w_out):
    seq = x_prompt.shape[1]
    assert seq % (MAX_DIL * BAND) == 0, "prompt length must tile into residue-class bands"
    assert all(w // d == BAND and MAX_DIL % d == 0 for w, d in DILATED)
    keep = min(MAX_WINDOW, seq)
    hp, hs = x_prompt, x_sample
    outs = [[] for _ in range(6)]
    for layer in range(w_in.shape[0]):
        p = _layer_params(norm_w[layer], w_in[layer], w_gate_up[layer], b_gate[layer], q_norm_w[layer],
                          k_norm_w[layer], gla_norm_w[layer], w_out[layer])
        hp, hs, kp, vp, st_p, kn, vn, st_s = _layer(hp, hs, p, keep, cache_k_win[layer], cache_v_win[layer],
                                                    state_gla[layer])
        for lst, val in zip(outs, (kp, vp, st_p.astype(state_gla.dtype), kn, vn, st_s.astype(state_gla.dtype))):
            lst.append(val)
    return (hp, hs) + tuple(jnp.stack(o) for o in outs)
```

```python
import functools

import numpy as np
import jax
import jax.numpy as jnp
from jax import lax
from jax.experimental import pallas as pl
from jax.experimental.pallas import tpu as pltpu

F32 = jnp.float32
BF16 = jnp.bfloat16

H_A, HD_A = 8, 64
D_A = H_A * HD_A
H_B, DK_B, DV_B = 4, 64, 128
D_KB = H_B * DK_B
D_B = H_B * DV_B
GATE_RANK = 16
GATE_TAU = 16.0
DILATED = ((128, 1), (512, 4), (2048, 16))
MAX_DIL = 16
MAX_WINDOW = 2048
BAND = 128
ROPE_THETA = 10000.0
EPS = 1e-6
PAST_LEN = 8192
GLA_CHUNK = 128
GLA_STEP_CHUNKS = 4
GLA_SHORT_BATCH = 8
GLA_SUB = 8
LANES = 128
MXU_DIM = 256
VMEM_LIMIT = 56 * 1024 * 1024
PROJ_TOKENS = 512
OUT_TOKENS = 1024
ATTN_QUERIES = 1024

_SECTIONS = (("q", D_A), ("k", D_A), ("v", D_A), ("qb", D_KB), ("kb", D_KB), ("vb", D_B),
             ("glr", LANES), ("ga", D_A), ("gb", D_B))
_OFF = {}
_o = 0
for _n, _w in _SECTIONS:
    _OFF[_n] = (_o, _o + _w)
    _o += _w
D_PAD = _o


def _mod2(x, n):
    assert n & (n - 1) == 0
    return jnp.bitwise_and(x, n - 1)


def _div2(x, n):
    assert n & (n - 1) == 0
    return jnp.right_shift(x, n.bit_length() - 1)


def _dot(a, b):
    return jnp.dot(a, b, preferred_element_type=F32)


def _dot_nt(a, b):
    return lax.dot_general(a, b, (((1,), (1,)), ((), ())), preferred_element_type=F32)


def _dot_tn(a, b):
    return lax.dot_general(a, b, (((0,), (0,)), ((), ())), preferred_element_type=F32)


def _proj_tile(x, cos, sin, nw, w_ref, bd, wq, wk, wgu, bg, emit):
    ms = jnp.mean(x * x, axis=-1, keepdims=True)
    h = (x * lax.rsqrt(ms + EPS) * nw).astype(BF16)
    sec = lambda name: _dot(h, w_ref[:, _OFF[name][0]:_OFF[name][1]])
    lane = lax.broadcasted_iota(jnp.int32, (x.shape[0], LANES), 1)
    first_half = _mod2(lane, HD_A) < (HD_A // 2)

    def qk_norm_rope(zz, wn):
        sq = (zz * zz).astype(BF16)
        ss = jnp.concatenate([_dot(sq[:, c:c + MXU_DIM], bd) for c in range(0, D_A, MXU_DIM)], axis=1)
        y = zz * lax.rsqrt(ss * (1.0 / HD_A) + EPS) * wn
        outs = []
        for j in range(D_A // LANES):
            yj = y[:, j * LANES:(j + 1) * LANES]
            swapped = jnp.where(first_half, pltpu.roll(yj, LANES - HD_A // 2, 1),
                                pltpu.roll(yj, HD_A // 2, 1))
            outs.append(yj * cos + swapped * sin)
        return jnp.concatenate(outs, axis=1)

    emit("q", qk_norm_rope(sec("q"), wq) * (HD_A ** -0.5))
    emit("k", qk_norm_rope(sec("k"), wk))
    emit("v", sec("v"))
    emit("qb", sec("qb") * (DK_B ** -0.5))
    emit("kb", sec("kb"))
    emit("vb", sec("vb"))
    xg = _dot(sec("glr").astype(BF16), wgu) + bg
    emit("la", (jnp.minimum(xg, 0.0) - jnp.log1p(jnp.exp(-jnp.abs(xg)))) * (1.0 / GATE_TAU))
    for name in ("ga", "gb"):
        gate = sec(name)
        emit(name, gate / (1.0 + jnp.exp(-gate)))


def _proj_prompt_kernel(x_ref, cos_ref, sin_ref, nw_ref, w_ref, bd_ref, wq_ref, wk_ref, wgu_ref, bg_ref,
                        q1, k1, v1, q4, k4, v4, q16, k16, v16, kt, vt, qb, kb, vb, la, ga, gb, slabs):
    tm = x_ref.shape[0]
    plain = dict(qb=qb, kb=kb, vb=vb, la=la, ga=ga, gb=gb)
    attn = dict(q=(0, (q1, q4, q16), None), k=(1, (k1, k4, k16), kt), v=(2, (v1, v4, v16), vt))

    def emit(name, val):
        if name in plain:
            plain[name][...] = val.astype(plain[name].dtype)
            return
        si, refs, win_t = attn[name]
        src = slabs.at[2 * si]
        for j in range(D_A // LANES):
            src[j] = val[:, j * LANES:(j + 1) * LANES]
        refs[0][...] = val.astype(BF16)
        for level in range(1, len(DILATED)):
            prev, dil = DILATED[level - 1][1], DILATED[level][1]
            f, n = dil // prev, tm // dil
            dst = slabs.at[2 * si + level % 2]
            for c_prev in range(prev):
                for a in range(f):
                    c = c_prev + prev * a
                    for j in range(D_A // LANES):
                        piece = src[j, pl.ds(c_prev * (tm // prev) + a, n, stride=f), :]
                        refs[level][c, :, j * LANES:(j + 1) * LANES] = piece.astype(BF16)
                        if level + 1 < len(DILATED):
                            dst[j, c * n:(c + 1) * n, :] = piece
            src = dst
        if win_t is not None:
            win_t[...] = val.T

    _proj_tile(x_ref[...], cos_ref[...], sin_ref[...], nw_ref[...], w_ref, bd_ref[...],
               wq_ref[...], wk_ref[...], wgu_ref[...], bg_ref[...], emit)


def _proj_sample_kernel(x_ref, cos_ref, sin_ref, nw_ref, w_ref, bd_ref, wq_ref, wk_ref, wgu_ref, bg_ref,
                        q, k, v, qb, kb, vb, la, ga, gb):
    refs = dict(q=q, k=k, v=v, qb=qb, kb=kb, vb=vb, la=la, ga=ga, gb=gb)

    def emit(name, val):
        refs[name][...] = val.astype(refs[name].dtype)

    _proj_tile(x_ref[...], cos_ref[...], sin_ref[...], nw_ref[...], w_ref, bd_ref[...],
               wq_ref[...], wk_ref[...], wgu_ref[...], bg_ref[...], emit)


def _rope_tables(pos):
    half = HD_A // 2
    inv_freq = ROPE_THETA ** (-jnp.arange(half, dtype=F32) / half)
    ang = pos.astype(F32)[:, None] * inv_freq[None, :]
    cos, sin = jnp.cos(ang), jnp.sin(ang)
    reps = LANES // HD_A
    cos_t = jnp.tile(jnp.concatenate([cos, cos], axis=1), (1, reps))
    sin_t = jnp.tile(jnp.concatenate([-sin, sin], axis=1), (1, reps))
    return cos_t, sin_t


def _const_spec(shape):
    return pl.BlockSpec(shape, lambda *_: (0,) * len(shape))


def _weight_args(p):
    return (p["nw"], p["w"], p["bd"], p["wq"], p["wk"], p["wgu"], p["bg"])


def _weight_specs(d_model):
    return [_const_spec((1, d_model)), _const_spec((d_model, D_PAD)), _const_spec((MXU_DIM, MXU_DIM)),
            _const_spec((1, D_A)), _const_spec((1, D_A)), _const_spec((LANES, D_KB)), _const_spec((1, D_KB))]


def _proj_prompt(x, p, keep):
    bsz, seq, d_model = x.shape
    tm = PROJ_TOKENS
    n_skip = (seq - keep) // tm
    cos_t, sin_t = _rope_tables(jnp.arange(seq))
    res = lambda dil: jax.ShapeDtypeStruct((bsz, dil, seq // dil, D_A), BF16)
    nat = lambda w, dt: jax.ShapeDtypeStruct((bsz, seq, w), dt)
    win_t = jax.ShapeDtypeStruct((bsz, D_A, keep), F32)
    res_spec = lambda dil: pl.BlockSpec((None, dil, tm // dil, D_A), lambda b, i: (b, 0, i, 0))
    nat_spec = lambda w: pl.BlockSpec((None, tm, w), lambda b, i: (b, i, 0))
    win_spec = pl.BlockSpec((None, D_A, tm), lambda b, i: (b, 0, jnp.maximum(i - n_skip, 0)))
    tab_spec = pl.BlockSpec((tm, LANES), lambda b, i: (i, 0))
    qkv_specs, qkv_shapes = [], []
    for _, dil in DILATED:
        qkv_specs += [nat_spec(D_A) if dil == 1 else res_spec(dil)] * 3
        qkv_shapes += [nat(D_A, BF16) if dil == 1 else res(dil)] * 3
    outs = pl.pallas_call(
        _proj_prompt_kernel,
        grid=(bsz, seq // tm),
        in_specs=[nat_spec(d_model), tab_spec, tab_spec] + _weight_specs(d_model),
        out_specs=qkv_specs + [win_spec] * 2
                  + [nat_spec(D_KB), nat_spec(D_KB), nat_spec(D_B), nat_spec(D_KB), nat_spec(D_A), nat_spec(D_B)],
        out_shape=qkv_shapes + [win_t] * 2
                  + [nat(D_KB, F32), nat(D_KB, F32), nat(D_B, F32), nat(D_KB, F32), nat(D_A, BF16), nat(D_B, BF16)],
        scratch_shapes=[pltpu.VMEM((6, D_A // LANES, tm, LANES), F32)],
        compiler_params=pltpu.CompilerParams(dimension_semantics=("arbitrary", "arbitrary"),
                                             vmem_limit_bytes=VMEM_LIMIT),
        name="proj_prompt",
    )(x, cos_t, sin_t, *_weight_args(p))
    names = [f"{n}{dil}" for _, dil in DILATED for n in "qkv"] + ["kt", "vt", "qb", "kb", "vb", "la", "ga", "gb"]
    return dict(zip(names, outs))


def _proj_sample(x2, pos, p):
    n, d_model = x2.shape
    tm = 256 if n % 256 == 0 else n
    cos_t, sin_t = _rope_tables(pos)
    row = lambda w: pl.BlockSpec((tm, w), lambda i: (i, 0))
    widths = (D_A, D_A, D_A, D_KB, D_KB, D_B, D_KB, D_A, D_B)
    outs = pl.pallas_call(
        _proj_sample_kernel,
        grid=(n // tm,),
        in_specs=[row(d_model), row(LANES), row(LANES)] + _weight_specs(d_model),
        out_specs=[row(w) for w in widths],
        out_shape=[jax.ShapeDtypeStruct((n, w), BF16 if i == 0 else F32) for i, w in enumerate(widths)],
        compiler_params=pltpu.CompilerParams(dimension_semantics=("arbitrary",), vmem_limit_bytes=VMEM_LIMIT),
        name="proj_sample",
    )(x2, cos_t, sin_t, *_weight_args(p))
    return dict(zip(("q", "k", "v", "qb", "kb", "vb", "la", "ga", "gb"), outs))


def _band_attn_kernel(bias_ref, q_ref, kp_ref, kc_ref, vp_ref, vc_ref, o_ref, lse_ref, kbuf, vbuf):
    tq = BAND
    kbuf[:, 0:tq] = kp_ref[...]
    kbuf[:, tq:] = kc_ref[...]
    vbuf[:, 0:tq] = vp_ref[...]
    vbuf[:, tq:] = vc_ref[...]
    table0 = jnp.where(pl.program_id(2) == 0, 0, 1)
    lane = lax.broadcasted_iota(jnp.int32, (tq, LANES), 1)
    low = lane < HD_A
    pairs = [slice(hp * LANES, (hp + 1) * LANES) for hp in range(D_A // LANES)]
    n_cls, n_blocks = q_ref.shape[0], q_ref.shape[1] // tq

    def score_phase(c, j):
        scores = []
        for cs in pairs:
            qp = q_ref[c, j * tq:(j + 1) * tq, cs]
            zero = jnp.zeros_like(qp)
            qs = jnp.concatenate([jnp.where(low, qp, zero), jnp.where(low, zero, qp)], axis=0)
            bias = bias_ref[table0] if j == 0 else bias_ref[1]
            scores.append(_dot_nt(qs, kbuf[c, j * tq:(j + 2) * tq, cs]) + bias)
        return scores

    def softmax_phase(scores):
        probs = []
        for s in scores:
            m = jnp.max(s, axis=-1, keepdims=True)
            pr = jnp.exp(s - m)
            probs.append((pr.astype(BF16), m, jnp.sum(pr, axis=-1, keepdims=True)))
        return probs

    def value_phase(c, j, probs):
        m_all = jnp.zeros((tq, LANES), F32)
        den_all = jnp.ones((tq, LANES), F32)
        for hp, (cs, (pr, m, den)) in enumerate(zip(pairs, probs)):
            pv = _dot(pr, vbuf[c, j * tq:(j + 2) * tq, cs]) / den
            o_ref[c, j * tq:(j + 1) * tq, cs] = jnp.where(low, pv[:tq], pv[tq:]).astype(BF16)
            for e in range(2):
                here = lane == 2 * hp + e
                m_all = jnp.where(here, m[e * tq:(e + 1) * tq], m_all)
                den_all = jnp.where(here, den[e * tq:(e + 1) * tq], den_all)
        lse_ref[c, j * tq:(j + 1) * tq, :] = m_all + jnp.log(den_all)

    for c in range(n_cls):
        for j in range(n_blocks):
            value_phase(c, j, softmax_phase(score_phase(c, j)))


def _band_bias():
    i = np.arange(2 * BAND)[:, None] % BAND
    j = np.arange(2 * BAND)[None, :]
    ok = (i + BAND - j >= 0) & (i + BAND - j <= BAND)
    tables = np.stack([ok & (j >= BAND), ok])
    return jnp.asarray(np.where(tables, 0.0, -np.inf), F32)


def _band_attn(q, k, v):
    bsz, ncls, length, _ = q.shape
    tq = min(ATTN_QUERIES, length)
    nc = min(ncls, ATTN_QUERIES // tq)
    ratio = tq // BAND
    cur = lambda b, c, i: (b, c, i, 0)
    prev = lambda b, c, i: (b, c, jnp.maximum(i * ratio - 1, 0), 0)
    cur_spec = lambda w: pl.BlockSpec((None, nc, tq, w), cur)
    prev_spec = pl.BlockSpec((None, nc, BAND, D_A), prev)
    return pl.pallas_call(
        _band_attn_kernel,
        grid=(bsz, ncls // nc, length // tq),
        in_specs=[_const_spec((2, 2 * BAND, 2 * BAND)), cur_spec(D_A), prev_spec, cur_spec(D_A),
                  prev_spec, cur_spec(D_A)],
        out_specs=[cur_spec(D_A), cur_spec(LANES)],
        out_shape=[jax.ShapeDtypeStruct(q.shape, BF16), jax.ShapeDtypeStruct((bsz, ncls, length, LANES), F32)],
        scratch_shapes=[pltpu.VMEM((nc, BAND + tq, D_A), BF16)] * 2,
        compiler_params=pltpu.CompilerParams(dimension_semantics=("arbitrary",) * 3, vmem_limit_bytes=VMEM_LIMIT),
        name=f"band_attn_c{ncls}",
    )(_band_bias(), q, k, k, v, v)


def _sample_attn(q, k_new, v_new, k_cache, v_cache, wc, wn):
    t = q.shape[0]
    head = _div2(lax.broadcasted_iota(jnp.int32, (t, D_A), 1), HD_A)
    zero = jnp.zeros_like(q)
    qbd = jnp.concatenate([jnp.where(head == h, q, zero) for h in range(H_A)], axis=0)
    sc = jnp.where(wc > 0, _dot(qbd, k_cache.astype(BF16)), -jnp.inf)
    sn = jnp.where(wn > 0, _dot_nt(qbd, k_new.astype(BF16)), -jnp.inf)
    m = jnp.maximum(jnp.max(sc, axis=-1, keepdims=True), jnp.max(sn, axis=-1, keepdims=True))
    pc = wc * jnp.exp(sc - m)
    pn = wn * jnp.exp(sn - m)
    den = jnp.sum(pc, axis=-1, keepdims=True) + jnp.sum(pn, axis=-1, keepdims=True)
    o = (_dot_nt(pc.astype(BF16), v_cache.astype(BF16)) + _dot(pn.astype(BF16), v_new.astype(BF16))) / den
    acc = jnp.zeros((t, D_A), F32)
    for h in range(H_A):
        acc = jnp.where(head == h, o[h * t:(h + 1) * t], acc)
    return acc


def _sample_key_weights(win, t_len):
    tt = np.arange(t_len)[:, None]
    rel_c = win + tt - np.arange(win)[None, :]
    rel_n = tt - np.arange(t_len)[None, :]
    def count(rel):
        c = np.zeros(rel.shape, np.float32)
        for window, dil in DILATED:
            c += (rel >= 0) & (rel % dil == 0) & (rel // dil <= window // dil)
        return c
    return np.tile(count(rel_c), (H_A, 1)), np.tile(count(rel_n), (H_A, 1))


def _stack_heads(x):
    low = lax.broadcasted_iota(jnp.int32, x.shape, 1) < DK_B
    zero = jnp.zeros_like(x)
    return jnp.concatenate([jnp.where(low, x, zero), jnp.where(low, zero, x)], axis=0)


def _gla_chunk(q, k, v, g, state, slabs, gw, bdx):
    c = q.shape[0]
    row = lax.broadcasted_iota(jnp.int32, (c, c), 0)
    col = lax.broadcasted_iota(jnp.int32, (c, c), 1)
    tri = (row >= col).astype(BF16)
    g_hi = g.astype(BF16)
    g_lo = (g - g_hi.astype(F32)).astype(BF16)
    b = _dot(tri, g_hi) + _dot(tri, g_lo)
    b_last = b[c - 1:c, :]
    q_in = q * jnp.exp(b)
    k_dec = k * jnp.exp(b_last - b)
    trow2 = _mod2(lax.broadcasted_iota(jnp.int32, (2 * c, c), 0), c)
    col2 = lax.broadcasted_iota(jnp.int32, (2 * c, c), 1)

    levels = []
    p_blk = c
    while p_blk > GLA_SUB:
        half = p_blk // 2
        pieces = [jnp.broadcast_to(b[s + half - 1:s + half, :], (p_blk, D_KB)) for s in range(0, c, p_blk)]
        bm = pieces[0] if len(pieces) == 1 else jnp.concatenate(pieces, axis=0)
        q_l = q * jnp.exp(jnp.minimum(b - bm, 0.0))
        k_l = k * jnp.exp(jnp.minimum(bm - b, 0.0))
        region = ((_div2(trow2, p_blk) == _div2(col2, p_blk)) & (_mod2(trow2, p_blk) >= half)
                  & (_mod2(col2, p_blk) < half))
        levels.append((q_l, k_l, region))
        p_blk = half

    n_ks = D_KB // LANES
    for j in range(n_ks):
        slabs[j] = k[:, j * LANES:(j + 1) * LANES]
        slabs[n_ks + j] = b[:, j * LANES:(j + 1) * LANES]

    def block_row(first_slab, s):
        return jnp.concatenate(
            [jnp.concatenate([jnp.broadcast_to(slabs[first_slab + j, g + s:g + s + 1, :], (GLA_SUB, LANES))
                              for g in range(0, c, GLA_SUB)], axis=0) for j in range(n_ks)], axis=1)

    sub = _mod2(lax.broadcasted_iota(jnp.int32, (c, D_KB), 0), GLA_SUB)
    terms = []
    for s in range(GLA_SUB):
        b_s = block_row(n_ks, s)
        decay = jnp.exp(b - b_s if s == 0 else jnp.where(sub >= s, b - b_s, -jnp.inf))
        terms.append(q * block_row(0, s) * decay)
    sc_d = _dot(jnp.concatenate(terms, axis=0).astype(BF16), bdx)
    key_in_block = col2 - (trow2 - _mod2(trow2, GLA_SUB))

    pair_lanes = [slice(p * LANES, (p + 1) * LANES) for p in range(H_B // 2)]
    o_pairs = [_dot_nt(_stack_heads(q_in[:, ks]).astype(BF16), state[p].astype(BF16))
               for p, ks in enumerate(pair_lanes)]
    level_scores = [[_dot_nt(_stack_heads(q_l[:, ks]).astype(BF16), k_l[:, ks].astype(BF16))
                     for q_l, k_l, _ in levels] for ks in pair_lanes]
    upds = [[_dot_tn(v[:, (2 * p + e) * DV_B:(2 * p + e + 1) * DV_B].astype(BF16), k_dec[:, ks].astype(BF16))
             for e in range(2)] for p, ks in enumerate(pair_lanes)]
    outs, new_state = [], []
    for p, ks in enumerate(pair_lanes):
        sc = jnp.zeros((2 * c, c), F32)
        for s_l, (_, _, region) in zip(level_scores[p], levels):
            sc = jnp.where(region, s_l, sc)
        for s in range(GLA_SUB):
            d_s = jnp.concatenate([sc_d[s * c:(s + 1) * c, (2 * p + e) * DV_B:(2 * p + e) * DV_B + c]
                                   for e in range(2)], axis=0)
            sc = jnp.where(key_in_block == s, d_s, sc)
        sc = sc.astype(BF16)
        for e in range(2):
            hidx = 2 * p + e
            v_h = v[:, hidx * DV_B:(hidx + 1) * DV_B].astype(BF16)
            o_h = o_pairs[p][e * c:(e + 1) * c] + _dot(sc[e * c:(e + 1) * c], v_h)
            ms = jnp.mean(o_h * o_h, axis=-1, keepdims=True)
            outs.append(o_h * lax.rsqrt(ms + EPS) * gw)
        low = lax.broadcasted_iota(jnp.int32, (DV_B, LANES), 1) < DK_B
        new_state.append(state[p] * jnp.exp(b_last[:, ks]) + jnp.where(low, upds[p][0], upds[p][1]))
    return jnp.concatenate(outs, axis=1), new_state


def _gla_kernel(*refs, chunk, n_chunks, n_decode):
    q_ref, k_ref, v_ref, g_ref, s0_ref, gw_ref, bdx_ref = refs[:7]
    n_in = 14 if n_decode else 7
    o_ref, sfin_ref = refs[n_in:n_in + 2]
    st_ref, slabs = refs[-2:]
    step = pl.program_id(1)
    n_seq = q_ref.shape[0]

    @pl.when(step == 0)
    def _():
        for i in range(n_seq):
            for p in range(H_B // 2):
                st_ref[i, p] = s0_ref[i, p * LANES:(p + 1) * LANES, :].T

    if n_decode:
        dq, dk, dv, kcache, vcache, wc, wn = refs[7:14]
        d_out = refs[n_in + 2]
        for i in range(n_decode):
            d_out[i] = _sample_attn(dq[i], dk[i], dv[i], kcache[i], vcache[i], wc[...], wn[...])

    for i in range(n_seq):
        state = [st_ref[i, p] for p in range(H_B // 2)]
        for ci in range(n_chunks):
            rs = slice(ci * chunk, (ci + 1) * chunk)
            o, state = _gla_chunk(q_ref[i, rs, :], k_ref[i, rs, :], v_ref[i, rs, :], g_ref[i, rs, :], state,
                                  slabs.at[i * n_chunks + ci], gw_ref[...], bdx_ref[...])
            o_ref[i, rs, :] = o.astype(o_ref.dtype)
        for p in range(H_B // 2):
            st_ref[i, p] = state[p]

    @pl.when(step == pl.num_programs(1) - 1)
    def _():
        for i in range(n_seq):
            for p in range(H_B // 2):
                sfin_ref[i, p * LANES:(p + 1) * LANES, :] = st_ref[i, p].T


def _gla(qb, kb, vb, la, s0, gw, bdx, decode=None):
    bsz, length, _ = qb.shape
    chunk = int(np.gcd(length, GLA_CHUNK))
    n_chunks = GLA_STEP_CHUNKS if length % (GLA_STEP_CHUNKS * chunk) == 0 else 1
    tl = chunk * n_chunks
    n_seq = GLA_SHORT_BATCH if length == tl and bsz % GLA_SHORT_BATCH == 0 else 1
    grid = (bsz // n_seq, length // tl)
    tok = lambda w: pl.BlockSpec((n_seq, tl, w), lambda b, i: (b, i, 0))
    st = pl.BlockSpec((n_seq, H_B * DK_B, DV_B), lambda b, i: (b, 0, 0))
    in_specs = [tok(D_KB), tok(D_KB), tok(D_B), tok(D_KB), st, _const_spec((1, DV_B)), _const_spec((D_KB, D_B))]
    out_specs = [tok(D_B), st]
    out_shape = [jax.ShapeDtypeStruct((bsz, length, D_B), BF16 if tl % 16 == 0 else F32),
                 jax.ShapeDtypeStruct((bsz, H_B * DK_B, DV_B), F32)]
    args = [qb, kb, vb, la, s0, gw, bdx]
    n_decode = 0
    if decode is not None:
        dq, dk, dv, kcache, vcache = decode
        n_dec, t_len, _ = dq.shape
        win = kcache.shape[2]
        assert n_dec % (grid[0] * grid[1]) == 0, "decode sequences must spread evenly over the grid steps"
        n_decode = n_dec // (grid[0] * grid[1])
        wc, wn = _sample_key_weights(win, t_len)
        by_step = lambda b, i: (b * grid[1] + i, 0, 0)
        new = pl.BlockSpec((n_decode, t_len, D_A), by_step)
        cache = pl.BlockSpec((n_decode, D_A, win), by_step)
        in_specs += [new, new, new, cache, cache, _const_spec(wc.shape), _const_spec(wn.shape)]
        out_specs.append(new)
        out_shape.append(jax.ShapeDtypeStruct((n_dec, t_len, D_A), F32))
        args += [dq, dk, dv, kcache, vcache, jnp.asarray(wc), jnp.asarray(wn)]
    return pl.pallas_call(
        functools.partial(_gla_kernel, chunk=chunk, n_chunks=n_chunks, n_decode=n_decode),
        grid=grid,
        in_specs=in_specs,
        out_specs=out_specs,
        out_shape=out_shape,
        scratch_shapes=[pltpu.VMEM((n_seq, H_B // 2, DV_B, LANES), F32),
                        pltpu.VMEM((n_seq * n_chunks, 2 * D_KB // LANES, chunk, LANES), F32)],
        compiler_params=pltpu.CompilerParams(dimension_semantics=("arbitrary", "arbitrary"),
                                             vmem_limit_bytes=VMEM_LIMIT),
        name="gla_decode_attn" if n_decode else "gla",
    )(*args)


def _mix_and_project(o_a, ga, o_b, gb, x, wo):
    mixed = jnp.concatenate([o_a * ga, o_b * gb], axis=1).astype(BF16)
    return x + _dot(mixed, wo)


def _out_prompt_kernel(*refs):
    nb = len(DILATED)
    att = refs[:2 * nb]
    ga, ob, gb, x_ref, ex_ref, wo_ref, y_ref = refs[2 * nb:2 * nb + 7]
    scratch = list(refs[2 * nb + 7:])
    tm = x_ref.shape[0]
    outs, lses = [], []
    for bi, (_, dil) in enumerate(DILATED):
        o_ref, l_ref = att[2 * bi], att[2 * bi + 1]
        if dil == 1:
            outs.append(o_ref[0].astype(F32))
            lses.append(l_ref[0])
            continue
        oslab, lslab = scratch.pop(0), scratch.pop(0)
        for c in range(dil):
            rows = pl.ds(c, tm // dil, stride=dil)
            for j in range(D_A // LANES):
                oslab[j, rows, :] = o_ref[c, :, j * LANES:(j + 1) * LANES].astype(F32)
            lslab[rows, :] = l_ref[c]
        outs.append(jnp.concatenate([oslab[j] for j in range(D_A // LANES)], axis=1))
        lses.append(lslab[...])
    m = functools.reduce(jnp.maximum, lses)
    es = [jnp.exp(l - m) for l in lses]
    inv = 1.0 / functools.reduce(lambda a, b: a + b, es)
    spread = lambda a: _dot((a * inv).astype(BF16), ex_ref[...])
    o_a = functools.reduce(lambda a, b: a + b, [spread(e) * (o - outs[-1]) for e, o in zip(es[:-1], outs[:-1])],
                           outs[-1])
    y_ref[...] = _mix_and_project(o_a, ga[...], ob[...], gb[...], x_ref[...], wo_ref[...])


def _out_sample_kernel(oa, ga, ob, gb, x_ref, wo_ref, y_ref):
    y_ref[...] = _mix_and_project(oa[...], ga[...], ob[...], gb[...], x_ref[...], wo_ref[...])


def _out_prompt(att, ga, o_b, gb, x, p):
    bsz, seq, d_model = x.shape
    tm = OUT_TOKENS
    nat_spec = lambda w: pl.BlockSpec((None, tm, w), lambda b, i: (b, i, 0))
    cls_spec = lambda dil, w: pl.BlockSpec((None, dil, tm // dil, w), lambda b, i: (b, 0, i, 0))
    att_specs, att_args, scratch = [], [], []
    for (_, dil), (o, lse) in zip(DILATED, att):
        att_specs += [cls_spec(dil, D_A), cls_spec(dil, LANES)]
        att_args += [o, lse]
        if dil > 1:
            scratch += [pltpu.VMEM((D_A // LANES, tm, LANES), F32), pltpu.VMEM((tm, LANES), F32)]
    return pl.pallas_call(
        _out_prompt_kernel,
        grid=(bsz, seq // tm),
        in_specs=att_specs + [nat_spec(D_A), nat_spec(D_B), nat_spec(D_B), nat_spec(d_model),
                              _const_spec((LANES, D_A)), _const_spec((D_A + D_B, d_model))],
        out_specs=nat_spec(d_model),
        out_shape=jax.ShapeDtypeStruct((bsz, seq, d_model), F32),
        scratch_shapes=scratch,
        compiler_params=pltpu.CompilerParams(dimension_semantics=("arbitrary", "arbitrary"),
                                             vmem_limit_bytes=VMEM_LIMIT),
        name="out_prompt",
    )(*att_args, ga, o_b, gb, x, p["ex"], p["wo"])


def _out_sample(o_a, ga, o_b, gb, x2, p):
    n, d_model = x2.shape
    tm = 256 if n % 256 == 0 else n
    row = lambda w: pl.BlockSpec((tm, w), lambda i: (i, 0))
    return pl.pallas_call(
        _out_sample_kernel,
        grid=(n // tm,),
        in_specs=[row(D_A), row(D_A), row(D_B), row(D_B), row(d_model), _const_spec((D_A + D_B, d_model))],
        out_specs=row(d_model),
        out_shape=jax.ShapeDtypeStruct((n, d_model), F32),
        compiler_params=pltpu.CompilerParams(dimension_semantics=("arbitrary",), vmem_limit_bytes=VMEM_LIMIT),
        name="out_sample",
    )(o_a, ga, o_b, gb, x2, p["wo"])


def _layer_params(norm_w, w_in, w_gate_up, b_gate, q_norm_w, k_norm_w, gla_norm_w, w_out):
    d_model = w_in.shape[0]
    split = np.cumsum((D_A, D_A, D_A, D_KB, D_KB, D_B, GATE_RANK, D_A, D_B))
    glr = w_in[:, split[5]:split[6]]
    w = jnp.concatenate([w_in[:, :split[5]], glr, jnp.zeros((d_model, LANES - GATE_RANK), w_in.dtype),
                         w_in[:, split[6]:]], axis=1).astype(BF16)
    head_of = np.arange(D_A) // HD_A
    bd = jnp.asarray(head_of[:MXU_DIM, None] == head_of[None, :MXU_DIM], BF16)
    ex = jnp.asarray(np.arange(LANES)[:, None] == head_of[None, :], BF16)
    bdx = jnp.asarray((np.arange(D_KB) // DK_B)[:, None] == (np.arange(D_B) // DV_B)[None, :], BF16)
    wgu = jnp.concatenate([w_gate_up, jnp.zeros((LANES - GATE_RANK, D_KB), w_gate_up.dtype)], axis=0).astype(BF16)
    return dict(nw=norm_w.reshape(1, d_model), w=w, bd=bd, ex=ex, bdx=bdx,
                wq=jnp.tile(q_norm_w, H_A).reshape(1, D_A), wk=jnp.tile(k_norm_w, H_A).reshape(1, D_A),
                wgu=wgu, bg=b_gate.reshape(1, D_KB), gw=gla_norm_w.reshape(1, DV_B), wo=w_out.astype(BF16))


def _layer(xp, xs, p, keep, k_cache, v_cache, s0):
    bsz, seq, _ = xp.shape
    bs, t_len, d_model = xs.shape
    win = k_cache.shape[1]
    rs = _proj_sample(xs.reshape(bs * t_len, d_model), PAST_LEN + jnp.tile(jnp.arange(t_len), bs), p)
    tok = lambda a: a.reshape(bs, t_len, a.shape[-1])
    by_pos = lambda a: a.transpose(0, 2, 3, 1).reshape(bs, D_A, win)

    r = _proj_prompt(xp, p, keep)
    by_class = lambda a, dil: a.reshape(bsz, dil, seq // dil, D_A)
    att = [_band_attn(*(by_class(r[f"{n}{dil}"], dil) for n in "qkv")) for _, dil in DILATED]
    o_b, s_fin, o_as = _gla(r["qb"], r["kb"], r["vb"], r["la"], jnp.zeros((bsz, H_B * DK_B, DV_B), F32),
                            p["gw"], p["bdx"],
                            decode=(tok(rs["q"]), tok(rs["k"]), tok(rs["v"]), by_pos(k_cache), by_pos(v_cache)))
    yp = _out_prompt(att, r["ga"], o_b, r["gb"], xp, p)
    window = lambda a: a.reshape(bsz, H_A, HD_A, keep).transpose(0, 3, 1, 2)

    o_bs, s_fin_s = _gla(tok(rs["qb"]), tok(rs["kb"]), tok(rs["vb"]), tok(rs["la"]),
                         s0.reshape(bs, H_B * DK_B, DV_B), p["gw"], p["bdx"])
    ys = _out_sample(o_as.reshape(bs * t_len, D_A), rs["ga"], o_bs.reshape(bs * t_len, D_B), rs["gb"],
                     xs.reshape(bs * t_len, d_model), p)
    heads = lambda a: a.reshape(bs, t_len, H_A, HD_A)
    state = lambda a, n: a.reshape(n, H_B, DK_B, DV_B)
    return (yp, ys.reshape(bs, t_len, d_model), window(r["kt"]), window(r["vt"]), state(s_fin, bsz),
            heads(rs["k"]), heads(rs["v"]), state(s_fin_s, bs))


def kernel(x_prompt, x_sample, cache_k_win, cache_v_win, state_gla, norm_w, w_in, w_gate_up, b_gate,
           q_norm_w, k_norm_w, gla_norm_w, w_out):
    seq = x_prompt.shape[1]
    assert seq % (MAX_DIL * BAND) == 0, "prompt length must tile into residue-class bands"
    assert all(w // d == BAND and MAX_DIL % d == 0 for w, d in DILATED)
    keep = min(MAX_WINDOW, seq)
    hp, hs = x_prompt, x_sample
    outs = [[] for _ in range(6)]
    for layer in range(w_in.shape[0]):
        p = _layer_params(norm_w[layer], w_in[layer], w_gate_up[layer], b_gate[layer], q_norm_w[layer],
                          k_norm_w[layer], gla_norm_w[layer], w_out[layer])
        hp, hs, kp, vp, st_p, kn, vn, st_s = _layer(hp, hs, p, keep, cache_k_win[layer], cache_v_win[layer],
                                                    state_gla[layer])
        for lst, val in zip(outs, (kp, vp, st_p.astype(state_gla.dtype), kn, vn, st_s.astype(state_gla.dtype))):
            lst.append(val)
    return (hp, hs) + tuple(jnp.stack(o) for o in outs)
```

```python
import functools

import numpy as np
import jax
import jax.numpy as jnp
from jax import lax
from jax.experimental import pallas as pl
from jax.experimental.pallas import tpu as pltpu

F32 = jnp.float32
BF16 = jnp.bfloat16

H_A, HD_A = 8, 64
D_A = H_A * HD_A
H_B, DK_B, DV_B = 4, 64, 128
D_KB = H_B * DK_B
D_B = H_B * DV_B
GATE_RANK = 16
GATE_TAU = 16.0
DILATED = ((128, 1), (512, 4), (2048, 16))
MAX_DIL = 16
MAX_WINDOW = 2048
BAND = 128
ROPE_THETA = 10000.0
EPS = 1e-6
PAST_LEN = 8192
GLA_CHUNK = 128
GLA_STEP_CHUNKS = 8
GLA_SHORT_BATCH = 8
GLA_SUB = 8
LANES = 128
MXU_DIM = 256
VMEM_LIMIT = 56 * 1024 * 1024
PROJ_TOKENS = 512
OUT_TOKENS = 1024
ATTN_QUERIES = 2048

_SECTIONS = (("q", D_A), ("k", D_A), ("v", D_A), ("qb", D_KB), ("kb", D_KB), ("vb", D_B),
             ("glr", LANES), ("ga", D_A), ("gb", D_B))
_OFF = {}
_o = 0
for _n, _w in _SECTIONS:
    _OFF[_n] = (_o, _o + _w)
    _o += _w
D_PAD = _o


def _mod2(x, n):
    assert n & (n - 1) == 0
    return jnp.bitwise_and(x, n - 1)


def _div2(x, n):
    assert n & (n - 1) == 0
    return jnp.right_shift(x, n.bit_length() - 1)


def _dot(a, b):
    return jnp.dot(a, b, preferred_element_type=F32)


def _dot_nt(a, b):
    return lax.dot_general(a, b, (((1,), (1,)), ((), ())), preferred_element_type=F32)


def _dot_tn(a, b):
    return lax.dot_general(a, b, (((0,), (0,)), ((), ())), preferred_element_type=F32)


def _proj_tile(x, cos, sin, nw, w_ref, bd, wq, wk, wgu, bg, emit):
    ms = jnp.mean(x * x, axis=-1, keepdims=True)
    h = (x * lax.rsqrt(ms + EPS) * nw).astype(BF16)
    sec = lambda name: _dot(h, w_ref[:, _OFF[name][0]:_OFF[name][1]])
    lane = lax.broadcasted_iota(jnp.int32, (x.shape[0], LANES), 1)
    first_half = _mod2(lane, HD_A) < (HD_A // 2)

    def qk_norm_rope(zz, wn):
        sq = (zz * zz).astype(BF16)
        ss = jnp.concatenate([_dot(sq[:, c:c + MXU_DIM], bd) for c in range(0, D_A, MXU_DIM)], axis=1)
        y = zz * lax.rsqrt(ss * (1.0 / HD_A) + EPS) * wn
        outs = []
        for j in range(D_A // LANES):
            yj = y[:, j * LANES:(j + 1) * LANES]
            swapped = jnp.where(first_half, pltpu.roll(yj, LANES - HD_A // 2, 1),
                                pltpu.roll(yj, HD_A // 2, 1))
            outs.append(yj * cos + swapped * sin)
        return jnp.concatenate(outs, axis=1)

    def silu_gate(name):
        gate = sec(name)
        emit(name, gate / (1.0 + jnp.exp(-gate)))

    emit("q", qk_norm_rope(sec("q"), wq) * (HD_A ** -0.5))
    emit("vb", sec("vb"))
    emit("k", qk_norm_rope(sec("k"), wk))
    emit("qb", sec("qb") * (DK_B ** -0.5))
    emit("kb", sec("kb"))
    emit("v", sec("v"))
    silu_gate("ga")
    xg = _dot(sec("glr").astype(BF16), wgu) + bg
    emit("la", (jnp.minimum(xg, 0.0) - jnp.log1p(jnp.exp(-jnp.abs(xg)))) * (1.0 / GATE_TAU))
    silu_gate("gb")


def _proj_prompt_kernel(x_ref, cos_ref, sin_ref, nw_ref, w_ref, bd_ref, wq_ref, wk_ref, wgu_ref, bg_ref,
                        q1, k1, v1, q4, k4, v4, q16, k16, v16, kt, vt, qb, kb, vb, la, ga, gb, slabs):
    tm = x_ref.shape[0]
    plain = dict(qb=qb, kb=kb, vb=vb, la=la, ga=ga, gb=gb)
    attn = dict(q=(0, (q1, q4, q16), None), k=(1, (k1, k4, k16), kt), v=(2, (v1, v4, v16), vt))

    def emit(name, val):
        if name in plain:
            plain[name][...] = val.astype(plain[name].dtype)
            return
        si, refs, win_t = attn[name]
        src = slabs.at[2 * si]
        for j in range(D_A // LANES):
            src[j] = val[:, j * LANES:(j + 1) * LANES]
        refs[0][...] = val.astype(BF16)
        for level in range(1, len(DILATED)):
            prev, dil = DILATED[level - 1][1], DILATED[level][1]
            f, n = dil // prev, tm // dil
            dst = slabs.at[2 * si + level % 2]
            for c_prev in range(prev):
                for a in range(f):
                    c = c_prev + prev * a
                    for j in range(D_A // LANES):
                        piece = src[j, pl.ds(c_prev * (tm // prev) + a, n, stride=f), :]
                        refs[level][c, :, j * LANES:(j + 1) * LANES] = piece.astype(BF16)
                        if level + 1 < len(DILATED):
                            dst[j, c * n:(c + 1) * n, :] = piece
            src = dst
        if win_t is not None:
            win_t[...] = val.T

    _proj_tile(x_ref[...], cos_ref[...], sin_ref[...], nw_ref[...], w_ref, bd_ref[...],
               wq_ref[...], wk_ref[...], wgu_ref[...], bg_ref[...], emit)


def _proj_sample_kernel(x_ref, cos_ref, sin_ref, nw_ref, w_ref, bd_ref, wq_ref, wk_ref, wgu_ref, bg_ref,
                        q, k, v, qb, kb, vb, la, ga, gb):
    refs = dict(q=q, k=k, v=v, qb=qb, kb=kb, vb=vb, la=la, ga=ga, gb=gb)

    def emit(name, val):
        refs[name][...] = val.astype(refs[name].dtype)

    _proj_tile(x_ref[...], cos_ref[...], sin_ref[...], nw_ref[...], w_ref, bd_ref[...],
               wq_ref[...], wk_ref[...], wgu_ref[...], bg_ref[...], emit)


def _rope_tables(pos):
    half = HD_A // 2
    inv_freq = ROPE_THETA ** (-jnp.arange(half, dtype=F32) / half)
    ang = pos.astype(F32)[:, None] * inv_freq[None, :]
    cos, sin = jnp.cos(ang), jnp.sin(ang)
    reps = LANES // HD_A
    cos_t = jnp.tile(jnp.concatenate([cos, cos], axis=1), (1, reps))
    sin_t = jnp.tile(jnp.concatenate([-sin, sin], axis=1), (1, reps))
    return cos_t, sin_t


def _const_spec(shape):
    return pl.BlockSpec(shape, lambda *_: (0,) * len(shape))


def _weight_args(p):
    return (p["nw"], p["w"], p["bd"], p["wq"], p["wk"], p["wgu"], p["bg"])


def _weight_specs(d_model):
    return [_const_spec((1, d_model)), _const_spec((d_model, D_PAD)), _const_spec((MXU_DIM, MXU_DIM)),
            _const_spec((1, D_A)), _const_spec((1, D_A)), _const_spec((LANES, D_KB)), _const_spec((1, D_KB))]


def _proj_prompt(x, p, keep):
    bsz, seq, d_model = x.shape
    tm = PROJ_TOKENS
    n_skip = (seq - keep) // tm
    cos_t, sin_t = _rope_tables(jnp.arange(seq))
    res = lambda dil: jax.ShapeDtypeStruct((bsz, dil, seq // dil, D_A), BF16)
    nat = lambda w, dt: jax.ShapeDtypeStruct((bsz, seq, w), dt)
    win_t = jax.ShapeDtypeStruct((bsz, D_A, keep), F32)
    res_spec = lambda dil: pl.BlockSpec((None, dil, tm // dil, D_A), lambda b, i: (b, 0, i, 0))
    nat_spec = lambda w: pl.BlockSpec((None, tm, w), lambda b, i: (b, i, 0))
    win_spec = pl.BlockSpec((None, D_A, tm), lambda b, i: (b, 0, jnp.maximum(i - n_skip, 0)))
    tab_spec = pl.BlockSpec((tm, LANES), lambda b, i: (i, 0))
    qkv_specs, qkv_shapes = [], []
    for _, dil in DILATED:
        qkv_specs += [nat_spec(D_A) if dil == 1 else res_spec(dil)] * 3
        qkv_shapes += [nat(D_A, BF16) if dil == 1 else res(dil)] * 3
    outs = pl.pallas_call(
        _proj_prompt_kernel,
        grid=(bsz, seq // tm),
        in_specs=[nat_spec(d_model), tab_spec, tab_spec] + _weight_specs(d_model),
        out_specs=qkv_specs + [win_spec] * 2
                  + [nat_spec(D_KB), nat_spec(D_KB), nat_spec(D_B), nat_spec(D_KB), nat_spec(D_A), nat_spec(D_B)],
        out_shape=qkv_shapes + [win_t] * 2
                  + [nat(D_KB, F32), nat(D_KB, F32), nat(D_B, F32), nat(D_KB, F32), nat(D_A, BF16), nat(D_B, BF16)],
        scratch_shapes=[pltpu.VMEM((6, D_A // LANES, tm, LANES), F32)],
        compiler_params=pltpu.CompilerParams(dimension_semantics=("arbitrary", "arbitrary"),
                                             vmem_limit_bytes=VMEM_LIMIT),
        name="proj_prompt",
    )(x, cos_t, sin_t, *_weight_args(p))
    names = [f"{n}{dil}" for _, dil in DILATED for n in "qkv"] + ["kt", "vt", "qb", "kb", "vb", "la", "ga", "gb"]
    return dict(zip(names, outs))


def _proj_sample(x2, pos, p):
    n, d_model = x2.shape
    tm = 256 if n % 256 == 0 else n
    cos_t, sin_t = _rope_tables(pos)
    row = lambda w: pl.BlockSpec((tm, w), lambda i: (i, 0))
    widths = (D_A, D_A, D_A, D_KB, D_KB, D_B, D_KB, D_A, D_B)
    outs = pl.pallas_call(
        _proj_sample_kernel,
        grid=(n // tm,),
        in_specs=[row(d_model), row(LANES), row(LANES)] + _weight_specs(d_model),
        out_specs=[row(w) for w in widths],
        out_shape=[jax.ShapeDtypeStruct((n, w), BF16 if i == 0 else F32) for i, w in enumerate(widths)],
        compiler_params=pltpu.CompilerParams(dimension_semantics=("arbitrary",), vmem_limit_bytes=VMEM_LIMIT),
        name="proj_sample",
    )(x2, cos_t, sin_t, *_weight_args(p))
    return dict(zip(("q", "k", "v", "qb", "kb", "vb", "la", "ga", "gb"), outs))


def _band_attn_kernel(bias_ref, q_ref, kp_ref, kc_ref, vp_ref, vc_ref, o_ref, lse_ref, kbuf, vbuf):
    tq = BAND
    kbuf[:, 0:tq] = kp_ref[...]
    kbuf[:, tq:] = kc_ref[...]
    vbuf[:, 0:tq] = vp_ref[...]
    vbuf[:, tq:] = vc_ref[...]
    table0 = jnp.where(pl.program_id(2) == 0, 0, 1)
    lane = lax.broadcasted_iota(jnp.int32, (tq, LANES), 1)
    low = lane < HD_A
    pairs = [slice(hp * LANES, (hp + 1) * LANES) for hp in range(D_A // LANES)]
    n_cls, n_blocks = q_ref.shape[0], q_ref.shape[1] // tq

    def score_phase(c, j):
        scores = []
        for cs in pairs:
            qp = q_ref[c, j * tq:(j + 1) * tq, cs]
            zero = jnp.zeros_like(qp)
            qs = jnp.concatenate([jnp.where(low, qp, zero), jnp.where(low, zero, qp)], axis=0)
            bias = bias_ref[table0] if j == 0 else bias_ref[1]
            scores.append(_dot_nt(qs, kbuf[c, j * tq:(j + 2) * tq, cs]) + bias)
        return scores

    def softmax_phase(scores):
        probs = []
        for s in scores:
            m = jnp.max(s, axis=-1, keepdims=True)
            pr = jnp.exp(s - m)
            probs.append((pr.astype(BF16), m, jnp.sum(pr, axis=-1, keepdims=True)))
        return probs

    def value_phase(c, j, probs):
        m_all = jnp.zeros((tq, LANES), F32)
        den_all = jnp.ones((tq, LANES), F32)
        for hp, (cs, (pr, m, den)) in enumerate(zip(pairs, probs)):
            pv = _dot(pr, vbuf[c, j * tq:(j + 2) * tq, cs]) / den
            o_ref[c, j * tq:(j + 1) * tq, cs] = jnp.where(low, pv[:tq], pv[tq:]).astype(BF16)
            for e in range(2):
                here = lane == 2 * hp + e
                m_all = jnp.where(here, m[e * tq:(e + 1) * tq], m_all)
                den_all = jnp.where(here, den[e * tq:(e + 1) * tq], den_all)
        lse_ref[c, j * tq:(j + 1) * tq, :] = m_all + jnp.log(den_all)

    for c in range(n_cls):
        for j in range(n_blocks):
            value_phase(c, j, softmax_phase(score_phase(c, j)))


def _band_bias():
    i = np.arange(2 * BAND)[:, None] % BAND
    j = np.arange(2 * BAND)[None, :]
    ok = (i + BAND - j >= 0) & (i + BAND - j <= BAND)
    tables = np.stack([ok & (j >= BAND), ok])
    return jnp.asarray(np.where(tables, 0.0, -np.inf), F32)


def _band_attn(q, k, v):
    bsz, ncls, length, _ = q.shape
    tq = min(ATTN_QUERIES, length)
    nc = min(ncls, ATTN_QUERIES // tq)
    ratio = tq // BAND
    cur = lambda b, c, i: (b, c, i, 0)
    prev = lambda b, c, i: (b, c, jnp.maximum(i * ratio - 1, 0), 0)
    cur_spec = lambda w: pl.BlockSpec((None, nc, tq, w), cur)
    prev_spec = pl.BlockSpec((None, nc, BAND, D_A), prev)
    return pl.pallas_call(
        _band_attn_kernel,
        grid=(bsz, ncls // nc, length // tq),
        in_specs=[_const_spec((2, 2 * BAND, 2 * BAND)), cur_spec(D_A), prev_spec, cur_spec(D_A),
                  prev_spec, cur_spec(D_A)],
        out_specs=[cur_spec(D_A), cur_spec(LANES)],
        out_shape=[jax.ShapeDtypeStruct(q.shape, BF16), jax.ShapeDtypeStruct((bsz, ncls, length, LANES), F32)],
        scratch_shapes=[pltpu.VMEM((nc, BAND + tq, D_A), BF16)] * 2,
        compiler_params=pltpu.CompilerParams(dimension_semantics=("arbitrary",) * 3, vmem_limit_bytes=VMEM_LIMIT),
        name=f"band_attn_c{ncls}",
    )(_band_bias(), q, k, k, v, v)


def _sample_attn(q, k_new, v_new, k_cache, v_cache, wc, wn):
    t = q.shape[0]
    head = _div2(lax.broadcasted_iota(jnp.int32, (t, D_A), 1), HD_A)
    zero = jnp.zeros_like(q)
    qbd = jnp.concatenate([jnp.where(head == h, q, zero) for h in range(H_A)], axis=0)
    sc = jnp.where(wc > 0, _dot(qbd, k_cache.astype(BF16)), -jnp.inf)
    sn = jnp.where(wn > 0, _dot_nt(qbd, k_new.astype(BF16)), -jnp.inf)
    m = jnp.maximum(jnp.max(sc, axis=-1, keepdims=True), jnp.max(sn, axis=-1, keepdims=True))
    pc = wc * jnp.exp(sc - m)
    pn = wn * jnp.exp(sn - m)
    den = jnp.sum(pc, axis=-1, keepdims=True) + jnp.sum(pn, axis=-1, keepdims=True)
    o = (_dot_nt(pc.astype(BF16), v_cache.astype(BF16)) + _dot(pn.astype(BF16), v_new.astype(BF16))) / den
    acc = jnp.zeros((t, D_A), F32)
    for h in range(H_A):
        acc = jnp.where(head == h, o[h * t:(h + 1) * t], acc)
    return acc


def _sample_key_weights(win, t_len):
    tt = np.arange(t_len)[:, None]
    rel_c = win + tt - np.arange(win)[None, :]
    rel_n = tt - np.arange(t_len)[None, :]
    def count(rel):
        c = np.zeros(rel.shape, np.float32)
        for window, dil in DILATED:
            c += (rel >= 0) & (rel % dil == 0) & (rel // dil <= window // dil)
        return c
    return np.tile(count(rel_c), (H_A, 1)), np.tile(count(rel_n), (H_A, 1))


def _stack_heads(x):
    low = lax.broadcasted_iota(jnp.int32, x.shape, 1) < DK_B
    zero = jnp.zeros_like(x)
    return jnp.concatenate([jnp.where(low, x, zero), jnp.where(low, zero, x)], axis=0)


def _gla_chunk(q, k, v, g, state, slabs, gw, bdx):
    c = q.shape[0]
    row = lax.broadcasted_iota(jnp.int32, (c, c), 0)
    col = lax.broadcasted_iota(jnp.int32, (c, c), 1)
    tri = (row >= col).astype(BF16)
    g_hi = g.astype(BF16)
    g_lo = (g - g_hi.astype(F32)).astype(BF16)
    b = _dot(tri, g_hi) + _dot(tri, g_lo)
    b_last = b[c - 1:c, :]
    q_in = q * jnp.exp(b)
    k_dec = k * jnp.exp(b_last - b)
    trow2 = _mod2(lax.broadcasted_iota(jnp.int32, (2 * c, c), 0), c)
    col2 = lax.broadcasted_iota(jnp.int32, (2 * c, c), 1)

    levels = []
    p_blk = c
    while p_blk > GLA_SUB:
        half = p_blk // 2
        pieces = [jnp.broadcast_to(b[s + half - 1:s + half, :], (p_blk, D_KB)) for s in range(0, c, p_blk)]
        bm = pieces[0] if len(pieces) == 1 else jnp.concatenate(pieces, axis=0)
        q_l = q * jnp.exp(jnp.minimum(b - bm, 0.0))
        k_l = k * jnp.exp(jnp.minimum(bm - b, 0.0))
        region = ((_div2(trow2, p_blk) == _div2(col2, p_blk)) & (_mod2(trow2, p_blk) >= half)
                  & (_mod2(col2, p_blk) < half))
        levels.append((q_l, k_l, region))
        p_blk = half

    n_ks = D_KB // LANES
    for j in range(n_ks):
        slabs[j] = k[:, j * LANES:(j + 1) * LANES]
        slabs[n_ks + j] = b[:, j * LANES:(j + 1) * LANES]

    def block_row(first_slab, s):
        return jnp.concatenate(
            [jnp.concatenate([jnp.broadcast_to(slabs[first_slab + j, g + s:g + s + 1, :], (GLA_SUB, LANES))
                              for g in range(0, c, GLA_SUB)], axis=0) for j in range(n_ks)], axis=1)

    sub = _mod2(lax.broadcasted_iota(jnp.int32, (c, D_KB), 0), GLA_SUB)
    terms = []
    for s in range(GLA_SUB):
        b_s = block_row(n_ks, s)
        decay = jnp.exp(b - b_s if s == 0 else jnp.where(sub >= s, b - b_s, -jnp.inf))
        terms.append(q * block_row(0, s) * decay)
    sc_d = _dot(jnp.concatenate(terms, axis=0).astype(BF16), bdx)
    key_in_block = col2 - (trow2 - _mod2(trow2, GLA_SUB))

    pair_lanes = [slice(p * LANES, (p + 1) * LANES) for p in range(H_B // 2)]
    o_pairs = [_dot_nt(_stack_heads(q_in[:, ks]).astype(BF16), state[p].astype(BF16))
               for p, ks in enumerate(pair_lanes)]
    level_scores = [[_dot_nt(_stack_heads(q_l[:, ks]).astype(BF16), k_l[:, ks].astype(BF16))
                     for q_l, k_l, _ in levels] for ks in pair_lanes]
    upds = [[_dot_tn(v[:, (2 * p + e) * DV_B:(2 * p + e + 1) * DV_B].astype(BF16), k_dec[:, ks].astype(BF16))
             for e in range(2)] for p, ks in enumerate(pair_lanes)]
    outs, new_state = [], []
    for p, ks in enumerate(pair_lanes):
        sc = jnp.zeros((2 * c, c), F32)
        for s_l, (_, _, region) in zip(level_scores[p], levels):
            sc = jnp.where(region, s_l, sc)
        for s in range(GLA_SUB):
            d_s = jnp.concatenate([sc_d[s * c:(s + 1) * c, (2 * p + e) * DV_B:(2 * p + e) * DV_B + c]
                                   for e in range(2)], axis=0)
            sc = jnp.where(key_in_block == s, d_s, sc)
        sc = sc.astype(BF16)
        for e in range(2):
            hidx = 2 * p + e
            v_h = v[:, hidx * DV_B:(hidx + 1) * DV_B].astype(BF16)
            o_h = o_pairs[p][e * c:(e + 1) * c] + _dot(sc[e * c:(e + 1) * c], v_h)
            ms = jnp.mean(o_h * o_h, axis=-1, keepdims=True)
            outs.append(o_h * lax.rsqrt(ms + EPS) * gw)
        low = lax.broadcasted_iota(jnp.int32, (DV_B, LANES), 1) < DK_B
        new_state.append(state[p] * jnp.exp(b_last[:, ks]) + jnp.where(low, upds[p][0], upds[p][1]))
    return jnp.concatenate(outs, axis=1), new_state


def _gla_kernel(*refs, chunk, n_chunks, n_decode):
    q_ref, k_ref, v_ref, g_ref, s0_ref, gw_ref, bdx_ref = refs[:7]
    n_in = 14 if n_decode else 7
    o_ref, sfin_ref = refs[n_in:n_in + 2]
    st_ref, slabs = refs[-2:]
    step = pl.program_id(1)
    n_seq = q_ref.shape[0]

    @pl.when(step == 0)
    def _():
        for i in range(n_seq):
            for p in range(H_B // 2):
                st_ref[i, p] = s0_ref[i, p * LANES:(p + 1) * LANES, :].T

    if n_decode:
        dq, dk, dv, kcache, vcache, wc, wn = refs[7:14]
        d_out = refs[n_in + 2]
        for i in range(n_decode):
            d_out[i] = _sample_attn(dq[i], dk[i], dv[i], kcache[i], vcache[i], wc[...], wn[...])

    for i in range(n_seq):
        state = [st_ref[i, p] for p in range(H_B // 2)]
        for ci in range(n_chunks):
            rs = slice(ci * chunk, (ci + 1) * chunk)
            o, state = _gla_chunk(q_ref[i, rs, :], k_ref[i, rs, :], v_ref[i, rs, :], g_ref[i, rs, :], state,
                                  slabs.at[i * n_chunks + ci], gw_ref[...], bdx_ref[...])
            o_ref[i, rs, :] = o.astype(o_ref.dtype)
        for p in range(H_B // 2):
            st_ref[i, p] = state[p]

    @pl.when(step == pl.num_programs(1) - 1)
    def _():
        for i in range(n_seq):
            for p in range(H_B // 2):
                sfin_ref[i, p * LANES:(p + 1) * LANES, :] = st_ref[i, p].T


def _gla(qb, kb, vb, la, s0, gw, bdx, decode=None):
    bsz, length, _ = qb.shape
    chunk = int(np.gcd(length, GLA_CHUNK))
    n_chunks = GLA_STEP_CHUNKS if length % (GLA_STEP_CHUNKS * chunk) == 0 else 1
    tl = chunk * n_chunks
    n_seq = GLA_SHORT_BATCH if length == tl and bsz % GLA_SHORT_BATCH == 0 else 1
    grid = (bsz // n_seq, length // tl)
    tok = lambda w: pl.BlockSpec((n_seq, tl, w), lambda b, i: (b, i, 0))
    st = pl.BlockSpec((n_seq, H_B * DK_B, DV_B), lambda b, i: (b, 0, 0))
    in_specs = [tok(D_KB), tok(D_KB), tok(D_B), tok(D_KB), st, _const_spec((1, DV_B)), _const_spec((D_KB, D_B))]
    out_specs = [tok(D_B), st]
    out_shape = [jax.ShapeDtypeStruct((bsz, length, D_B), BF16 if tl % 16 == 0 else F32),
                 jax.ShapeDtypeStruct((bsz, H_B * DK_B, DV_B), F32)]
    args = [qb, kb, vb, la, s0, gw, bdx]
    n_decode = 0
    if decode is not None:
        dq, dk, dv, kcache, vcache = decode
        n_dec, t_len, _ = dq.shape
        win = kcache.shape[2]
        assert n_dec % (grid[0] * grid[1]) == 0, "decode sequences must spread evenly over the grid steps"
        n_decode = n_dec // (grid[0] * grid[1])
        wc, wn = _sample_key_weights(win, t_len)
        by_step = lambda b, i: (b * grid[1] + i, 0, 0)
        new = pl.BlockSpec((n_decode, t_len, D_A), by_step)
        cache = pl.BlockSpec((n_decode, D_A, win), by_step)
        in_specs += [new, new, new, cache, cache, _const_spec(wc.shape), _const_spec(wn.shape)]
        out_specs.append(new)
        out_shape.append(jax.ShapeDtypeStruct((n_dec, t_len, D_A), F32))
        args += [dq, dk, dv, kcache, vcache, jnp.asarray(wc), jnp.asarray(wn)]
    return pl.pallas_call(
        functools.partial(_gla_kernel, chunk=chunk, n_chunks=n_chunks, n_decode=n_decode),
        grid=grid,
        in_specs=in_specs,
        out_specs=out_specs,
        out_shape=out_shape,
        scratch_shapes=[pltpu.VMEM((n_seq, H_B // 2, DV_B, LANES), F32),
                        pltpu.VMEM((n_seq * n_chunks, 2 * D_KB // LANES, chunk, LANES), F32)],
        compiler_params=pltpu.CompilerParams(dimension_semantics=("arbitrary", "arbitrary"),
                                             vmem_limit_bytes=VMEM_LIMIT),
        name="gla_decode_attn" if n_decode else "gla",
    )(*args)


def _mix_and_project(o_a, ga, o_b, gb, x, wo):
    mixed = jnp.concatenate([o_a * ga, o_b * gb], axis=1).astype(BF16)
    return x + _dot(mixed, wo)


def _out_prompt_kernel(*refs):
    nb = len(DILATED)
    att = refs[:2 * nb]
    ga, ob, gb, x_ref, ex_ref, wo_ref, y_ref = refs[2 * nb:2 * nb + 7]
    scratch = list(refs[2 * nb + 7:])
    tm = x_ref.shape[0]
    outs, lses = [], []
    for bi, (_, dil) in enumerate(DILATED):
        o_ref, l_ref = att[2 * bi], att[2 * bi + 1]
        if dil == 1:
            outs.append(o_ref[0].astype(F32))
            lses.append(l_ref[0])
            continue
        oslab, lslab = scratch.pop(0), scratch.pop(0)
        for c in range(dil):
            rows = pl.ds(c, tm // dil, stride=dil)
            for j in range(D_A // LANES):
                oslab[j, rows, :] = o_ref[c, :, j * LANES:(j + 1) * LANES].astype(F32)
            lslab[rows, :] = l_ref[c]
        outs.append(jnp.concatenate([oslab[j] for j in range(D_A // LANES)], axis=1))
        lses.append(lslab[...])
    m = functools.reduce(jnp.maximum, lses)
    es = [jnp.exp(l - m) for l in lses]
    inv = 1.0 / functools.reduce(lambda a, b: a + b, es)
    spread = lambda a: _dot((a * inv).astype(BF16), ex_ref[...])
    o_a = functools.reduce(lambda a, b: a + b, [spread(e) * (o - outs[-1]) for e, o in zip(es[:-1], outs[:-1])],
                           outs[-1])
    y_ref[...] = _mix_and_project(o_a, ga[...], ob[...], gb[...], x_ref[...], wo_ref[...])


def _out_sample_kernel(oa, ga, ob, gb, x_ref, wo_ref, y_ref):
    y_ref[...] = _mix_and_project(oa[...], ga[...], ob[...], gb[...], x_ref[...], wo_ref[...])


def _out_prompt(att, ga, o_b, gb, x, p):
    bsz, seq, d_model = x.shape
    tm = OUT_TOKENS
    nat_spec = lambda w: pl.BlockSpec((None, tm, w), lambda b, i: (b, i, 0))
    cls_spec = lambda dil, w: pl.BlockSpec((None, dil, tm // dil, w), lambda b, i: (b, 0, i, 0))
    att_specs, att_args, scratch = [], [], []
    for (_, dil), (o, lse) in zip(DILATED, att):
        att_specs += [cls_spec(dil, D_A), cls_spec(dil, LANES)]
        att_args += [o, lse]
        if dil > 1:
            scratch += [pltpu.VMEM((D_A // LANES, tm, LANES), F32), pltpu.VMEM((tm, LANES), F32)]
    return pl.pallas_call(
        _out_prompt_kernel,
        grid=(bsz, seq // tm),
        in_specs=att_specs + [nat_spec(D_A), nat_spec(D_B), nat_spec(D_B), nat_spec(d_model),
                              _const_spec((LANES, D_A)), _const_spec((D_A + D_B, d_model))],
        out_specs=nat_spec(d_model),
        out_shape=jax.ShapeDtypeStruct((bsz, seq, d_model), F32),
        scratch_shapes=scratch,
        compiler_params=pltpu.CompilerParams(dimension_semantics=("arbitrary", "arbitrary"),
                                             vmem_limit_bytes=VMEM_LIMIT),
        name="out_prompt",
    )(*att_args, ga, o_b, gb, x, p["ex"], p["wo"])


def _out_sample(o_a, ga, o_b, gb, x2, p):
    n, d_model = x2.shape
    tm = 256 if n % 256 == 0 else n
    row = lambda w: pl.BlockSpec((tm, w), lambda i: (i, 0))
    return pl.pallas_call(
        _out_sample_kernel,
        grid=(n // tm,),
        in_specs=[row(D_A), row(D_A), row(D_B), row(D_B), row(d_model), _const_spec((D_A + D_B, d_model))],
        out_specs=row(d_model),
        out_shape=jax.ShapeDtypeStruct((n, d_model), F32),
        compiler_params=pltpu.CompilerParams(dimension_semantics=("arbitrary",), vmem_limit_bytes=VMEM_LIMIT),
        name="out_sample",
    )(o_a, ga, o_b, gb, x2, p["wo"])


def _layer_params(norm_w, w_in, w_gate_up, b_gate, q_norm_w, k_norm_w, gla_norm_w, w_out):
    d_model = w_in.shape[0]
    split = np.cumsum((D_A, D_A, D_A, D_KB, D_KB, D_B, GATE_RANK, D_A, D_B))
    glr = w_in[:, split[5]:split[6]]
    w = jnp.concatenate([w_in[:, :split[5]], glr, jnp.zeros((d_model, LANES - GATE_RANK), w_in.dtype),
                         w_in[:, split[6]:]], axis=1).astype(BF16)
    head_of = np.arange(D_A) // HD_A
    bd = jnp.asarray(head_of[:MXU_DIM, None] == head_of[None, :MXU_DIM], BF16)
    ex = jnp.asarray(np.arange(LANES)[:, None] == head_of[None, :], BF16)
    bdx = jnp.asarray((np.arange(D_KB) // DK_B)[:, None] == (np.arange(D_B) // DV_B)[None, :], BF16)
    wgu = jnp.concatenate([w_gate_up, jnp.zeros((LANES - GATE_RANK, D_KB), w_gate_up.dtype)], axis=0).astype(BF16)
    return dict(nw=norm_w.reshape(1, d_model), w=w, bd=bd, ex=ex, bdx=bdx,
                wq=jnp.tile(q_norm_w, H_A).reshape(1, D_A), wk=jnp.tile(k_norm_w, H_A).reshape(1, D_A),
                wgu=wgu, bg=b_gate.reshape(1, D_KB), gw=gla_norm_w.reshape(1, DV_B), wo=w_out.astype(BF16))


def _layer(xp, xs, p, keep, k_cache, v_cache, s0):
    bsz, seq, _ = xp.shape
    bs, t_len, d_model = xs.shape
    win = k_cache.shape[1]
    rs = _proj_sample(xs.reshape(bs * t_len, d_model), PAST_LEN + jnp.tile(jnp.arange(t_len), bs), p)
    tok = lambda a: a.reshape(bs, t_len, a.shape[-1])
    by_pos = lambda a: a.transpose(0, 2, 3, 1).reshape(bs, D_A, win)

    r = _proj_prompt(xp, p, keep)
    by_class = lambda a, dil: a.reshape(bsz, dil, seq // dil, D_A)
    att = [_band_attn(*(by_class(r[f"{n}{dil}"], dil) for n in "qkv")) for _, dil in DILATED]
    o_b, s_fin, o_as = _gla(r["qb"], r["kb"], r["vb"], r["la"], jnp.zeros((bsz, H_B * DK_B, DV_B), F32),
                            p["gw"], p["bdx"],
                            decode=(tok(rs["q"]), tok(rs["k"]), tok(rs["v"]), by_pos(k_cache), by_pos(v_cache)))
    yp = _out_prompt(att, r["ga"], o_b, r["gb"], xp, p)
    window = lambda a: a.reshape(bsz, H_A, HD_A, keep).transpose(0, 3, 1, 2)

    o_bs, s_fin_s = _gla(tok(rs["qb"]), tok(rs["kb"]), tok(rs["vb"]), tok(rs["la"]),
                         s0.reshape(bs, H_B * DK_B, DV_B), p["gw"], p["bdx"])
    ys = _out_sample(o_as.reshape(bs * t_len, D_A), rs["ga"], o_bs.reshape(bs * t_len, D_B), rs["gb"],
                     xs.reshape(bs * t_len, d_model), p)
    heads = lambda a: a.reshape(bs, t_len, H_A, HD_A)
    state = lambda a, n: a.reshape(n, H_B, DK_B, DV_B)
    return (yp, ys.reshape(bs, t_len, d_model), window(r["kt"]), window(r["vt"]), state(s_fin, bsz),
            heads(rs["k"]), heads(rs["v"]), state(s_fin_s, bs))


def kernel(x_prompt, x_sample, cache_k_win, cache_v_win, state_gla, norm_w, w_in, w_gate_up, b_gate,
           q_norm_w, k_norm_w, gla_norm_w, w_out):
    seq = x_prompt.shape[1]
    assert seq % (MAX_DIL * BAND) == 0, "prompt length must tile into residue-class bands"
    assert all(w // d == BAND and MAX_DIL % d == 0 for w, d in DILATED)
    keep = min(MAX_WINDOW, seq)
    hp, hs = x_prompt, x_sample
    outs = [[] for _ in range(6)]
    for layer in range(w_in.shape[0]):
        p = _layer_params(norm_w[layer], w_in[layer], w_gate_up[layer], b_gate[layer], q_norm_w[layer],
                          k_norm_w[layer], gla_norm_w[layer], w_out[layer])
        hp, hs, kp, vp, st_p, kn, vn, st_s = _layer(hp, hs, p, keep, cache_k_win[layer], cache_v_win[layer],
                                                    state_gla[layer])
        for lst, val in zip(outs, (kp, vp, st_p.astype(state_gla.dtype), kn, vn, st_s.astype(state_gla.dtype))):
            lst.append(val)
    return (hp, hs) + tuple(jnp.stack(o) for o in outs)
```

```python
import functools

import numpy as np
import jax
import jax.numpy as jnp
from jax import lax
from jax.experimental import pallas as pl
from jax.experimental.pallas import tpu as pltpu

F32 = jnp.float32
BF16 = jnp.bfloat16

H_A, HD_A = 8, 64
D_A = H_A * HD_A
H_B, DK_B, DV_B = 4, 64, 128
D_KB = H_B * DK_B
D_B = H_B * DV_B
GATE_RANK = 16
GATE_TAU = 16.0
DILATED = ((128, 1), (512, 4), (2048, 16))
MAX_DIL = 16
MAX_WINDOW = 2048
BAND = 128
ROPE_THETA = 10000.0
EPS = 1e-6
PAST_LEN = 8192
GLA_CHUNK = 128
GLA_STEP_CHUNKS = 4
GLA_SHORT_BATCH = 8
GLA_SUB = 8
LANES = 128
MXU_DIM = 256
VMEM_LIMIT = 56 * 1024 * 1024
PROJ_TOKENS = 512
ATTN_QUERIES = 2048

_SECTIONS = (("q", D_A), ("k", D_A), ("v", D_A), ("qb", D_KB), ("kb", D_KB), ("vb", D_B),
             ("glr", LANES), ("ga", D_A), ("gb", D_B))
_OFF = {}
_o = 0
for _n, _w in _SECTIONS:
    _OFF[_n] = (_o, _o + _w)
    _o += _w
D_PAD = _o


def _mod2(x, n):
    assert n & (n - 1) == 0
    return jnp.bitwise_and(x, n - 1)


def _div2(x, n):
    assert n & (n - 1) == 0
    return jnp.right_shift(x, n.bit_length() - 1)


def _dot(a, b):
    return jnp.dot(a, b, preferred_element_type=F32)


def _dot_nt(a, b):
    return lax.dot_general(a, b, (((1,), (1,)), ((), ())), preferred_element_type=F32)


def _dot_tn(a, b):
    return lax.dot_general(a, b, (((0,), (0,)), ((), ())), preferred_element_type=F32)


def _proj_tile(x, cos, sin, nw, w_ref, bd, wq, wk, wgu, bg, emit):
    ms = jnp.mean(x * x, axis=-1, keepdims=True)
    h = (x * lax.rsqrt(ms + EPS) * nw).astype(BF16)
    sec = lambda name: _dot(h, w_ref[:, _OFF[name][0]:_OFF[name][1]])
    lane = lax.broadcasted_iota(jnp.int32, (x.shape[0], LANES), 1)
    first_half = _mod2(lane, HD_A) < (HD_A // 2)

    def qk_norm_rope(zz, wn):
        sq = (zz * zz).astype(BF16)
        ss = jnp.concatenate([_dot(sq[:, c:c + MXU_DIM], bd) for c in range(0, D_A, MXU_DIM)], axis=1)
        y = zz * lax.rsqrt(ss * (1.0 / HD_A) + EPS) * wn
        outs = []
        for j in range(D_A // LANES):
            yj = y[:, j * LANES:(j + 1) * LANES]
            swapped = jnp.where(first_half, pltpu.roll(yj, LANES - HD_A // 2, 1),
                                pltpu.roll(yj, HD_A // 2, 1))
            outs.append(yj * cos + swapped * sin)
        return jnp.concatenate(outs, axis=1)

    def silu_gate(name):
        gate = sec(name)
        emit(name, gate / (1.0 + jnp.exp(-gate)))

    emit("q", qk_norm_rope(sec("q"), wq) * (HD_A ** -0.5))
    emit("vb", sec("vb"))
    emit("k", qk_norm_rope(sec("k"), wk))
    emit("qb", sec("qb") * (DK_B ** -0.5))
    emit("kb", sec("kb"))
    emit("v", sec("v"))
    silu_gate("ga")
    xg = _dot(sec("glr").astype(BF16), wgu) + bg
    emit("la", (jnp.minimum(xg, 0.0) - jnp.log1p(jnp.exp(-jnp.abs(xg)))) * (1.0 / GATE_TAU))
    silu_gate("gb")


def _proj_prompt_kernel(x_ref, cos_ref, sin_ref, nw_ref, w_ref, bd_ref, wq_ref, wk_ref, wgu_ref, bg_ref,
                        q1, k1, v1, q4, k4, v4, q16, k16, v16, kt, vt, qb, kb, vb, la, ga, gb, slabs):
    tm = x_ref.shape[0]
    plain = dict(qb=qb, kb=kb, vb=vb, la=la, ga=ga, gb=gb)
    attn = dict(q=(0, (q1, q4, q16), None), k=(1, (k1, k4, k16), kt), v=(2, (v1, v4, v16), vt))

    def emit(name, val):
        if name in plain:
            plain[name][...] = val.astype(plain[name].dtype)
            return
        si, refs, win_t = attn[name]
        src = slabs.at[2 * si]
        for j in range(D_A // LANES):
            src[j] = val[:, j * LANES:(j + 1) * LANES]
        refs[0][...] = val.astype(BF16)
        for level in range(1, len(DILATED)):
            prev, dil = DILATED[level - 1][1], DILATED[level][1]
            f, n = dil // prev, tm // dil
            dst = slabs.at[2 * si + level % 2]
            for c_prev in range(prev):
                for a in range(f):
                    c = c_prev + prev * a
                    for j in range(D_A // LANES):
                        piece = src[j, pl.ds(c_prev * (tm // prev) + a, n, stride=f), :]
                        refs[level][c, :, j * LANES:(j + 1) * LANES] = piece.astype(BF16)
                        if level + 1 < len(DILATED):
                            dst[j, c * n:(c + 1) * n, :] = piece
            src = dst
        if win_t is not None:
            win_t[...] = val.T

    _proj_tile(x_ref[...], cos_ref[...], sin_ref[...], nw_ref[...], w_ref, bd_ref[...],
               wq_ref[...], wk_ref[...], wgu_ref[...], bg_ref[...], emit)


def _proj_sample_kernel(x_ref, cos_ref, sin_ref, nw_ref, w_ref, bd_ref, wq_ref, wk_ref, wgu_ref, bg_ref,
                        q, k, v, qb, kb, vb, la, ga, gb):
    refs = dict(q=q, k=k, v=v, qb=qb, kb=kb, vb=vb, la=la, ga=ga, gb=gb)

    def emit(name, val):
        refs[name][...] = val.astype(refs[name].dtype)

    _proj_tile(x_ref[...], cos_ref[...], sin_ref[...], nw_ref[...], w_ref, bd_ref[...],
               wq_ref[...], wk_ref[...], wgu_ref[...], bg_ref[...], emit)


def _rope_tables(pos):
    half = HD_A // 2
    inv_freq = ROPE_THETA ** (-jnp.arange(half, dtype=F32) / half)
    ang = pos.astype(F32)[:, None] * inv_freq[None, :]
    cos, sin = jnp.cos(ang), jnp.sin(ang)
    reps = LANES // HD_A
    cos_t = jnp.tile(jnp.concatenate([cos, cos], axis=1), (1, reps))
    sin_t = jnp.tile(jnp.concatenate([-sin, sin], axis=1), (1, reps))
    return cos_t, sin_t


def _const_spec(shape):
    return pl.BlockSpec(shape, lambda *_: (0,) * len(shape))


def _weight_args(p):
    return (p["nw"], p["w"], p["bd"], p["wq"], p["wk"], p["wgu"], p["bg"])


def _weight_specs(d_model):
    return [_const_spec((1, d_model)), _const_spec((d_model, D_PAD)), _const_spec((MXU_DIM, MXU_DIM)),
            _const_spec((1, D_A)), _const_spec((1, D_A)), _const_spec((LANES, D_KB)), _const_spec((1, D_KB))]


def _proj_prompt(x, p, keep):
    bsz, seq, d_model = x.shape
    tm = PROJ_TOKENS
    n_skip = (seq - keep) // tm
    cos_t, sin_t = _rope_tables(jnp.arange(seq))
    res = lambda dil: jax.ShapeDtypeStruct((bsz, dil, seq // dil, D_A), BF16)
    nat = lambda w, dt: jax.ShapeDtypeStruct((bsz, seq, w), dt)
    win_t = jax.ShapeDtypeStruct((bsz, D_A, keep), F32)
    res_spec = lambda dil: pl.BlockSpec((None, dil, tm // dil, D_A), lambda b, i: (b, 0, i, 0))
    nat_spec = lambda w: pl.BlockSpec((None, tm, w), lambda b, i: (b, i, 0))
    win_spec = pl.BlockSpec((None, D_A, tm), lambda b, i: (b, 0, jnp.maximum(i - n_skip, 0)))
    tab_spec = pl.BlockSpec((tm, LANES), lambda b, i: (i, 0))
    qkv_specs, qkv_shapes = [], []
    for _, dil in DILATED:
        qkv_specs += [nat_spec(D_A) if dil == 1 else res_spec(dil)] * 3
        qkv_shapes += [nat(D_A, BF16) if dil == 1 else res(dil)] * 3
    outs = pl.pallas_call(
        _proj_prompt_kernel,
        grid=(bsz, seq // tm),
        in_specs=[nat_spec(d_model), tab_spec, tab_spec] + _weight_specs(d_model),
        out_specs=qkv_specs + [win_spec] * 2
                  + [nat_spec(D_KB), nat_spec(D_KB), nat_spec(D_B), nat_spec(D_KB), nat_spec(D_A), nat_spec(D_B)],
        out_shape=qkv_shapes + [win_t] * 2
                  + [nat(D_KB, F32), nat(D_KB, F32), nat(D_B, F32), nat(D_KB, F32), nat(D_A, BF16), nat(D_B, BF16)],
        scratch_shapes=[pltpu.VMEM((6, D_A // LANES, tm, LANES), F32)],
        compiler_params=pltpu.CompilerParams(dimension_semantics=("arbitrary", "arbitrary"),
                                             vmem_limit_bytes=VMEM_LIMIT),
        name="proj_prompt",
    )(x, cos_t, sin_t, *_weight_args(p))
    names = [f"{n}{dil}" for _, dil in DILATED for n in "qkv"] + ["kt", "vt", "qb", "kb", "vb", "la", "ga", "gb"]
    return dict(zip(names, outs))


def _proj_sample(x2, pos, p):
    n, d_model = x2.shape
    tm = 256 if n % 256 == 0 else n
    cos_t, sin_t = _rope_tables(pos)
    row = lambda w: pl.BlockSpec((tm, w), lambda i: (i, 0))
    widths = (D_A, D_A, D_A, D_KB, D_KB, D_B, D_KB, D_A, D_B)
    outs = pl.pallas_call(
        _proj_sample_kernel,
        grid=(n // tm,),
        in_specs=[row(d_model), row(LANES), row(LANES)] + _weight_specs(d_model),
        out_specs=[row(w) for w in widths],
        out_shape=[jax.ShapeDtypeStruct((n, w), BF16 if i == 0 else F32) for i, w in enumerate(widths)],
        compiler_params=pltpu.CompilerParams(dimension_semantics=("arbitrary",), vmem_limit_bytes=VMEM_LIMIT),
        name="proj_sample",
    )(x2, cos_t, sin_t, *_weight_args(p))
    return dict(zip(("q", "k", "v", "qb", "kb", "vb", "la", "ga", "gb"), outs))


def _band_attn_kernel(bias_ref, q_ref, kp_ref, kc_ref, vp_ref, vc_ref, o_ref, lse_ref, kbuf, vbuf):
    tq = BAND
    kbuf[:, 0:tq] = kp_ref[...]
    kbuf[:, tq:] = kc_ref[...]
    vbuf[:, 0:tq] = vp_ref[...]
    vbuf[:, tq:] = vc_ref[...]
    table0 = jnp.where(pl.program_id(2) == 0, 0, 1)
    lane = lax.broadcasted_iota(jnp.int32, (tq, LANES), 1)
    low = lane < HD_A
    pairs = [slice(hp * LANES, (hp + 1) * LANES) for hp in range(D_A // LANES)]
    n_cls, n_blocks = q_ref.shape[0], q_ref.shape[1] // tq

    def score_phase(c, j):
        scores = []
        for cs in pairs:
            qp = q_ref[c, j * tq:(j + 1) * tq, cs]
            zero = jnp.zeros_like(qp)
            qs = jnp.concatenate([jnp.where(low, qp, zero), jnp.where(low, zero, qp)], axis=0)
            bias = bias_ref[table0] if j == 0 else bias_ref[1]
            scores.append(_dot_nt(qs, kbuf[c, j * tq:(j + 2) * tq, cs]) + bias)
        return scores

    def softmax_phase(scores):
        probs = []
        for s in scores:
            m = jnp.max(s, axis=-1, keepdims=True)
            pr = jnp.exp(s - m)
            probs.append((pr.astype(BF16), m, jnp.sum(pr, axis=-1, keepdims=True)))
        return probs

    def value_phase(c, j, probs):
        m_all = jnp.zeros((tq, LANES), F32)
        den_all = jnp.ones((tq, LANES), F32)
        for hp, (cs, (pr, m, den)) in enumerate(zip(pairs, probs)):
            pv = _dot(pr, vbuf[c, j * tq:(j + 2) * tq, cs]) / den
            o_ref[c, j * tq:(j + 1) * tq, cs] = jnp.where(low, pv[:tq], pv[tq:]).astype(BF16)
            for e in range(2):
                here = lane == 2 * hp + e
                m_all = jnp.where(here, m[e * tq:(e + 1) * tq], m_all)
                den_all = jnp.where(here, den[e * tq:(e + 1) * tq], den_all)
        lse_ref[c, j * tq:(j + 1) * tq, :] = m_all + jnp.log(den_all)

    for c in range(n_cls):
        for j in range(n_blocks):
            value_phase(c, j, softmax_phase(score_phase(c, j)))


def _band_bias():
    i = np.arange(2 * BAND)[:, None] % BAND
    j = np.arange(2 * BAND)[None, :]
    ok = (i + BAND - j >= 0) & (i + BAND - j <= BAND)
    tables = np.stack([ok & (j >= BAND), ok])
    return jnp.asarray(np.where(tables, 0.0, -np.inf), F32)


def _band_attn(q, k, v):
    bsz, ncls, length, _ = q.shape
    tq = min(ATTN_QUERIES, length)
    nc = min(ncls, ATTN_QUERIES // tq)
    ratio = tq // BAND
    cur = lambda b, c, i: (b, c, i, 0)
    prev = lambda b, c, i: (b, c, jnp.maximum(i * ratio - 1, 0), 0)
    cur_spec = lambda w: pl.BlockSpec((None, nc, tq, w), cur)
    prev_spec = pl.BlockSpec((None, nc, BAND, D_A), prev)
    return pl.pallas_call(
        _band_attn_kernel,
        grid=(bsz, ncls // nc, length // tq),
        in_specs=[_const_spec((2, 2 * BAND, 2 * BAND)), cur_spec(D_A), prev_spec, cur_spec(D_A),
                  prev_spec, cur_spec(D_A)],
        out_specs=[cur_spec(D_A), cur_spec(LANES)],
        out_shape=[jax.ShapeDtypeStruct(q.shape, BF16), jax.ShapeDtypeStruct((bsz, ncls, length, LANES), F32)],
        scratch_shapes=[pltpu.VMEM((nc, BAND + tq, D_A), BF16)] * 2,
        compiler_params=pltpu.CompilerParams(dimension_semantics=("arbitrary",) * 3, vmem_limit_bytes=VMEM_LIMIT),
        name=f"band_attn_c{ncls}",
    )(_band_bias(), q, k, k, v, v)


def _sample_attn(q, k_new, v_new, k_cache, v_cache, wc, wn):
    t = q.shape[0]
    head = _div2(lax.broadcasted_iota(jnp.int32, (t, D_A), 1), HD_A)
    zero = jnp.zeros_like(q)
    qbd = jnp.concatenate([jnp.where(head == h, q, zero) for h in range(H_A)], axis=0)
    sc = jnp.where(wc > 0, _dot(qbd, k_cache.astype(BF16)), -jnp.inf)
    sn = jnp.where(wn > 0, _dot_nt(qbd, k_new.astype(BF16)), -jnp.inf)
    m = jnp.maximum(jnp.max(sc, axis=-1, keepdims=True), jnp.max(sn, axis=-1, keepdims=True))
    pc = wc * jnp.exp(sc - m)
    pn = wn * jnp.exp(sn - m)
    den = jnp.sum(pc, axis=-1, keepdims=True) + jnp.sum(pn, axis=-1, keepdims=True)
    o = (_dot_nt(pc.astype(BF16), v_cache.astype(BF16)) + _dot(pn.astype(BF16), v_new.astype(BF16))) / den
    acc = jnp.zeros((t, D_A), F32)
    for h in range(H_A):
        acc = jnp.where(head == h, o[h * t:(h + 1) * t], acc)
    return acc


def _sample_key_weights(win, t_len):
    tt = np.arange(t_len)[:, None]
    rel_c = win + tt - np.arange(win)[None, :]
    rel_n = tt - np.arange(t_len)[None, :]
    def count(rel):
        c = np.zeros(rel.shape, np.float32)
        for window, dil in DILATED:
            c += (rel >= 0) & (rel % dil == 0) & (rel // dil <= window // dil)
        return c
    return np.tile(count(rel_c), (H_A, 1)), np.tile(count(rel_n), (H_A, 1))


def _stack_heads(x):
    low = lax.broadcasted_iota(jnp.int32, x.shape, 1) < DK_B
    zero = jnp.zeros_like(x)
    return jnp.concatenate([jnp.where(low, x, zero), jnp.where(low, zero, x)], axis=0)


def _gla_chunk(q, k, v, g, state, slabs, gw, bdx):
    c = q.shape[0]
    row = lax.broadcasted_iota(jnp.int32, (c, c), 0)
    col = lax.broadcasted_iota(jnp.int32, (c, c), 1)
    tri = (row >= col).astype(BF16)
    g_hi = g.astype(BF16)
    g_lo = (g - g_hi.astype(F32)).astype(BF16)
    b = _dot(tri, g_hi) + _dot(tri, g_lo)
    b_last = b[c - 1:c, :]
    q_in = q * jnp.exp(b)
    k_dec = k * jnp.exp(b_last - b)
    trow2 = _mod2(lax.broadcasted_iota(jnp.int32, (2 * c, c), 0), c)
    col2 = lax.broadcasted_iota(jnp.int32, (2 * c, c), 1)

    levels = []
    p_blk = c
    while p_blk > GLA_SUB:
        half = p_blk // 2
        pieces = [jnp.broadcast_to(b[s + half - 1:s + half, :], (p_blk, D_KB)) for s in range(0, c, p_blk)]
        bm = pieces[0] if len(pieces) == 1 else jnp.concatenate(pieces, axis=0)
        q_l = q * jnp.exp(jnp.minimum(b - bm, 0.0))
        k_l = k * jnp.exp(jnp.minimum(bm - b, 0.0))
        region = ((_div2(trow2, p_blk) == _div2(col2, p_blk)) & (_mod2(trow2, p_blk) >= half)
                  & (_mod2(col2, p_blk) < half))
        levels.append((q_l, k_l, region))
        p_blk = half

    n_ks = D_KB // LANES
    for j in range(n_ks):
        slabs[j] = k[:, j * LANES:(j + 1) * LANES]
        slabs[n_ks + j] = b[:, j * LANES:(j + 1) * LANES]

    def block_row(first_slab, s):
        return jnp.concatenate(
            [jnp.concatenate([jnp.broadcast_to(slabs[first_slab + j, g + s:g + s + 1, :], (GLA_SUB, LANES))
                              for g in range(0, c, GLA_SUB)], axis=0) for j in range(n_ks)], axis=1)

    sub = _mod2(lax.broadcasted_iota(jnp.int32, (c, D_KB), 0), GLA_SUB)
    terms = []
    for s in range(GLA_SUB):
        b_s = block_row(n_ks, s)
        decay = jnp.exp(b - b_s if s == 0 else jnp.where(sub >= s, b - b_s, -jnp.inf))
        terms.append(q * block_row(0, s) * decay)
    sc_d = _dot(jnp.concatenate(terms, axis=0).astype(BF16), bdx)
    key_in_block = col2 - (trow2 - _mod2(trow2, GLA_SUB))

    pair_lanes = [slice(p * LANES, (p + 1) * LANES) for p in range(H_B // 2)]
    o_pairs = [_dot_nt(_stack_heads(q_in[:, ks]).astype(BF16), state[p].astype(BF16))
               for p, ks in enumerate(pair_lanes)]
    level_scores = [[_dot_nt(_stack_heads(q_l[:, ks]).astype(BF16), k_l[:, ks].astype(BF16))
                     for q_l, k_l, _ in levels] for ks in pair_lanes]
    upds = [[_dot_tn(v[:, (2 * p + e) * DV_B:(2 * p + e + 1) * DV_B].astype(BF16), k_dec[:, ks].astype(BF16))
             for e in range(2)] for p, ks in enumerate(pair_lanes)]
    outs, new_state = [], []
    for p, ks in enumerate(pair_lanes):
        sc = jnp.zeros((2 * c, c), F32)
        for s_l, (_, _, region) in zip(level_scores[p], levels):
            sc = jnp.where(region, s_l, sc)
        for s in range(GLA_SUB):
            d_s = jnp.concatenate([sc_d[s * c:(s + 1) * c, (2 * p + e) * DV_B:(2 * p + e) * DV_B + c]
                                   for e in range(2)], axis=0)
            sc = jnp.where(key_in_block == s, d_s, sc)
        sc = sc.astype(BF16)
        for e in range(2):
            hidx = 2 * p + e
            v_h = v[:, hidx * DV_B:(hidx + 1) * DV_B].astype(BF16)
            o_h = o_pairs[p][e * c:(e + 1) * c] + _dot(sc[e * c:(e + 1) * c], v_h)
            ms = jnp.mean(o_h * o_h, axis=-1, keepdims=True)
            outs.append(o_h * lax.rsqrt(ms + EPS) * gw)
        low = lax.broadcasted_iota(jnp.int32, (DV_B, LANES), 1) < DK_B
        new_state.append(state[p] * jnp.exp(b_last[:, ks]) + jnp.where(low, upds[p][0], upds[p][1]))
    return jnp.concatenate(outs, axis=1), new_state


def _gla_kernel(*refs, chunk, n_chunks, n_decode, mix):
    q_ref, k_ref, v_ref, g_ref, s0_ref, gw_ref, bdx_ref = refs[:7]
    pos = 7
    if n_decode:
        dq, dk, dv, kcache, vcache, wc, wn = refs[pos:pos + 7]
        pos += 7
    if mix:
        n_mix = 2 * len(DILATED) + 5
        att, (ga, gb, x_ref, ex_ref, wo_ref) = refs[pos:pos + n_mix - 5], refs[pos + n_mix - 5:pos + n_mix]
        pos += n_mix
    else:
        o_ref = refs[pos]
        pos += 1
    sfin_ref = refs[pos]
    pos += 1
    if n_decode:
        d_out = refs[pos]
        pos += 1
    if mix:
        y_ref = refs[pos]
        pos += 1
    st_ref, slabs = refs[pos:pos + 2]
    if mix:
        o_ref, mix_scratch = refs[pos + 2], refs[pos + 3:]
    step = pl.program_id(1)
    n_seq = q_ref.shape[0]

    @pl.when(step == 0)
    def _():
        for i in range(n_seq):
            for p in range(H_B // 2):
                st_ref[i, p] = s0_ref[i, p * LANES:(p + 1) * LANES, :].T

    if n_decode:
        for i in range(n_decode):
            d_out[i] = _sample_attn(dq[i], dk[i], dv[i], kcache[i], vcache[i], wc[...], wn[...])

    for i in range(n_seq):
        state = [st_ref[i, p] for p in range(H_B // 2)]
        for ci in range(n_chunks):
            rs = slice(ci * chunk, (ci + 1) * chunk)
            o, state = _gla_chunk(q_ref[i, rs, :], k_ref[i, rs, :], v_ref[i, rs, :], g_ref[i, rs, :], state,
                                  slabs.at[i * n_chunks + ci], gw_ref[...], bdx_ref[...])
            o_ref[i, rs, :] = o.astype(o_ref.dtype)
        for p in range(H_B // 2):
            st_ref[i, p] = state[p]

    if mix:
        o_a = _branch_mixture(att, ex_ref, mix_scratch, x_ref.shape[0])
        y_ref[...] = _mix_and_project(o_a, ga[...], o_ref[0], gb[...], x_ref[...], wo_ref[...])

    @pl.when(step == pl.num_programs(1) - 1)
    def _():
        for i in range(n_seq):
            for p in range(H_B // 2):
                sfin_ref[i, p * LANES:(p + 1) * LANES, :] = st_ref[i, p].T


def _gla(qb, kb, vb, la, s0, gw, bdx, decode=None, mix=None):
    bsz, length, _ = qb.shape
    chunk = int(np.gcd(length, GLA_CHUNK))
    n_chunks = GLA_STEP_CHUNKS if length % (GLA_STEP_CHUNKS * chunk) == 0 else 1
    tl = chunk * n_chunks
    n_seq = GLA_SHORT_BATCH if length == tl and bsz % GLA_SHORT_BATCH == 0 else 1
    grid = (bsz // n_seq, length // tl)
    tok = lambda w: pl.BlockSpec((n_seq, tl, w), lambda b, i: (b, i, 0))
    st = pl.BlockSpec((n_seq, H_B * DK_B, DV_B), lambda b, i: (b, 0, 0))
    in_specs = [tok(D_KB), tok(D_KB), tok(D_B), tok(D_KB), st, _const_spec((1, DV_B)), _const_spec((D_KB, D_B))]
    out_specs = [tok(D_B), st]
    out_shape = [jax.ShapeDtypeStruct((bsz, length, D_B), BF16 if tl % 16 == 0 else F32),
                 jax.ShapeDtypeStruct((bsz, H_B * DK_B, DV_B), F32)]
    args = [qb, kb, vb, la, s0, gw, bdx]
    n_decode = 0
    if decode is not None:
        dq, dk, dv, kcache, vcache = decode
        n_dec, t_len, _ = dq.shape
        win = kcache.shape[2]
        assert n_dec % (grid[0] * grid[1]) == 0, "decode sequences must spread evenly over the grid steps"
        n_decode = n_dec // (grid[0] * grid[1])
        wc, wn = _sample_key_weights(win, t_len)
        by_step = lambda b, i: (b * grid[1] + i, 0, 0)
        new = pl.BlockSpec((n_decode, t_len, D_A), by_step)
        cache = pl.BlockSpec((n_decode, D_A, win), by_step)
        in_specs += [new, new, new, cache, cache, _const_spec(wc.shape), _const_spec(wn.shape)]
        out_specs.append(new)
        out_shape.append(jax.ShapeDtypeStruct((n_dec, t_len, D_A), F32))
        args += [dq, dk, dv, kcache, vcache, jnp.asarray(wc), jnp.asarray(wn)]
    scratch = [pltpu.VMEM((n_seq, H_B // 2, DV_B, LANES), F32),
               pltpu.VMEM((n_seq * n_chunks, 2 * D_KB // LANES, chunk, LANES), F32)]
    if mix is not None:
        att, ga, gb, x, ex, wo = mix
        assert n_seq == 1
        d_model = x.shape[-1]
        nat_spec = lambda w: pl.BlockSpec((None, tl, w), lambda b, i: (b, i, 0))
        cls_spec = lambda dil, w: pl.BlockSpec((None, dil, tl // dil, w), lambda b, i: (b, 0, i, 0))
        mix_scratch = []
        for (_, dil), (o, lse) in zip(DILATED, att):
            in_specs += [cls_spec(dil, D_A), cls_spec(dil, LANES)]
            args += [o, lse]
            if dil > 1:
                mix_scratch += [pltpu.VMEM((D_A // LANES, tl, LANES), F32), pltpu.VMEM((tl, LANES), F32)]
        in_specs += [nat_spec(D_A), nat_spec(D_B), nat_spec(d_model), _const_spec(ex.shape), _const_spec(wo.shape)]
        args += [ga, gb, x, ex, wo]
        scratch += [pltpu.VMEM((n_seq, tl, D_B), BF16)] + mix_scratch
        out_specs = out_specs[1:] + [nat_spec(d_model)]
        out_shape = out_shape[1:] + [jax.ShapeDtypeStruct(x.shape, F32)]
    return pl.pallas_call(
        functools.partial(_gla_kernel, chunk=chunk, n_chunks=n_chunks, n_decode=n_decode, mix=mix is not None),
        grid=grid,
        in_specs=in_specs,
        out_specs=out_specs,
        out_shape=out_shape,
        scratch_shapes=scratch,
        compiler_params=pltpu.CompilerParams(dimension_semantics=("arbitrary", "arbitrary"),
                                             vmem_limit_bytes=VMEM_LIMIT),
        name="gla" + ("_decode_attn" if n_decode else "") + ("_out" if mix is not None else ""),
    )(*args)


def _mix_and_project(o_a, ga, o_b, gb, x, wo):
    mixed = jnp.concatenate([o_a * ga, o_b * gb], axis=1).astype(BF16)
    return x + _dot(mixed, wo)


def _branch_mixture(att, ex_ref, scratch, tm):
    scratch = list(scratch)
    outs, lses = [], []
    for bi, (_, dil) in enumerate(DILATED):
        o_ref, l_ref = att[2 * bi], att[2 * bi + 1]
        if dil == 1:
            outs.append(o_ref[0].astype(F32))
            lses.append(l_ref[0])
            continue
        oslab, lslab = scratch.pop(0), scratch.pop(0)
        for c in range(dil):
            rows = pl.ds(c, tm // dil, stride=dil)
            for j in range(D_A // LANES):
                oslab[j, rows, :] = o_ref[c, :, j * LANES:(j + 1) * LANES].astype(F32)
            lslab[rows, :] = l_ref[c]
        outs.append(jnp.concatenate([oslab[j] for j in range(D_A // LANES)], axis=1))
        lses.append(lslab[...])
    m = functools.reduce(jnp.maximum, lses)
    es = [jnp.exp(l - m) for l in lses]
    inv = 1.0 / functools.reduce(lambda a, b: a + b, es)
    spread = lambda a: _dot((a * inv).astype(BF16), ex_ref[...])
    return functools.reduce(lambda a, b: a + b, [spread(e) * (o - outs[-1]) for e, o in zip(es[:-1], outs[:-1])],
                            outs[-1])


def _out_sample_kernel(oa, ga, ob, gb, x_ref, wo_ref, y_ref):
    y_ref[...] = _mix_and_project(oa[...], ga[...], ob[...], gb[...], x_ref[...], wo_ref[...])


def _out_sample(o_a, ga, o_b, gb, x2, p):
    n, d_model = x2.shape
    tm = 256 if n % 256 == 0 else n
    row = lambda w: pl.BlockSpec((tm, w), lambda i: (i, 0))
    return pl.pallas_call(
        _out_sample_kernel,
        grid=(n // tm,),
        in_specs=[row(D_A), row(D_A), row(D_B), row(D_B), row(d_model), _const_spec((D_A + D_B, d_model))],
        out_specs=row(d_model),
        out_shape=jax.ShapeDtypeStruct((n, d_model), F32),
        compiler_params=pltpu.CompilerParams(dimension_semantics=("arbitrary",), vmem_limit_bytes=VMEM_LIMIT),
        name="out_sample",
    )(o_a, ga, o_b, gb, x2, p["wo"])


def _layer_params(norm_w, w_in, w_gate_up, b_gate, q_norm_w, k_norm_w, gla_norm_w, w_out):
    d_model = w_in.shape[0]
    split = np.cumsum((D_A, D_A, D_A, D_KB, D_KB, D_B, GATE_RANK, D_A, D_B))
    glr = w_in[:, split[5]:split[6]]
    w = jnp.concatenate([w_in[:, :split[5]], glr, jnp.zeros((d_model, LANES - GATE_RANK), w_in.dtype),
                         w_in[:, split[6]:]], axis=1).astype(BF16)
    head_of = np.arange(D_A) // HD_A
    bd = jnp.asarray(head_of[:MXU_DIM, None] == head_of[None, :MXU_DIM], BF16)
    ex = jnp.asarray(np.arange(LANES)[:, None] == head_of[None, :], BF16)
    bdx = jnp.asarray((np.arange(D_KB) // DK_B)[:, None] == (np.arange(D_B) // DV_B)[None, :], BF16)
    wgu = jnp.concatenate([w_gate_up, jnp.zeros((LANES - GATE_RANK, D_KB), w_gate_up.dtype)], axis=0).astype(BF16)
    return dict(nw=norm_w.reshape(1, d_model), w=w, bd=bd, ex=ex, bdx=bdx,
                wq=jnp.tile(q_norm_w, H_A).reshape(1, D_A), wk=jnp.tile(k_norm_w, H_A).reshape(1, D_A),
                wgu=wgu, bg=b_gate.reshape(1, D_KB), gw=gla_norm_w.reshape(1, DV_B), wo=w_out.astype(BF16))


def _layer(xp, xs, p, keep, k_cache, v_cache, s0):
    bsz, seq, _ = xp.shape
    bs, t_len, d_model = xs.shape
    win = k_cache.shape[1]
    rs = _proj_sample(xs.reshape(bs * t_len, d_model), PAST_LEN + jnp.tile(jnp.arange(t_len), bs), p)
    tok = lambda a: a.reshape(bs, t_len, a.shape[-1])
    by_pos = lambda a: a.transpose(0, 2, 3, 1).reshape(bs, D_A, win)

    r = _proj_prompt(xp, p, keep)
    by_class = lambda a, dil: a.reshape(bsz, dil, seq // dil, D_A)
    att = [_band_attn(*(by_class(r[f"{n}{dil}"], dil) for n in "qkv")) for _, dil in DILATED]
    s_fin, o_as, yp = _gla(r["qb"], r["kb"], r["vb"], r["la"], jnp.zeros((bsz, H_B * DK_B, DV_B), F32),
                           p["gw"], p["bdx"],
                           decode=(tok(rs["q"]), tok(rs["k"]), tok(rs["v"]), by_pos(k_cache), by_pos(v_cache)),
                           mix=(att, r["ga"], r["gb"], xp, p["ex"], p["wo"]))
    window = lambda a: a.reshape(bsz, H_A, HD_A, keep).transpose(0, 3, 1, 2)

    o_bs, s_fin_s = _gla(tok(rs["qb"]), tok(rs["kb"]), tok(rs["vb"]), tok(rs["la"]),
                         s0.reshape(bs, H_B * DK_B, DV_B), p["gw"], p["bdx"])
    ys = _out_sample(o_as.reshape(bs * t_len, D_A), rs["ga"], o_bs.reshape(bs * t_len, D_B), rs["gb"],
                     xs.reshape(bs * t_len, d_model), p)
    heads = lambda a: a.reshape(bs, t_len, H_A, HD_A)
    state = lambda a, n: a.reshape(n, H_B, DK_B, DV_B)
    return (yp, ys.reshape(bs, t_len, d_model), window(r["kt"]), window(r["vt"]), state(s_fin, bsz),
            heads(rs["k"]), heads(rs["v"]), state(s_fin_s, bs))


def kernel(x_prompt, x_sample, cache_k_win, cache_v_win, state_gla, norm_w, w_in, w_gate_up, b_gate,
           q_norm_w, k_norm_w, gla_norm_w, w_out):
    seq = x_prompt.shape[1]
    assert seq % (MAX_DIL * BAND) == 0, "prompt length must tile into residue-class bands"
    assert all(w // d == BAND and MAX_DIL % d == 0 for w, d in DILATED)
    keep = min(MAX_WINDOW, seq)
    hp, hs = x_prompt, x_sample
    outs = [[] for _ in range(6)]
    for layer in range(w_in.shape[0]):
        p = _layer_params(norm_w[layer], w_in[layer], w_gate_up[layer], b_gate[layer], q_norm_w[layer],
                          k_norm_w[layer], gla_norm_w[layer], w_out[layer])
        hp, hs, kp, vp, st_p, kn, vn, st_s = _layer(hp, hs, p, keep, cache_k_win[layer], cache_v_win[layer],
                                                    state_gla[layer])
        for lst, val in zip(outs, (kp, vp, st_p.astype(state_gla.dtype), kn, vn, st_s.astype(state_gla.dtype))):
            lst.append(val)
    return (hp, hs) + tuple(jnp.stack(o) for o in outs)
```

```python
import functools

import numpy as np
import jax
import jax.numpy as jnp
from jax import lax
from jax.experimental import pallas as pl
from jax.experimental.pallas import tpu as pltpu

F32 = jnp.float32
BF16 = jnp.bfloat16

H_A, HD_A = 8, 64
D_A = H_A * HD_A
H_B, DK_B, DV_B = 4, 64, 128
D_KB = H_B * DK_B
D_B = H_B * DV_B
GATE_RANK = 16
GATE_TAU = 16.0
DILATED = ((128, 1), (512, 4), (2048, 16))
MAX_DIL = 16
MAX_WINDOW = 2048
BAND = 128
ROPE_THETA = 10000.0
EPS = 1e-6
PAST_LEN = 8192
GLA_CHUNK = 128
GLA_STEP_CHUNKS = 4
GLA_SHORT_BATCH = 8
GLA_SUB = 8
LANES = 128
MXU_DIM = 256
VMEM_LIMIT = 56 * 1024 * 1024
PROJ_TOKENS = 512
ATTN_QUERIES = 2048

_SECTIONS = (("q", D_A), ("k", D_A), ("v", D_A), ("qb", D_KB), ("kb", D_KB), ("vb", D_B),
             ("glr", LANES), ("ga", D_A), ("gb", D_B))
_OFF = {}
_o = 0
for _n, _w in _SECTIONS:
    _OFF[_n] = (_o, _o + _w)
    _o += _w
D_PAD = _o


def _mod2(x, n):
    assert n & (n - 1) == 0
    return jnp.bitwise_and(x, n - 1)


def _div2(x, n):
    assert n & (n - 1) == 0
    return jnp.right_shift(x, n.bit_length() - 1)


def _dot(a, b):
    return jnp.dot(a, b, preferred_element_type=F32)


def _dot_nt(a, b):
    return lax.dot_general(a, b, (((1,), (1,)), ((), ())), preferred_element_type=F32)


def _dot_tn(a, b):
    return lax.dot_general(a, b, (((0,), (0,)), ((), ())), preferred_element_type=F32)


def _proj_tile(x, cos, sin, nw, w_ref, bd, wq, wk, wgu, bg, emit):
    ms = jnp.mean(x * x, axis=-1, keepdims=True)
    h = (x * lax.rsqrt(ms + EPS) * nw).astype(BF16)
    sec = lambda name: _dot(h, w_ref[:, _OFF[name][0]:_OFF[name][1]])
    lane = lax.broadcasted_iota(jnp.int32, (x.shape[0], LANES), 1)
    first_half = _mod2(lane, HD_A) < (HD_A // 2)

    def qk_norm_rope(zz, wn):
        sq = (zz * zz).astype(BF16)
        ss = jnp.concatenate([_dot(sq[:, c:c + MXU_DIM], bd) for c in range(0, D_A, MXU_DIM)], axis=1)
        y = zz * lax.rsqrt(ss * (1.0 / HD_A) + EPS) * wn
        outs = []
        for j in range(D_A // LANES):
            yj = y[:, j * LANES:(j + 1) * LANES]
            swapped = jnp.where(first_half, pltpu.roll(yj, LANES - HD_A // 2, 1),
                                pltpu.roll(yj, HD_A // 2, 1))
            outs.append(yj * cos + swapped * sin)
        return jnp.concatenate(outs, axis=1)

    def silu_gate(name):
        gate = sec(name)
        emit(name, gate / (1.0 + jnp.exp(-gate)))

    emit("q", qk_norm_rope(sec("q"), wq) * (HD_A ** -0.5))
    emit("vb", sec("vb"))
    emit("k", qk_norm_rope(sec("k"), wk))
    emit("qb", sec("qb") * (DK_B ** -0.5))
    emit("kb", sec("kb"))
    emit("v", sec("v"))
    silu_gate("ga")
    xg = _dot(sec("glr").astype(BF16), wgu) + bg
    emit("la", (jnp.minimum(xg, 0.0) - jnp.log1p(jnp.exp(-jnp.abs(xg)))) * (1.0 / GATE_TAU))
    silu_gate("gb")


def _proj_prompt_kernel(x_ref, cos_ref, sin_ref, nw_ref, w_ref, bd_ref, wq_ref, wk_ref, wgu_ref, bg_ref,
                        q1, k1, v1, q4, k4, v4, q16, k16, v16, kt, vt, qb, kb, vb, la, ga, gb, slabs):
    tm = x_ref.shape[0]
    plain = dict(qb=qb, kb=kb, vb=vb, la=la, ga=ga, gb=gb)
    attn = dict(q=(0, (q1, q4, q16), None), k=(1, (k1, k4, k16), kt), v=(2, (v1, v4, v16), vt))

    def emit(name, val):
        if name in plain:
            plain[name][...] = val.astype(plain[name].dtype)
            return
        si, refs, win_t = attn[name]
        src = slabs.at[2 * si]
        for j in range(D_A // LANES):
            src[j] = val[:, j * LANES:(j + 1) * LANES]
        refs[0][...] = val.astype(BF16)
        for level in range(1, len(DILATED)):
            prev, dil = DILATED[level - 1][1], DILATED[level][1]
            f, n = dil // prev, tm // dil
            dst = slabs.at[2 * si + level % 2]
            for c_prev in range(prev):
                for a in range(f):
                    c = c_prev + prev * a
                    for j in range(D_A // LANES):
                        piece = src[j, pl.ds(c_prev * (tm // prev) + a, n, stride=f), :]
                        refs[level][c, :, j * LANES:(j + 1) * LANES] = piece.astype(BF16)
                        if level + 1 < len(DILATED):
                            dst[j, c * n:(c + 1) * n, :] = piece
            src = dst
        if win_t is not None:
            win_t[...] = val.T

    _proj_tile(x_ref[...], cos_ref[...], sin_ref[...], nw_ref[...], w_ref, bd_ref[...],
               wq_ref[...], wk_ref[...], wgu_ref[...], bg_ref[...], emit)


def _proj_sample_kernel(x_ref, cos_ref, sin_ref, nw_ref, w_ref, bd_ref, wq_ref, wk_ref, wgu_ref, bg_ref,
                        q, k, v, qb, kb, vb, la, ga, gb):
    refs = dict(q=q, k=k, v=v, qb=qb, kb=kb, vb=vb, la=la, ga=ga, gb=gb)

    def emit(name, val):
        refs[name][...] = val.astype(refs[name].dtype)

    _proj_tile(x_ref[...], cos_ref[...], sin_ref[...], nw_ref[...], w_ref, bd_ref[...],
               wq_ref[...], wk_ref[...], wgu_ref[...], bg_ref[...], emit)


def _rope_tables(pos):
    half = HD_A // 2
    inv_freq = ROPE_THETA ** (-jnp.arange(half, dtype=F32) / half)
    ang = pos.astype(F32)[:, None] * inv_freq[None, :]
    cos, sin = jnp.cos(ang), jnp.sin(ang)
    reps = LANES // HD_A
    cos_t = jnp.tile(jnp.concatenate([cos, cos], axis=1), (1, reps))
    sin_t = jnp.tile(jnp.concatenate([-sin, sin], axis=1), (1, reps))
    return cos_t, sin_t


def _const_spec(shape):
    return pl.BlockSpec(shape, lambda *_: (0,) * len(shape))


def _weight_args(p):
    return (p["nw"], p["w"], p["bd"], p["wq"], p["wk"], p["wgu"], p["bg"])


def _weight_specs(d_model):
    return [_const_spec((1, d_model)), _const_spec((d_model, D_PAD)), _const_spec((MXU_DIM, MXU_DIM)),
            _const_spec((1, D_A)), _const_spec((1, D_A)), _const_spec((LANES, D_KB)), _const_spec((1, D_KB))]


def _proj_prompt(x, p, keep):
    bsz, seq, d_model = x.shape
    tm = PROJ_TOKENS
    n_skip = (seq - keep) // tm
    cos_t, sin_t = _rope_tables(jnp.arange(seq))
    res = lambda dil: jax.ShapeDtypeStruct((bsz, dil, seq // dil, D_A), BF16)
    nat = lambda w, dt: jax.ShapeDtypeStruct((bsz, seq, w), dt)
    win_t = jax.ShapeDtypeStruct((bsz, D_A, keep), F32)
    res_spec = lambda dil: pl.BlockSpec((None, dil, tm // dil, D_A), lambda b, i: (b, 0, i, 0))
    nat_spec = lambda w: pl.BlockSpec((None, tm, w), lambda b, i: (b, i, 0))
    win_spec = pl.BlockSpec((None, D_A, tm), lambda b, i: (b, 0, jnp.maximum(i - n_skip, 0)))
    tab_spec = pl.BlockSpec((tm, LANES), lambda b, i: (i, 0))
    qkv_specs, qkv_shapes = [], []
    for _, dil in DILATED:
        qkv_specs += [nat_spec(D_A) if dil == 1 else res_spec(dil)] * 3
        qkv_shapes += [nat(D_A, BF16) if dil == 1 else res(dil)] * 3
    outs = pl.pallas_call(
        _proj_prompt_kernel,
        grid=(bsz, seq // tm),
        in_specs=[nat_spec(d_model), tab_spec, tab_spec] + _weight_specs(d_model),
        out_specs=qkv_specs + [win_spec] * 2
                  + [nat_spec(D_KB), nat_spec(D_KB), nat_spec(D_B), nat_spec(D_KB), nat_spec(D_A), nat_spec(D_B)],
        out_shape=qkv_shapes + [win_t] * 2
                  + [nat(D_KB, F32), nat(D_KB, F32), nat(D_B, BF16), nat(D_KB, F32), nat(D_A, BF16), nat(D_B, BF16)],
        scratch_shapes=[pltpu.VMEM((6, D_A // LANES, tm, LANES), F32)],
        compiler_params=pltpu.CompilerParams(dimension_semantics=("arbitrary", "arbitrary"),
                                             vmem_limit_bytes=VMEM_LIMIT),
        name="proj_prompt",
    )(x, cos_t, sin_t, *_weight_args(p))
    names = [f"{n}{dil}" for _, dil in DILATED for n in "qkv"] + ["kt", "vt", "qb", "kb", "vb", "la", "ga", "gb"]
    return dict(zip(names, outs))


def _proj_sample(x2, pos, p):
    n, d_model = x2.shape
    tm = 256 if n % 256 == 0 else n
    cos_t, sin_t = _rope_tables(pos)
    row = lambda w: pl.BlockSpec((tm, w), lambda i: (i, 0))
    widths = (D_A, D_A, D_A, D_KB, D_KB, D_B, D_KB, D_A, D_B)
    outs = pl.pallas_call(
        _proj_sample_kernel,
        grid=(n // tm,),
        in_specs=[row(d_model), row(LANES), row(LANES)] + _weight_specs(d_model),
        out_specs=[row(w) for w in widths],
        out_shape=[jax.ShapeDtypeStruct((n, w), BF16 if i == 0 else F32) for i, w in enumerate(widths)],
        compiler_params=pltpu.CompilerParams(dimension_semantics=("arbitrary",), vmem_limit_bytes=VMEM_LIMIT),
        name="proj_sample",
    )(x2, cos_t, sin_t, *_weight_args(p))
    return dict(zip(("q", "k", "v", "qb", "kb", "vb", "la", "ga", "gb"), outs))


def _band_attn_kernel(bias_ref, q_ref, kp_ref, kc_ref, vp_ref, vc_ref, o_ref, lse_ref, kbuf, vbuf):
    tq = BAND
    kbuf[:, 0:tq] = kp_ref[...]
    kbuf[:, tq:] = kc_ref[...]
    vbuf[:, 0:tq] = vp_ref[...]
    vbuf[:, tq:] = vc_ref[...]
    table0 = jnp.where(pl.program_id(2) == 0, 0, 1)
    lane = lax.broadcasted_iota(jnp.int32, (tq, LANES), 1)
    low = lane < HD_A
    pairs = [slice(hp * LANES, (hp + 1) * LANES) for hp in range(D_A // LANES)]
    n_cls, n_blocks = q_ref.shape[0], q_ref.shape[1] // tq

    def score_phase(c, j):
        scores = []
        for cs in pairs:
            qp = q_ref[c, j * tq:(j + 1) * tq, cs]
            zero = jnp.zeros_like(qp)
            qs = jnp.concatenate([jnp.where(low, qp, zero), jnp.where(low, zero, qp)], axis=0)
            bias = bias_ref[table0] if j == 0 else bias_ref[1]
            scores.append(_dot_nt(qs, kbuf[c, j * tq:(j + 2) * tq, cs]) + bias)
        return scores

    def softmax_phase(scores):
        probs = []
        for s in scores:
            m = jnp.max(s, axis=-1, keepdims=True)
            pr = jnp.exp(s - m)
            probs.append((pr.astype(BF16), m, jnp.sum(pr, axis=-1, keepdims=True)))
        return probs

    def value_phase(c, j, probs):
        m_all = jnp.zeros((tq, LANES), F32)
        den_all = jnp.ones((tq, LANES), F32)
        for hp, (cs, (pr, m, den)) in enumerate(zip(pairs, probs)):
            pv = _dot(pr, vbuf[c, j * tq:(j + 2) * tq, cs]) / den
            o_ref[c, j * tq:(j + 1) * tq, cs] = jnp.where(low, pv[:tq], pv[tq:]).astype(BF16)
            for e in range(2):
                here = lane == 2 * hp + e
                m_all = jnp.where(here, m[e * tq:(e + 1) * tq], m_all)
                den_all = jnp.where(here, den[e * tq:(e + 1) * tq], den_all)
        lse_ref[c, j * tq:(j + 1) * tq, :] = m_all + jnp.log(den_all)

    for c in range(n_cls):
        for j in range(n_blocks):
            value_phase(c, j, softmax_phase(score_phase(c, j)))


def _band_bias():
    i = np.arange(2 * BAND)[:, None] % BAND
    j = np.arange(2 * BAND)[None, :]
    ok = (i + BAND - j >= 0) & (i + BAND - j <= BAND)
    tables = np.stack([ok & (j >= BAND), ok])
    return jnp.asarray(np.where(tables, 0.0, -np.inf), F32)


def _band_attn(q, k, v):
    bsz, ncls, length, _ = q.shape
    tq = min(ATTN_QUERIES, length)
    nc = min(ncls, ATTN_QUERIES // tq)
    ratio = tq // BAND
    cur = lambda b, c, i: (b, c, i, 0)
    prev = lambda b, c, i: (b, c, jnp.maximum(i * ratio - 1, 0), 0)
    cur_spec = lambda w: pl.BlockSpec((None, nc, tq, w), cur)
    prev_spec = pl.BlockSpec((None, nc, BAND, D_A), prev)
    return pl.pallas_call(
        _band_attn_kernel,
        grid=(bsz, ncls // nc, length // tq),
        in_specs=[_const_spec((2, 2 * BAND, 2 * BAND)), cur_spec(D_A), prev_spec, cur_spec(D_A),
                  prev_spec, cur_spec(D_A)],
        out_specs=[cur_spec(D_A), cur_spec(LANES)],
        out_shape=[jax.ShapeDtypeStruct(q.shape, BF16), jax.ShapeDtypeStruct((bsz, ncls, length, LANES), F32)],
        scratch_shapes=[pltpu.VMEM((nc, BAND + tq, D_A), BF16)] * 2,
        compiler_params=pltpu.CompilerParams(dimension_semantics=("arbitrary",) * 3, vmem_limit_bytes=VMEM_LIMIT),
        name=f"band_attn_c{ncls}",
    )(_band_bias(), q, k, k, v, v)


def _sample_attn(q, k_new, v_new, k_cache, v_cache, wc, wn):
    t = q.shape[0]
    head = _div2(lax.broadcasted_iota(jnp.int32, (t, D_A), 1), HD_A)
    zero = jnp.zeros_like(q)
    qbd = jnp.concatenate([jnp.where(head == h, q, zero) for h in range(H_A)], axis=0)
    sc = jnp.where(wc > 0, _dot(qbd, k_cache.astype(BF16)), -jnp.inf)
    sn = jnp.where(wn > 0, _dot_nt(qbd, k_new.astype(BF16)), -jnp.inf)
    m = jnp.maximum(jnp.max(sc, axis=-1, keepdims=True), jnp.max(sn, axis=-1, keepdims=True))
    pc = wc * jnp.exp(sc - m)
    pn = wn * jnp.exp(sn - m)
    den = jnp.sum(pc, axis=-1, keepdims=True) + jnp.sum(pn, axis=-1, keepdims=True)
    o = (_dot_nt(pc.astype(BF16), v_cache.astype(BF16)) + _dot(pn.astype(BF16), v_new.astype(BF16))) / den
    acc = jnp.zeros((t, D_A), F32)
    for h in range(H_A):
        acc = jnp.where(head == h, o[h * t:(h + 1) * t], acc)
    return acc


def _sample_key_weights(win, t_len):
    tt = np.arange(t_len)[:, None]
    rel_c = win + tt - np.arange(win)[None, :]
    rel_n = tt - np.arange(t_len)[None, :]
    def count(rel):
        c = np.zeros(rel.shape, np.float32)
        for window, dil in DILATED:
            c += (rel >= 0) & (rel % dil == 0) & (rel // dil <= window // dil)
        return c
    return np.tile(count(rel_c), (H_A, 1)), np.tile(count(rel_n), (H_A, 1))


def _stack_heads(x):
    low = lax.broadcasted_iota(jnp.int32, x.shape, 1) < DK_B
    zero = jnp.zeros_like(x)
    return jnp.concatenate([jnp.where(low, x, zero), jnp.where(low, zero, x)], axis=0)


def _gla_chunk(q, k, v, g, state, slabs, gw, bdx):
    c = q.shape[0]
    row = lax.broadcasted_iota(jnp.int32, (c, c), 0)
    col = lax.broadcasted_iota(jnp.int32, (c, c), 1)
    tri = (row >= col).astype(BF16)
    g_hi = g.astype(BF16)
    g_lo = (g - g_hi.astype(F32)).astype(BF16)
    b = _dot(tri, g_hi) + _dot(tri, g_lo)
    b_last = b[c - 1:c, :]
    q_in = q * jnp.exp(b)
    k_dec = k * jnp.exp(b_last - b)
    trow2 = _mod2(lax.broadcasted_iota(jnp.int32, (2 * c, c), 0), c)
    col2 = lax.broadcasted_iota(jnp.int32, (2 * c, c), 1)

    levels = []
    p_blk = c
    while p_blk > GLA_SUB:
        half = p_blk // 2
        pieces = [jnp.broadcast_to(b[s + half - 1:s + half, :], (p_blk, D_KB)) for s in range(0, c, p_blk)]
        bm = pieces[0] if len(pieces) == 1 else jnp.concatenate(pieces, axis=0)
        q_l = q * jnp.exp(jnp.minimum(b - bm, 0.0))
        k_l = k * jnp.exp(jnp.minimum(bm - b, 0.0))
        region = ((_div2(trow2, p_blk) == _div2(col2, p_blk)) & (_mod2(trow2, p_blk) >= half)
                  & (_mod2(col2, p_blk) < half))
        levels.append((q_l, k_l, region))
        p_blk = half

    n_ks = D_KB // LANES
    for j in range(n_ks):
        slabs[j] = k[:, j * LANES:(j + 1) * LANES]
        slabs[n_ks + j] = b[:, j * LANES:(j + 1) * LANES]

    def block_row(first_slab, s):
        return jnp.concatenate(
            [jnp.concatenate([jnp.broadcast_to(slabs[first_slab + j, g + s:g + s + 1, :], (GLA_SUB, LANES))
                              for g in range(0, c, GLA_SUB)], axis=0) for j in range(n_ks)], axis=1)

    sub = _mod2(lax.broadcasted_iota(jnp.int32, (c, D_KB), 0), GLA_SUB)
    terms = []
    for s in range(GLA_SUB):
        b_s = block_row(n_ks, s)
        decay = jnp.exp(b - b_s if s == 0 else jnp.where(sub >= s, b - b_s, -jnp.inf))
        terms.append(q * block_row(0, s) * decay)
    sc_d = _dot(jnp.concatenate(terms, axis=0).astype(BF16), bdx)
    key_in_block = col2 - (trow2 - _mod2(trow2, GLA_SUB))

    pair_lanes = [slice(p * LANES, (p + 1) * LANES) for p in range(H_B // 2)]
    o_pairs = [_dot_nt(_stack_heads(q_in[:, ks]).astype(BF16), state[p].astype(BF16))
               for p, ks in enumerate(pair_lanes)]
    level_scores = [[_dot_nt(_stack_heads(q_l[:, ks]).astype(BF16), k_l[:, ks].astype(BF16))
                     for q_l, k_l, _ in levels] for ks in pair_lanes]
    upds = [[_dot_tn(v[:, (2 * p + e) * DV_B:(2 * p + e + 1) * DV_B].astype(BF16), k_dec[:, ks].astype(BF16))
             for e in range(2)] for p, ks in enumerate(pair_lanes)]
    outs, new_state = [], []
    for p, ks in enumerate(pair_lanes):
        sc = jnp.zeros((2 * c, c), F32)
        for s_l, (_, _, region) in zip(level_scores[p], levels):
            sc = jnp.where(region, s_l, sc)
        for s in range(GLA_SUB):
            d_s = jnp.concatenate([sc_d[s * c:(s + 1) * c, (2 * p + e) * DV_B:(2 * p + e) * DV_B + c]
                                   for e in range(2)], axis=0)
            sc = jnp.where(key_in_block == s, d_s, sc)
        sc = sc.astype(BF16)
        for e in range(2):
            hidx = 2 * p + e
            v_h = v[:, hidx * DV_B:(hidx + 1) * DV_B].astype(BF16)
            o_h = o_pairs[p][e * c:(e + 1) * c] + _dot(sc[e * c:(e + 1) * c], v_h)
            ms = jnp.mean(o_h * o_h, axis=-1, keepdims=True)
            outs.append(o_h * lax.rsqrt(ms + EPS) * gw)
        low = lax.broadcasted_iota(jnp.int32, (DV_B, LANES), 1) < DK_B
        new_state.append(state[p] * jnp.exp(b_last[:, ks]) + jnp.where(low, upds[p][0], upds[p][1]))
    return jnp.concatenate(outs, axis=1), new_state


def _gla_kernel(*refs, chunk, n_chunks, n_decode, mix):
    q_ref, k_ref, v_ref, g_ref, s0_ref, gw_ref, bdx_ref = refs[:7]
    pos = 7
    if n_decode:
        dq, dk, dv, kcache, vcache, wc, wn = refs[pos:pos + 7]
        pos += 7
    if mix:
        n_mix = 2 * len(DILATED) + 5
        att, (ga, gb, x_ref, ex_ref, wo_ref) = refs[pos:pos + n_mix - 5], refs[pos + n_mix - 5:pos + n_mix]
        pos += n_mix
    else:
        o_ref = refs[pos]
        pos += 1
    sfin_ref = refs[pos]
    pos += 1
    if n_decode:
        d_out = refs[pos]
        pos += 1
    if mix:
        y_ref = refs[pos]
        pos += 1
    st_ref, slabs = refs[pos:pos + 2]
    if mix:
        o_ref, mix_scratch = refs[pos + 2], refs[pos + 3:]
    step = pl.program_id(1)
    n_seq = q_ref.shape[0]

    @pl.when(step == 0)
    def _():
        for i in range(n_seq):
            for p in range(H_B // 2):
                st_ref[i, p] = s0_ref[i, p * LANES:(p + 1) * LANES, :].T

    if n_decode:
        for i in range(n_decode):
            d_out[i] = _sample_attn(dq[i], dk[i], dv[i], kcache[i], vcache[i], wc[...], wn[...])

    for i in range(n_seq):
        state = [st_ref[i, p] for p in range(H_B // 2)]
        for ci in range(n_chunks):
            rs = slice(ci * chunk, (ci + 1) * chunk)
            o, state = _gla_chunk(q_ref[i, rs, :], k_ref[i, rs, :], v_ref[i, rs, :], g_ref[i, rs, :], state,
                                  slabs.at[i * n_chunks + ci], gw_ref[...], bdx_ref[...])
            o_ref[i, rs, :] = o.astype(o_ref.dtype)
        for p in range(H_B // 2):
            st_ref[i, p] = state[p]

    if mix:
        o_a = _branch_mixture(att, ex_ref, mix_scratch, x_ref.shape[0])
        y_ref[...] = _mix_and_project(o_a, ga[...], o_ref[0], gb[...], x_ref[...], wo_ref[...])

    @pl.when(step == pl.num_programs(1) - 1)
    def _():
        for i in range(n_seq):
            for p in range(H_B // 2):
                sfin_ref[i, p * LANES:(p + 1) * LANES, :] = st_ref[i, p].T


def _gla(qb, kb, vb, la, s0, gw, bdx, decode=None, mix=None):
    bsz, length, _ = qb.shape
    chunk = int(np.gcd(length, GLA_CHUNK))
    n_chunks = GLA_STEP_CHUNKS if length % (GLA_STEP_CHUNKS * chunk) == 0 else 1
    tl = chunk * n_chunks
    n_seq = GLA_SHORT_BATCH if length == tl and bsz % GLA_SHORT_BATCH == 0 else 1
    grid = (bsz // n_seq, length // tl)
    tok = lambda w: pl.BlockSpec((n_seq, tl, w), lambda b, i: (b, i, 0))
    st = pl.BlockSpec((n_seq, H_B * DK_B, DV_B), lambda b, i: (b, 0, 0))
    in_specs = [tok(D_KB), tok(D_KB), tok(D_B), tok(D_KB), st, _const_spec((1, DV_B)), _const_spec((D_KB, D_B))]
    out_specs = [tok(D_B), st]
    out_shape = [jax.ShapeDtypeStruct((bsz, length, D_B), BF16 if tl % 16 == 0 else F32),
                 jax.ShapeDtypeStruct((bsz, H_B * DK_B, DV_B), F32)]
    args = [qb, kb, vb, la, s0, gw, bdx]
    n_decode = 0
    if decode is not None:
        dq, dk, dv, kcache, vcache = decode
        n_dec, t_len, _ = dq.shape
        win = kcache.shape[2]
        assert n_dec % (grid[0] * grid[1]) == 0, "decode sequences must spread evenly over the grid steps"
        n_decode = n_dec // (grid[0] * grid[1])
        wc, wn = _sample_key_weights(win, t_len)
        by_step = lambda b, i: (b * grid[1] + i, 0, 0)
        new = pl.BlockSpec((n_decode, t_len, D_A), by_step)
        cache = pl.BlockSpec((n_decode, D_A, win), by_step)
        in_specs += [new, new, new, cache, cache, _const_spec(wc.shape), _const_spec(wn.shape)]
        out_specs.append(new)
        out_shape.append(jax.ShapeDtypeStruct((n_dec, t_len, D_A), F32))
        args += [dq, dk, dv, kcache, vcache, jnp.asarray(wc), jnp.asarray(wn)]
    scratch = [pltpu.VMEM((n_seq, H_B // 2, DV_B, LANES), F32),
               pltpu.VMEM((n_seq * n_chunks, 2 * D_KB // LANES, chunk, LANES), F32)]
    if mix is not None:
        att, ga, gb, x, ex, wo = mix
        assert n_seq == 1
        d_model = x.shape[-1]
        nat_spec = lambda w: pl.BlockSpec((None, tl, w), lambda b, i: (b, i, 0))
        cls_spec = lambda dil, w: pl.BlockSpec((None, dil, tl // dil, w), lambda b, i: (b, 0, i, 0))
        mix_scratch = []
        for (_, dil), (o, lse) in zip(DILATED, att):
            in_specs += [cls_spec(dil, D_A), cls_spec(dil, LANES)]
            args += [o, lse]
            if dil > 1:
                mix_scratch += [pltpu.VMEM((D_A // LANES, tl, LANES), F32), pltpu.VMEM((tl, LANES), F32)]
        in_specs += [nat_spec(D_A), nat_spec(D_B), nat_spec(d_model), _const_spec(ex.shape), _const_spec(wo.shape)]
        args += [ga, gb, x, ex, wo]
        scratch += [pltpu.VMEM((n_seq, tl, D_B), BF16)] + mix_scratch
        out_specs = out_specs[1:] + [nat_spec(d_model)]
        out_shape = out_shape[1:] + [jax.ShapeDtypeStruct(x.shape, F32)]
    return pl.pallas_call(
        functools.partial(_gla_kernel, chunk=chunk, n_chunks=n_chunks, n_decode=n_decode, mix=mix is not None),
        grid=grid,
        in_specs=in_specs,
        out_specs=out_specs,
        out_shape=out_shape,
        scratch_shapes=scratch,
        compiler_params=pltpu.CompilerParams(dimension_semantics=("arbitrary", "arbitrary"),
                                             vmem_limit_bytes=VMEM_LIMIT),
        name="gla" + ("_decode_attn" if n_decode else "") + ("_out" if mix is not None else ""),
    )(*args)


def _mix_and_project(o_a, ga, o_b, gb, x, wo):
    mixed = jnp.concatenate([o_a * ga, o_b * gb], axis=1).astype(BF16)
    return x + _dot(mixed, wo)


def _branch_mixture(att, ex_ref, scratch, tm):
    scratch = list(scratch)
    outs, lses = [], []
    for bi, (_, dil) in enumerate(DILATED):
        o_ref, l_ref = att[2 * bi], att[2 * bi + 1]
        if dil == 1:
            outs.append(o_ref[0].astype(F32))
            lses.append(l_ref[0])
            continue
        oslab, lslab = scratch.pop(0), scratch.pop(0)
        for c in range(dil):
            rows = pl.ds(c, tm // dil, stride=dil)
            for j in range(D_A // LANES):
                oslab[j, rows, :] = o_ref[c, :, j * LANES:(j + 1) * LANES].astype(F32)
            lslab[rows, :] = l_ref[c]
        outs.append(jnp.concatenate([oslab[j] for j in range(D_A // LANES)], axis=1))
        lses.append(lslab[...])
    m = functools.reduce(jnp.maximum, lses)
    es = [jnp.exp(l - m) for l in lses]
    inv = 1.0 / functools.reduce(lambda a, b: a + b, es)
    spread = lambda a: _dot((a * inv).astype(BF16), ex_ref[...])
    return functools.reduce(lambda a, b: a + b, [spread(e) * (o - outs[-1]) for e, o in zip(es[:-1], outs[:-1])],
                            outs[-1])


def _out_sample_kernel(oa, ga, ob, gb, x_ref, wo_ref, y_ref):
    y_ref[...] = _mix_and_project(oa[...], ga[...], ob[...], gb[...], x_ref[...], wo_ref[...])


def _out_sample(o_a, ga, o_b, gb, x2, p):
    n, d_model = x2.shape
    tm = 256 if n % 256 == 0 else n
    row = lambda w: pl.BlockSpec((tm, w), lambda i: (i, 0))
    return pl.pallas_call(
        _out_sample_kernel,
        grid=(n // tm,),
        in_specs=[row(D_A), row(D_A), row(D_B), row(D_B), row(d_model), _const_spec((D_A + D_B, d_model))],
        out_specs=row(d_model),
        out_shape=jax.ShapeDtypeStruct((n, d_model), F32),
        compiler_params=pltpu.CompilerParams(dimension_semantics=("arbitrary",), vmem_limit_bytes=VMEM_LIMIT),
        name="out_sample",
    )(o_a, ga, o_b, gb, x2, p["wo"])


def _layer_params(norm_w, w_in, w_gate_up, b_gate, q_norm_w, k_norm_w, gla_norm_w, w_out):
    d_model = w_in.shape[0]
    split = np.cumsum((D_A, D_A, D_A, D_KB, D_KB, D_B, GATE_RANK, D_A, D_B))
    glr = w_in[:, split[5]:split[6]]
    w = jnp.concatenate([w_in[:, :split[5]], glr, jnp.zeros((d_model, LANES - GATE_RANK), w_in.dtype),
                         w_in[:, split[6]:]], axis=1).astype(BF16)
    head_of = np.arange(D_A) // HD_A
    bd = jnp.asarray(head_of[:MXU_DIM, None] == head_of[None, :MXU_DIM], BF16)
    ex = jnp.asarray(np.arange(LANES)[:, None] == head_of[None, :], BF16)
    bdx = jnp.asarray((np.arange(D_KB) // DK_B)[:, None] == (np.arange(D_B) // DV_B)[None, :], BF16)
    wgu = jnp.concatenate([w_gate_up, jnp.zeros((LANES - GATE_RANK, D_KB), w_gate_up.dtype)], axis=0).astype(BF16)
    return dict(nw=norm_w.reshape(1, d_model), w=w, bd=bd, ex=ex, bdx=bdx,
                wq=jnp.tile(q_norm_w, H_A).reshape(1, D_A), wk=jnp.tile(k_norm_w, H_A).reshape(1, D_A),
                wgu=wgu, bg=b_gate.reshape(1, D_KB), gw=gla_norm_w.reshape(1, DV_B), wo=w_out.astype(BF16))


def _layer(xp, xs, p, keep, k_cache, v_cache, s0):
    bsz, seq, _ = xp.shape
    bs, t_len, d_model = xs.shape
    win = k_cache.shape[1]
    rs = _proj_sample(xs.reshape(bs * t_len, d_model), PAST_LEN + jnp.tile(jnp.arange(t_len), bs), p)
    tok = lambda a: a.reshape(bs, t_len, a.shape[-1])
    by_pos = lambda a: a.transpose(0, 2, 3, 1).reshape(bs, D_A, win)

    r = _proj_prompt(xp, p, keep)
    by_class = lambda a, dil: a.reshape(bsz, dil, seq // dil, D_A)
    att = [_band_attn(*(by_class(r[f"{n}{dil}"], dil) for n in "qkv")) for _, dil in DILATED]
    s_fin, o_as, yp = _gla(r["qb"], r["kb"], r["vb"], r["la"], jnp.zeros((bsz, H_B * DK_B, DV_B), F32),
                           p["gw"], p["bdx"],
                           decode=(tok(rs["q"]), tok(rs["k"]), tok(rs["v"]), by_pos(k_cache), by_pos(v_cache)),
                           mix=(att, r["ga"], r["gb"], xp, p["ex"], p["wo"]))
    window = lambda a: a.reshape(bsz, H_A, HD_A, keep).transpose(0, 3, 1, 2)

    o_bs, s_fin_s = _gla(tok(rs["qb"]), tok(rs["kb"]), tok(rs["vb"]), tok(rs["la"]),
                         s0.reshape(bs, H_B * DK_B, DV_B), p["gw"], p["bdx"])
    ys = _out_sample(o_as.reshape(bs * t_len, D_A), rs["ga"], o_bs.reshape(bs * t_len, D_B), rs["gb"],
                     xs.reshape(bs * t_len, d_model), p)
    heads = lambda a: a.reshape(bs, t_len, H_A, HD_A)
    state = lambda a, n: a.reshape(n, H_B, DK_B, DV_B)
    return (yp, ys.reshape(bs, t_len, d_model), window(r["kt"]), window(r["vt"]), state(s_fin, bsz),
            heads(rs["k"]), heads(rs["v"]), state(s_fin_s, bs))


def kernel(x_prompt, x_sample, cache_k_win, cache_v_win, state_gla, norm_w, w_in, w_gate_up, b_gate,
           q_norm_w, k_norm_w, gla_norm_w, w_out):
    seq = x_prompt.shape[1]
    assert seq % (MAX_DIL * BAND) == 0, "prompt length must tile into residue-class bands"
    assert all(w // d == BAND and MAX_DIL % d == 0 for w, d in DILATED)
    keep = min(MAX_WINDOW, seq)
    hp, hs = x_prompt, x_sample
    outs = [[] for _ in range(6)]
    for layer in range(w_in.shape[0]):
        p = _layer_params(norm_w[layer], w_in[layer], w_gate_up[layer], b_gate[layer], q_norm_w[layer],
                          k_norm_w[layer], gla_norm_w[layer], w_out[layer])
        hp, hs, kp, vp, st_p, kn, vn, st_s = _layer(hp, hs, p, keep, cache_k_win[layer], cache_v_win[layer],
                                                    state_gla[layer])
        for lst, val in zip(outs, (kp, vp, st_p.astype(state_gla.dtype), kn, vn, st_s.astype(state_gla.dtype))):
            lst.append(val)
    return (hp, hs) + tuple(jnp.stack(o) for o in outs)
```
